```python
import jax, jax.numpy as jnp
from jax import lax
import numpy as np

D_MODEL = 1024
BATCH = 8
SEQ = 8192
DEPTH = 1
DEC_BATCH = 128
DEC_SEQ = 4
PAST_LEN = 8192
PAGE_SIZE = 128

HEAD_DIM = 64
N_Q_HEADS = 8
N_KV_HEADS = 2
Q_PER_KV = N_Q_HEADS // N_KV_HEADS
ATTN_W = N_Q_HEADS * HEAD_DIM
KV_W = N_KV_HEADS * HEAD_DIM
ATTN_SCALE = HEAD_DIM ** -0.5
N_IDX_HEADS = 8
D_IDX = 64
IDX_SCALE = (N_IDX_HEADS * D_IDX) ** -0.5
TOPK_MAX = 256
Q_BLOCK = 128
RWKV_HEAD = 64
N_RWKV_HEADS = 8
RWKV_W = N_RWKV_HEADS * RWKV_HEAD
W_LORA = 64
A_LORA = 64
G_LORA = 128
RWKV_PROJ_W = 3 * RWKV_W + W_LORA + A_LORA + G_LORA
LNX_EPS = 64e-5
IN_SPLITS = (ATTN_W, KV_W, KV_W, N_IDX_HEADS * D_IDX, D_IDX, N_IDX_HEADS, RWKV_PROJ_W, D_MODEL, D_MODEL)
IN_W = ATTN_W + 2 * KV_W + N_IDX_HEADS * D_IDX + D_IDX + N_IDX_HEADS + RWKV_PROJ_W + 2 * D_MODEL
N_EXPERTS = 32
TOP_K = 4
D_FF = 1024
SWIGLU_LIMIT = 7.0
SWIGLU_ALPHA = 1.702
MOE_BLOCK = 128
NORM_EPS = 1e-6

kernel_name = 'dsa_rwkv7_moe_hybrid_step'


def rmsnorm(x, g):
    xf = x.astype(jnp.float32)
    y = xf * lax.rsqrt(jnp.mean(xf * xf, axis=-1, keepdims=True) + NORM_EPS)
    return (y * g.astype(jnp.float32)).astype(x.dtype)


def in_proj(x, norm_mix, w_in, q_norm, k_norm, idx_k_norm):
    b, t, _ = x.shape
    cuts = np.cumsum(IN_SPLITS)[:-1].tolist()
    q, k, v, iq, ik, iw, pr, ga, gb = jnp.split(rmsnorm(x, norm_mix) @ w_in, cuts, axis=-1)
    q = rmsnorm(q.reshape(b, t, N_Q_HEADS, HEAD_DIM), q_norm)
    k = rmsnorm(k.reshape(b, t, N_KV_HEADS, HEAD_DIM), k_norm)
    v = v.reshape(b, t, N_KV_HEADS, HEAD_DIM)
    iq = iq.reshape(b, t, N_IDX_HEADS, D_IDX)
    ik = rmsnorm(ik, idx_k_norm)
    return q, k, v, iq, ik, iw, pr, ga, gb


def index_scores(iq, iw, ik):
    dots = jnp.einsum('bqhd,bld->bqhl', iq, ik).astype(jnp.float32)
    return jnp.einsum('bqhl,bqh->bql', jax.nn.relu(dots), iw.astype(jnp.float32)) * IDX_SCALE


def select_keys(scores, q_pos, n_keys):
    top = min(TOPK_MAX, n_keys // 4)
    causal = jnp.arange(n_keys)[None, None, :] <= q_pos[None, :, None]
    _, sel = lax.top_k(jnp.where(causal, scores, -jnp.inf), top)
    return sel, sel <= q_pos[None, :, None]


def sparse_attend(q, kb, vb, valid):
    b, nq = q.shape[:2]
    qg = q.reshape(b, nq, N_KV_HEADS, Q_PER_KV, HEAD_DIM)
    s = jnp.einsum('bqkgd,bqtkd->bqkgt', qg, kb).astype(jnp.float32) * ATTN_SCALE
    s = jnp.where(valid[:, :, None, None, :], s, -jnp.inf)
    p = jax.nn.softmax(s, axis=-1).astype(vb.dtype)
    o = jnp.einsum('bqkgt,bqtkd->bqkgd', p, vb)
    return o.reshape(b, nq, ATTN_W)


def dsa_prompt(q, k, v, iq, iw, ik):
    b, s = q.shape[:2]
    bidx = jnp.arange(b)[:, None, None]

    def q_block(i):
        s0 = i * Q_BLOCK
        sl = lambda a: lax.dynamic_slice_in_dim(a, s0, Q_BLOCK, axis=1)
        q_pos = s0 + jnp.arange(Q_BLOCK)
        sel, valid = select_keys(index_scores(sl(iq), sl(iw), ik), q_pos, s)
        return sparse_attend(sl(q), k[bidx, sel], v[bidx, sel], valid)

    out = lax.map(q_block, jnp.arange(s // Q_BLOCK))
    return jnp.swapaxes(out, 0, 1).reshape(b, s, ATTN_W)


def dsa_sample(q, k, v, iq, iw, ik, ck, cv, cik, page_table):
    db, t = q.shape[:2]
    past = page_table.shape[1] * PAGE_SIZE
    n_keys = past + t
    ik_all = jnp.concatenate([cik[page_table].reshape(db, past, D_IDX), ik], axis=1)
    q_pos = past + jnp.arange(t)
    sel, valid = select_keys(index_scores(iq, iw, ik_all), q_pos, n_keys)
    bidx = jnp.arange(db)[:, None, None]
    psel = jnp.minimum(sel, past - 1)
    phys = page_table[bidx, psel // PAGE_SIZE]
    slot = psel % PAGE_SIZE
    nsel = jnp.clip(sel - past, 0, t - 1)
    is_new = (sel >= past)[..., None, None]
    kb = jnp.where(is_new, k[bidx, nsel], ck[phys, slot])
    vb = jnp.where(is_new, v[bidx, nsel], cv[phys, slot])
    return sparse_attend(q, kb, vb, valid)


def rwkv_mix(pr, pr_prev, wkv0, shift_mu, w0, w_lora_up, a0, a_lora_up, g_lora_up, k_k, k_a, r_k, ln_x_w, ln_x_b):
    b, t, _ = pr.shape
    m = pr + shift_mu * (pr_prev - pr)
    cuts = [RWKV_W, 2 * RWKV_W, 3 * RWKV_W, 3 * RWKV_W + W_LORA, 3 * RWKV_W + W_LORA + A_LORA]
    r, k, v, wd, ad, gd = jnp.split(m, cuts, axis=-1)
    wlog = -jax.nn.softplus(-(w0 + jnp.tanh(wd) @ w_lora_up)) - 0.5
    decay = jnp.exp(-jnp.exp(wlog.astype(jnp.float32)))
    a = jax.nn.sigmoid(a0 + ad @ a_lora_up)
    g = jax.nn.sigmoid(gd) @ g_lora_up
    hv = lambda z: z.reshape(b, t, N_RWKV_HEADS, RWKV_HEAD).astype(jnp.float32)
    kk = hv(k * k_k)
    kk = kk / jnp.maximum(jnp.sqrt(jnp.sum(kk * kk, axis=-1, keepdims=True)), 1e-12)
    k = k * (1.0 + (a - 1.0) * k_a)
    rh, kh, vh, ah, wh = hv(r), hv(k), hv(v), hv(a), hv(decay)
    bh = kk * ah

    def step(S, inp):
        rt, wt, kt, vt, kkt, bt = inp
        sa = jnp.einsum('bhij,bhj->bhi', S, -kkt)
        S = S * wt[:, :, None, :] + sa[..., None] * bt[:, :, None, :] + vt[..., None] * kt[:, :, None, :]
        return S, jnp.einsum('bhij,bhj->bhi', S, rt)

    tm = lambda z: jnp.moveaxis(z, 1, 0)
    s_fin, y = lax.scan(step, wkv0.astype(jnp.float32), (tm(rh), tm(wh), tm(kh), tm(vh), tm(kk), tm(bh)))
    y = jnp.moveaxis(y, 0, 1)
    mu = jnp.mean(y, axis=-1, keepdims=True)
    var = jnp.mean(jnp.square(y - mu), axis=-1, keepdims=True)
    y = (y - mu) * lax.rsqrt(var + LNX_EPS) * ln_x_w.reshape(N_RWKV_HEADS, RWKV_HEAD) + ln_x_b.reshape(N_RWKV_HEADS, RWKV_HEAD)
    y = y + jnp.sum(rh * kh * r_k, axis=-1, keepdims=True) * vh
    out = (y.reshape(b, t, RWKV_W) * g.astype(jnp.float32)).astype(pr.dtype)
    return out, s_fin


def moe(x2d, router_w, router_b, w_gate_up, b_gate_up, w_down, b_down):
    n, d = x2d.shape
    logits = (x2d @ router_w + router_b).astype(jnp.float32)
    top_val, top_idx = lax.top_k(logits, TOP_K)
    gate = jax.nn.softmax(top_val, axis=-1)
    nk = n * TOP_K
    e_flat = top_idx.reshape(-1)
    tok_flat = jnp.arange(nk, dtype=jnp.int32) // TOP_K
    order = jnp.argsort(e_flat)
    e_sorted = e_flat[order]
    counts = jnp.bincount(e_flat, length=N_EXPERTS)
    starts = jnp.cumsum(counts) - counts
    padded = (counts + MOE_BLOCK - 1) // MOE_BLOCK * MOE_BLOCK
    pends = jnp.cumsum(padded)
    pstarts = pends - padded
    dest = pstarts[e_sorted] + jnp.arange(nk) - starts[e_sorted]
    n_blocks = -(-nk // MOE_BLOCK) + N_EXPERTS
    rows = n_blocks * MOE_BLOCK
    row_tok = jnp.full((rows,), n, jnp.int32).at[dest].set(tok_flat[order])
    row_gate = jnp.zeros((rows,), jnp.float32).at[dest].set(gate.reshape(-1)[order])
    block_exp = jnp.clip(jnp.searchsorted(pends, jnp.arange(n_blocks) * MOE_BLOCK, side='right'), 0, N_EXPERTS - 1)
    x_pad = jnp.concatenate([x2d, jnp.zeros((1, d), x2d.dtype)], axis=0)

    def expert_block(args):
        toks, e = args
        h = x_pad[toks] @ w_gate_up[e] + b_gate_up[e]
        glu, lin = jnp.split(h, 2, axis=-1)
        glu = jnp.minimum(glu, SWIGLU_LIMIT)
        lin = jnp.clip(lin, -SWIGLU_LIMIT, SWIGLU_LIMIT)
        act = glu * jax.nn.sigmoid(SWIGLU_ALPHA * glu) * (lin + 1.0)
        return act @ w_down[e] + b_down[e]

    out = lax.map(expert_block, (row_tok.reshape(n_blocks, MOE_BLOCK), block_exp)).reshape(rows, d)
    y = jnp.zeros((n + 1, d), x2d.dtype).at[row_tok].add((out * row_gate[:, None]).astype(x2d.dtype))
    return y[:n]


def merge_and_ffn(x, oa, ob, ga, gb, w_proj_a, w_proj_b, w_out, norm_ffn, router_w, router_b, w_gate_up, b_gate_up, w_down, b_down):
    m = jax.nn.sigmoid(ga) * (oa @ w_proj_a) + jax.nn.sigmoid(gb) * (ob @ w_proj_b)
    h = x + m @ w_out
    b, t, d = h.shape
    f = moe(rmsnorm(h, norm_ffn).reshape(b * t, d), router_w, router_b, w_gate_up, b_gate_up, w_down, b_down)
    return h + f.reshape(b, t, d)


def decoder_layer(xp, xs, ck, cv, cik, page_table, swkv, sshift, norm_mix, w_in, q_norm, k_norm, idx_k_norm, shift_mu, w0, w_lora_up, a0, a_lora_up, g_lora_up, k_k, k_a, r_k, ln_x_w, ln_x_b, w_proj_a, w_proj_b, w_out, norm_ffn, router_w, router_b, w_gate_up, b_gate_up, w_down, b_down):
    proj = (norm_mix, w_in, q_norm, k_norm, idx_k_norm)
    rw = (shift_mu, w0, w_lora_up, a0, a_lora_up, g_lora_up, k_k, k_a, r_k, ln_x_w, ln_x_b)
    tail = (w_proj_a, w_proj_b, w_out, norm_ffn, router_w, router_b, w_gate_up, b_gate_up, w_down, b_down)
    q, k, v, iq, ik, iw, pr, ga, gb = in_proj(xp, *proj)
    oa = dsa_prompt(q, k, v, iq, iw, ik)
    pr_prev = jnp.concatenate([jnp.zeros_like(pr[:, :1]), pr[:, :-1]], axis=1)
    wkv0 = jnp.zeros((xp.shape[0], N_RWKV_HEADS, RWKV_HEAD, RWKV_HEAD), jnp.float32)
    ob, wkv_p = rwkv_mix(pr, pr_prev, wkv0, *rw)
    yp = merge_and_ffn(xp, oa, ob, ga, gb, *tail)
    qs, ks, vs, iqs, iks, iws, prs, gas, gbs = in_proj(xs, *proj)
    oas = dsa_sample(qs, ks, vs, iqs, iws, iks, ck, cv, cik, page_table)
    prs_prev = jnp.concatenate([sshift[:, None, :].astype(prs.dtype), prs[:, :-1]], axis=1)
    obs, wkv_s = rwkv_mix(prs, prs_prev, swkv, *rw)
    ys = merge_and_ffn(xs, oas, obs, gas, gbs, *tail)
    return yp, ys, (k, v, ik, wkv_p.astype(xp.dtype), pr[:, -1], ks, vs, iks, wkv_s.astype(xs.dtype), prs[:, -1])


def setup_inputs(seed: int = 0) -> dict:
    key = jax.random.key(seed)
    keys = iter(jax.random.split(key, 48))
    nrm = lambda shape, scale: jax.random.normal(next(keys), shape, jnp.float32) * scale
    gain = lambda shape: 1.0 + nrm(shape, 0.02)
    L = DEPTH
    n_pages = PAST_LEN // PAGE_SIZE
    n_used = DEC_BATCH * n_pages
    n_pool = n_used + (n_used + 3) // 4
    perm = jax.random.permutation(next(keys), n_pool)
    page_table = perm[:n_used].reshape(DEC_BATCH, n_pages).astype(jnp.int32)
    return {
        'x_prompt': nrm((BATCH, SEQ, D_MODEL), 1.0),
        'x_sample': nrm((DEC_BATCH, DEC_SEQ, D_MODEL), 1.0),
        'cache_k': nrm((L, n_pool, PAGE_SIZE, N_KV_HEADS, HEAD_DIM), 1.0),
        'cache_v': nrm((L, n_pool, PAGE_SIZE, N_KV_HEADS, HEAD_DIM), 1.0),
        'cache_idx_k': nrm((L, n_pool, PAGE_SIZE, D_IDX), 1.0),
        'page_table': page_table,
        'state_wkv': nrm((L, DEC_BATCH, N_RWKV_HEADS, RWKV_HEAD, RWKV_HEAD), 0.3),
        'state_shift': nrm((L, DEC_BATCH, RWKV_PROJ_W), 1.0),
        'norm_mix': gain((L, D_MODEL)),
        'w_in': nrm((L, D_MODEL, IN_W), D_MODEL ** -0.5),
        'q_norm': gain((L, HEAD_DIM)),
        'k_norm': gain((L, HEAD_DIM)),
        'idx_k_norm': gain((L, D_IDX)),
        'shift_mu': jax.random.uniform(next(keys), (L, RWKV_PROJ_W), jnp.float32),
        'w0': jax.random.uniform(next(keys), (L, RWKV_W), jnp.float32, minval=-6.0, maxval=1.0),
        'w_lora_up': nrm((L, W_LORA, RWKV_W), 0.1),
        'a0': nrm((L, RWKV_W), 0.3),
        'a_lora_up': nrm((L, A_LORA, RWKV_W), A_LORA ** -0.5),
        'g_lora_up': nrm((L, G_LORA, RWKV_W), G_LORA ** -0.5),
        'k_k': 0.85 + nrm((L, RWKV_W), 0.02),
        'k_a': gain((L, RWKV_W)),
        'r_k': nrm((L, N_RWKV_HEADS, RWKV_HEAD), 0.1),
        'ln_x_w': gain((L, RWKV_W)),
        'ln_x_b': nrm((L, RWKV_W), 0.01),
        'w_proj_a': nrm((L, ATTN_W, D_MODEL), ATTN_W ** -0.5),
        'w_proj_b': nrm((L, RWKV_W, D_MODEL), RWKV_W ** -0.5),
        'w_out': nrm((L, D_MODEL, D_MODEL), D_MODEL ** -0.5),
        'norm_ffn': gain((L, D_MODEL)),
        'router_w': nrm((L, D_MODEL, N_EXPERTS), D_MODEL ** -0.5),
        'router_b': nrm((L, N_EXPERTS), 0.01),
        'w_gate_up': nrm((L, N_EXPERTS, D_MODEL, 2 * D_FF), D_MODEL ** -0.5),
        'b_gate_up': nrm((L, N_EXPERTS, 2 * D_FF), 0.01),
        'w_down': nrm((L, N_EXPERTS, D_FF, D_MODEL), D_FF ** -0.5),
        'b_down': nrm((L, N_EXPERTS, D_MODEL), 0.01),
    }


def reference(x_prompt, x_sample, cache_k, cache_v, cache_idx_k, page_table, state_wkv, state_shift, norm_mix, w_in, q_norm, k_norm, idx_k_norm, shift_mu, w0, w_lora_up, a0, a_lora_up, g_lora_up, k_k, k_a, r_k, ln_x_w, ln_x_b, w_proj_a, w_proj_b, w_out, norm_ffn, router_w, router_b, w_gate_up, b_gate_up, w_down, b_down):
    params = (norm_mix, w_in, q_norm, k_norm, idx_k_norm, shift_mu, w0, w_lora_up, a0, a_lora_up, g_lora_up, k_k, k_a, r_k, ln_x_w, ln_x_b, w_proj_a, w_proj_b, w_out, norm_ffn, router_w, router_b, w_gate_up, b_gate_up, w_down, b_down)
    yp, ys = x_prompt, x_sample
    new = [[] for _ in range(10)]
    for layer in range(DEPTH):
        yp, ys, st = decoder_layer(yp, ys, cache_k[layer], cache_v[layer], cache_idx_k[layer], page_table, state_wkv[layer], state_shift[layer], *[p[layer] for p in params])
        for lst, s in zip(new, st):
            lst.append(s)
    k_p, v_p, ik_p, wkv_p, sh_p, k_s, v_s, ik_s, wkv_s, sh_s = [jnp.stack(s, axis=0) for s in new]
    return (yp, ys, k_p, v_p, ik_p, wkv_p, sh_p, k_s, v_s, ik_s, wkv_s, sh_s)
```

```python
import functools

import jax
import jax.numpy as jnp
import numpy as np
from jax import lax
from jax.experimental import pallas as pl
from jax.experimental.pallas import tpu as pltpu

D_MODEL = 1024
PAGE_SIZE = 128
HEAD_DIM = 64
N_Q_HEADS = 8
N_KV_HEADS = 2
Q_PER_KV = N_Q_HEADS // N_KV_HEADS
ATTN_W = N_Q_HEADS * HEAD_DIM
KV_W = N_KV_HEADS * HEAD_DIM
ATTN_SCALE = HEAD_DIM ** -0.5
N_IDX_HEADS = 8
D_IDX = 64
IDX_W = N_IDX_HEADS * D_IDX
IDX_SCALE = (N_IDX_HEADS * D_IDX) ** -0.5
TOPK_MAX = 256
RWKV_HEAD = 64
N_RWKV_HEADS = 8
RWKV_W = N_RWKV_HEADS * RWKV_HEAD
W_LORA = 64
A_LORA = 64
G_LORA = 128
RWKV_PROJ_W = 3 * RWKV_W + W_LORA + A_LORA + G_LORA
LNX_EPS = 64e-5
N_EXPERTS = 32
TOP_K = 4
D_FF = 1024
SWIGLU_LIMIT = 7.0
SWIGLU_ALPHA = 1.702
NORM_EPS = 1e-6

LANES = 128
VMEM_LIMIT = 56 * 1024 * 1024
INT_MIN = -(2 ** 31)
NEG_BIG = -1e30
F32 = jnp.float32
BF16 = jnp.bfloat16
HI = lax.Precision.HIGHEST


def _dot(a, b, precision=None):
    return jnp.dot(a, b, preferred_element_type=F32, precision=precision)


def _dot_nt(a, b, precision=None):
    return lax.dot_general(a, b, (((1,), (1,)), ((), ())), preferred_element_type=F32, precision=precision)


def _dot_tn(a, b, precision=None):
    return lax.dot_general(a, b, (((0,), (0,)), ((), ())), preferred_element_type=F32, precision=precision)


def _group_ones(width, group):
    r = np.arange(width) // group
    return jnp.asarray((r[:, None] == r[None, :]).astype(np.float32))


def _const_spec(shape):
    nd = len(shape)
    return pl.BlockSpec(shape, lambda *_: (0,) * nd)


def _inproj_kernel(x_ref, g_ref, wq_ref, wk_ref, wv_ref, wiq_ref, wikw_ref, wpr_ref, wga_ref, wgb_ref,
                   qn_ref, kn_ref, ikn_ref, g512_ref, g128_ref,
                   q_out, k_out, v_out, iq_out, ikw_out, pr_out, ga_out, gb_out):
    x = x_ref[...]
    ms = jnp.mean(x * x, axis=-1, keepdims=True)
    xn = (x * lax.rsqrt(ms + NORM_EPS) * g_ref[...]).astype(BF16)

    q = _dot(xn, wq_ref[...])
    qs = _dot(q * q, g512_ref[...], HI) * (1.0 / HEAD_DIM)
    q_out[...] = (q * lax.rsqrt(qs + NORM_EPS) * qn_ref[...]).astype(q_out.dtype)

    k = _dot(xn, wk_ref[...])
    ks = _dot(k * k, g128_ref[...], HI) * (1.0 / HEAD_DIM)
    k_out[...] = k * lax.rsqrt(ks + NORM_EPS) * kn_ref[...]

    v_out[...] = _dot(xn, wv_ref[...])
    iq_out[...] = _dot(xn, wiq_ref[...]).astype(iq_out.dtype)

    ikw = _dot(xn, wikw_ref[...])
    lane = lax.broadcasted_iota(jnp.int32, ikw.shape, 1)
    is_ik = lane < D_IDX
    iks = jnp.sum(jnp.where(is_ik, ikw * ikw, 0.0), axis=-1, keepdims=True) * (1.0 / D_IDX)
    ikw_out[...] = jnp.where(is_ik, ikw * lax.rsqrt(iks + NORM_EPS) * ikn_ref[...], ikw)

    pr_out[...] = _dot(xn, wpr_ref[...])
    ga_out[...] = jax.nn.sigmoid(_dot(xn, wga_ref[...])).astype(ga_out.dtype)
    gb_out[...] = jax.nn.sigmoid(_dot(xn, wgb_ref[...])).astype(gb_out.dtype)


def _in_proj(x2d, p, tm):
    n = x2d.shape[0]
    widths = (ATTN_W, KV_W, KV_W, IDX_W, LANES, RWKV_PROJ_W, D_MODEL, D_MODEL)
    dtypes = (BF16, F32, F32, BF16, F32, F32, BF16, BF16)
    row = lambda w: pl.BlockSpec((tm, w), lambda i: (i, 0))
    consts = (p["norm_mix"], p["wq"], p["wk"], p["wv"], p["wiq"], p["wikw"], p["wpr"], p["wga"], p["wgb"],
              p["q_norm_t"], p["k_norm_t"], p["ik_norm_t"], p["g512"], p["g128"])
    return pl.pallas_call(
        _inproj_kernel,
        grid=(n // tm,),
        in_specs=[row(D_MODEL)] + [_const_spec(c.shape) for c in consts],
        out_specs=[row(w) for w in widths],
        out_shape=[jax.ShapeDtypeStruct((n, w), d) for w, d in zip(widths, dtypes)],
        compiler_params=pltpu.CompilerParams(dimension_semantics=("parallel",), vmem_limit_bytes=VMEM_LIMIT),
        name="in_proj",
    )(x2d, *consts)


def _dsa_kernel(iq_ref, iw_ref, q_ref, ikt_ref, kt_ref, v_ref, o_ref, keys_ref, *, tq, tk, q_offset, n_keys, top):
    qi = pl.program_id(1)
    q_base = q_offset + qi * tq
    n_kt = jnp.minimum((q_base + tq + tk - 1) // tk, n_keys // tk)
    q_pos = q_base + lax.broadcasted_iota(jnp.int32, (tq, 1), 0)
    lane_pos = lax.broadcasted_iota(jnp.int32, (tq, tk), 1)

    iq = iq_ref[0].reshape(N_IDX_HEADS * tq, D_IDX)
    iw = iw_ref[0]

    def score_tile(kt, carry):
        start = pl.multiple_of(kt * tk, tk)
        d = _dot(iq, ikt_ref[0, :, pl.ds(start, tk)])
        acc = jnp.zeros((tq, tk), F32)
        for h in range(N_IDX_HEADS):
            acc = acc + jnp.maximum(d[h * tq:(h + 1) * tq], 0.0) * iw[:, h:h + 1]
        sc = acc * IDX_SCALE
        sc = jnp.where(sc == 0.0, 0.0, sc)
        bits = pltpu.bitcast(sc, jnp.int32)
        key = bits ^ ((bits >> 31) & 0x7FFFFFFF)
        key = jnp.where(start + lane_pos <= q_pos, key, INT_MIN)
        keys_ref[:, pl.ds(start, tk)] = key
        return carry

    lax.fori_loop(0, n_kt, score_tile, 0)

    def count(pred):
        def body(kt, c):
            start = pl.multiple_of(kt * tk, tk)
            m = pred(keys_ref[:, pl.ds(start, tk)], start).astype(jnp.int32)
            part = m[:, 0:LANES]
            for j in range(1, tk // LANES):
                part = part + m[:, j * LANES:(j + 1) * LANES]
            return c + part
        c = lax.fori_loop(0, n_kt, body, jnp.zeros((tq, LANES), jnp.int32))
        return jnp.sum(c, axis=-1, keepdims=True)

    c0 = count(lambda k, s: k >= 0)
    t = jnp.where(c0 >= top, 0, INT_MIN).astype(jnp.int32)

    def bit_step(i, t):
        cand = t | jnp.left_shift(jnp.int32(1), 30 - i)
        c = count(lambda k, s: k >= cand)
        return jnp.where(c >= top, cand, t)

    t = lax.fori_loop(0, 31, bit_step, t)
    t = jnp.maximum(t, INT_MIN + 1)

    c_ge = count(lambda k, s: k >= t)
    excess = c_ge > top

    @pl.when(jnp.max(excess.astype(jnp.int32)) > 0)
    def _():
        keep = top - count(lambda k, s: k > t)

        def idx_step(i, lim):
            cand = lim | jnp.left_shift(jnp.int32(1), 14 - i)
            c = count(lambda k, s: ((k == t) & (s + lane_pos < cand)))
            return jnp.where(c <= keep, cand, lim)

        lim = lax.fori_loop(0, 15, idx_step, jnp.zeros((tq, 1), jnp.int32))

        def demote(kt, carry):
            start = pl.multiple_of(kt * tk, tk)
            k = keys_ref[:, pl.ds(start, tk)]
            drop = (k == t) & (start + lane_pos >= lim) & excess
            keys_ref[:, pl.ds(start, tk)] = jnp.where(drop, t - 1, k)
            return carry

        lax.fori_loop(0, n_kt, demote, 0)

    gq = Q_PER_KV * tq
    qs = [q_ref[0, g * Q_PER_KV:(g + 1) * Q_PER_KV].reshape(gq, HEAD_DIM) for g in range(N_KV_HEADS)]

    def attn_tile(kt, carry):
        start = pl.multiple_of(kt * tk, tk)
        sel = keys_ref[:, pl.ds(start, tk)] >= t
        new = []
        for g in range(N_KV_HEADS):
            m_i, l_i, acc = carry[g]
            s = _dot(qs[g], kt_ref[0, g, :, pl.ds(start, tk)])
            s = jnp.where(sel[None], s.reshape(Q_PER_KV, tq, tk), NEG_BIG).reshape(gq, tk)
            m_n = jnp.maximum(m_i, jnp.max(s, axis=-1, keepdims=True))
            alpha = jnp.exp(m_i - m_n)
            pm = jnp.exp(s - m_n)
            l_n = alpha * l_i + jnp.sum(pm, axis=-1, keepdims=True)
            acc_n = alpha * acc + _dot(pm.astype(BF16), v_ref[0, g, pl.ds(start, tk), :])
            new.append((m_n, l_n, acc_n))
        return tuple(new)

    init = tuple((jnp.full((gq, 1), NEG_BIG, F32), jnp.zeros((gq, 1), F32), jnp.zeros((gq, HEAD_DIM), F32))
                 for _ in range(N_KV_HEADS))
    res = lax.fori_loop(0, n_kt, attn_tile, init)
    for g in range(N_KV_HEADS):
        _, l_i, acc = res[g]
        og = acc / l_i
        for j in range(Q_PER_KV):
            h = g * Q_PER_KV + j
            o_ref[0, :, h * HEAD_DIM:(h + 1) * HEAD_DIM] = og[j * tq:(j + 1) * tq].astype(o_ref.dtype)


def _dsa(iq_hm, iw, q_hm, ikt, kt, v_gm, *, tq, tk, q_offset, top):
    b, _, sq, _ = iq_hm.shape
    n_keys = ikt.shape[-1]
    assert sq % tq == 0 and n_keys % tk == 0 and tk % LANES == 0
    kern = functools.partial(_dsa_kernel, tq=tq, tk=tk, q_offset=q_offset, n_keys=n_keys, top=top)
    return pl.pallas_call(
        kern,
        grid=(b, sq // tq),
        in_specs=[
            pl.BlockSpec((1, N_IDX_HEADS, tq, D_IDX), lambda bi, qi: (bi, 0, qi, 0)),
            pl.BlockSpec((1, tq, N_IDX_HEADS), lambda bi, qi: (bi, qi, 0)),
            pl.BlockSpec((1, N_Q_HEADS, tq, HEAD_DIM), lambda bi, qi: (bi, 0, qi, 0)),
            pl.BlockSpec((1, D_IDX, n_keys), lambda bi, qi: (bi, 0, 0)),
            pl.BlockSpec((1, N_KV_HEADS, HEAD_DIM, n_keys), lambda bi, qi: (bi, 0, 0, 0)),
            pl.BlockSpec((1, N_KV_HEADS, n_keys, HEAD_DIM), lambda bi, qi: (bi, 0, 0, 0)),
        ],
        out_specs=pl.BlockSpec((1, tq, ATTN_W), lambda bi, qi: (bi, qi, 0)),
        out_shape=jax.ShapeDtypeStruct((b, sq, ATTN_W), BF16),
        scratch_shapes=[pltpu.VMEM((tq, n_keys), jnp.int32)],
        compiler_params=pltpu.CompilerParams(dimension_semantics=("parallel", "arbitrary"),
                                             vmem_limit_bytes=VMEM_LIMIT),
        name="dsa",
    )(iq_hm, iw, q_hm, ikt, kt, v_gm)


def _rwkv_kernel(pr_ref, sh0_ref, z0_ref, mu_ref, w0_ref, wup_ref, a0_ref, aup_ref, gup_ref, kk_ref, ka_ref,
                 rk_ref, lnw_ref, lnb_ref, g512_ref,
                 ob_ref, zout_ref,
                 z_scr, prev_scr, r_s, k_s, v_s, a_s, b_s, lw_s, y_s, *, tt, chunk, t_valid):
    ti = pl.program_id(1)
    n_t = pl.num_programs(1)
    nh, hd = N_RWKV_HEADS, RWKV_HEAD

    @pl.when(ti == 0)
    def _():
        z_scr[...] = z0_ref[0]
        prev_scr[...] = sh0_ref[0]

    g512 = g512_ref[...]
    gsum = lambda z: _dot(z, g512, HI)

    pr = pr_ref[0]
    row = lax.broadcasted_iota(jnp.int32, (tt, 1), 0)
    prev = jnp.where(row == 0, prev_scr[...], pltpu.roll(pr, 1, 0))
    prev_scr[...] = pr[tt - 1:tt]
    m = pr + mu_ref[...] * (prev - pr)
    r = m[:, 0:RWKV_W]
    k = m[:, RWKV_W:2 * RWKV_W]
    v = m[:, 2 * RWKV_W:3 * RWKV_W]
    o = 3 * RWKV_W
    wd = m[:, o:o + W_LORA]
    ad = m[:, o + W_LORA:o + W_LORA + A_LORA]
    gd = m[:, o + W_LORA + A_LORA:]

    u = -(w0_ref[...] + _dot(jnp.tanh(wd), wup_ref[...], HI))
    softplus = jnp.maximum(u, 0.0) + jnp.log(1.0 + jnp.exp(-jnp.abs(u)))
    lw = -jnp.exp(-softplus - 0.5)
    a = jax.nn.sigmoid(a0_ref[...] + _dot(ad, aup_ref[...], HI))
    gate = _dot(jax.nn.sigmoid(gd), gup_ref[...], HI)
    kk = k * kk_ref[...]
    kk = kk / jnp.maximum(jnp.sqrt(gsum(kk * kk)), 1e-12)
    k2 = k * (1.0 + (a - 1.0) * ka_ref[...])
    bonus = gsum(r * k2 * rk_ref[...])
    av = -kk
    bv = kk * a
    if t_valid < tt:
        ok = (row < t_valid).astype(F32)
        k2, v, av, bv, lw = k2 * ok, v * ok, av * ok, bv * ok, lw * ok
    r_s[...] = r
    k_s[...] = k2
    v_s[...] = v
    a_s[...] = av
    b_s[...] = bv
    lw_s[...] = lw

    c = chunk
    ri = lax.broadcasted_iota(jnp.int32, (c, c), 0)
    ci = lax.broadcasted_iota(jnp.int32, (c, c), 1)
    tri_incl = (ci <= ri)
    tri_strict = (ci < ri)
    ltri = tri_incl.astype(F32)
    eye_h = (lax.broadcasted_iota(jnp.int32, (hd, hd), 0) == lax.broadcasted_iota(jnp.int32, (hd, hd), 1))
    n_double = max(int(np.ceil(np.log2(c))), 1)

    def chunk_body(ci_, carry):
        s0 = pl.multiple_of(ci_ * c, c)
        sl = pl.ds(s0, c)
        lwc = lw_s[sl, :]
        cum = _dot(ltri, lwc, HI)
        cum_end = cum[c - 1:c, :]
        g_end = jnp.exp(cum_end)
        at = a_s[sl, :] * jnp.exp(cum - lwc)
        rt = r_s[sl, :] * jnp.exp(cum)
        g_inv = jnp.exp(-cum)
        g_tail = jnp.exp(cum_end - cum)
        bt = b_s[sl, :] * g_inv
        kt = k_s[sl, :] * g_inv
        bc = b_s[sl, :] * g_tail
        kc = k_s[sl, :] * g_tail
        vc = v_s[sl, :]
        for h in range(nh):
            hs = slice(h * hd, (h + 1) * hd)
            left = jnp.concatenate([at[:, hs], rt[:, hs]], axis=0)
            right = jnp.concatenate([bt[:, hs], kt[:, hs]], axis=0)
            amat = _dot_nt(left, right, HI)
            a_ab = jnp.where(tri_strict, amat[:c, :c], 0.0)
            a_ak = jnp.where(tri_strict, amat[:c, c:], 0.0)
            a_rb = jnp.where(tri_incl, amat[c:, :c], 0.0)
            a_rk = jnp.where(tri_incl, amat[c:, c:], 0.0)
            z0 = z_scr[h]
            lz = _dot(left, z0, HI)
            vh = vc[:, hs]
            uu = lz[:c] + _dot(a_ak, vh, HI)
            pw = a_ab
            for step in range(n_double):
                uu = uu + _dot(pw, uu, HI)
                if step + 1 < n_double:
                    pw = _dot(pw, pw, HI)
            y_s[sl, hs] = lz[c:] + _dot(a_rb, uu, HI) + _dot(a_rk, vh, HI)
            dmat = jnp.where(eye_h, g_end[:, hs], 0.0)
            z_scr[h] = (_dot(dmat, z0, HI) + _dot_tn(bc[:, hs], uu, HI) + _dot_tn(kc[:, hs], vh, HI))
        return carry

    lax.fori_loop(0, tt // c, chunk_body, 0)

    y = y_s[...]
    mean = gsum(y) * (1.0 / hd)
    dlt = y - mean
    var = gsum(dlt * dlt) * (1.0 / hd)
    yn = dlt * lax.rsqrt(var + LNX_EPS) * lnw_ref[...] + lnb_ref[...]
    ob_ref[0] = ((yn + bonus * v_s[...]) * gate).astype(ob_ref.dtype)

    @pl.when(ti == n_t - 1)
    def _():
        zout_ref[0] = z_scr[...]


def _rwkv(pr, shift0, z0, p, *, tt, chunk, t_valid):
    b, t, _ = pr.shape
    assert t % tt == 0 and tt % chunk == 0
    consts = (p["shift_mu"], p["w0"], p["w_lora_up"], p["a0"], p["a_lora_up"], p["g_lora_up"], p["k_k"], p["k_a"],
              p["r_k"], p["ln_x_w"], p["ln_x_b"], p["g512"])
    kern = functools.partial(_rwkv_kernel, tt=tt, chunk=chunk, t_valid=t_valid)
    wide = lambda: pltpu.VMEM((tt, RWKV_W), F32)
    return pl.pallas_call(
        kern,
        grid=(b, t // tt),
        in_specs=[
            pl.BlockSpec((1, tt, RWKV_PROJ_W), lambda bi, ti: (bi, ti, 0)),
            pl.BlockSpec((1, 1, RWKV_PROJ_W), lambda bi, ti: (bi, 0, 0)),
            pl.BlockSpec((1, N_RWKV_HEADS, RWKV_HEAD, RWKV_HEAD), lambda bi, ti: (bi, 0, 0, 0)),
        ] + [_const_spec(c.shape) for c in consts],
        out_specs=[
            pl.BlockSpec((1, tt, RWKV_W), lambda bi, ti: (bi, ti, 0)),
            pl.BlockSpec((1, N_RWKV_HEADS, RWKV_HEAD, RWKV_HEAD), lambda bi, ti: (bi, 0, 0, 0)),
        ],
        out_shape=[jax.ShapeDtypeStruct((b, t, RWKV_W), BF16),
                   jax.ShapeDtypeStruct((b, N_RWKV_HEADS, RWKV_HEAD, RWKV_HEAD), F32)],
        scratch_shapes=[pltpu.VMEM((N_RWKV_HEADS, RWKV_HEAD, RWKV_HEAD), F32),
                        pltpu.VMEM((1, RWKV_PROJ_W), F32),
                        wide(), wide(), wide(), wide(), wide(), wide(), wide()],
        compiler_params=pltpu.CompilerParams(dimension_semantics=("parallel", "arbitrary"),
                                             vmem_limit_bytes=VMEM_LIMIT),
        name="rwkv",
    )(pr, shift0, z0, *consts)


def _merge_kernel(x_ref, oa_ref, ob_ref, ga_ref, gb_ref, wa_ref, wb_ref, wo_ref, nf_ref, rw_ref, rb_ref,
                  h_out, hn_out, idx_out, gate_out):
    ma = _dot(oa_ref[...], wa_ref[...])
    mb = _dot(ob_ref[...], wb_ref[...])
    mm = ga_ref[...].astype(F32) * ma + gb_ref[...].astype(F32) * mb
    h = x_ref[...] + _dot(mm.astype(BF16), wo_ref[...])
    h_out[...] = h
    ms = jnp.mean(h * h, axis=-1, keepdims=True)
    hn = h * lax.rsqrt(ms + NORM_EPS) * nf_ref[...]
    hn_out[...] = hn.astype(hn_out.dtype)
    logits = _dot(hn, rw_ref[...], HI) + rb_ref[...]
    tm = logits.shape[0]
    lane = lax.broadcasted_iota(jnp.int32, logits.shape, 1)
    wide = lax.broadcasted_iota(jnp.int32, (tm, LANES), 1)
    idx_w = jnp.zeros((tm, LANES), jnp.int32)
    val_w = jnp.full((tm, LANES), -jnp.inf, F32)
    for kth in range(TOP_K):
        mx = jnp.max(logits, axis=-1, keepdims=True)
        ix = jnp.min(jnp.where(logits == mx, lane, N_EXPERTS), axis=-1, keepdims=True)
        idx_w = jnp.where(wide == kth, ix, idx_w)
        val_w = jnp.where(wide == kth, mx, val_w)
        logits = jnp.where(lane == ix, -jnp.inf, logits)
    e = jnp.exp(val_w - jnp.max(val_w, axis=-1, keepdims=True))
    idx_out[...] = idx_w
    gate_out[...] = e / jnp.sum(e, axis=-1, keepdims=True)


def _merge(x2d, oa, ob, ga, gb, p, tm):
    n = x2d.shape[0]
    row = lambda w: pl.BlockSpec((tm, w), lambda i: (i, 0))
    consts = (p["w_proj_a"], p["w_proj_b"], p["w_out"], p["norm_ffn"], p["router_w"], p["router_b"])
    return pl.pallas_call(
        _merge_kernel,
        grid=(n // tm,),
        in_specs=[row(D_MODEL), row(ATTN_W), row(RWKV_W), row(D_MODEL), row(D_MODEL)]
        + [_const_spec(c.shape) for c in consts],
        out_specs=[row(D_MODEL), row(D_MODEL), row(LANES), row(LANES)],
        out_shape=[jax.ShapeDtypeStruct((n, D_MODEL), F32), jax.ShapeDtypeStruct((n, D_MODEL), BF16),
                   jax.ShapeDtypeStruct((n, LANES), jnp.int32), jax.ShapeDtypeStruct((n, LANES), F32)],
        compiler_params=pltpu.CompilerParams(dimension_semantics=("parallel",), vmem_limit_bytes=VMEM_LIMIT),
        name="merge",
    )(x2d, oa, ob, ga, gb, *consts)


def _moe_kernel(be_ref, x_ref, gate_ref, wgu_ref, bgu_ref, wd_ref, bd_ref, o_ref):
    del be_ref
    hcat = _dot(x_ref[...], wgu_ref[0]) + bgu_ref[0]
    glu = jnp.minimum(hcat[:, :D_FF], SWIGLU_LIMIT)
    lin = jnp.clip(hcat[:, D_FF:], -SWIGLU_LIMIT, SWIGLU_LIMIT)
    act = glu * jax.nn.sigmoid(SWIGLU_ALPHA * glu) * (lin + 1.0)
    out = _dot(act.astype(BF16), wd_ref[0]) + bd_ref[0]
    o_ref[...] = out * gate_ref[...]


def _moe_rows(xg, row_gate, block_exp, p, bm):
    rows = xg.shape[0]
    grid_spec = pltpu.PrefetchScalarGridSpec(
        num_scalar_prefetch=1,
        grid=(rows // bm,),
        in_specs=[
            pl.BlockSpec((bm, D_MODEL), lambda i, be: (i, 0)),
            pl.BlockSpec((bm, 1), lambda i, be: (i, 0)),
            pl.BlockSpec((1, D_MODEL, 2 * D_FF), lambda i, be: (be[i], 0, 0)),
            pl.BlockSpec((1, 1, 2 * D_FF), lambda i, be: (be[i], 0, 0)),
            pl.BlockSpec((1, D_FF, D_MODEL), lambda i, be: (be[i], 0, 0)),
            pl.BlockSpec((1, 1, D_MODEL), lambda i, be: (be[i], 0, 0)),
        ],
        out_specs=pl.BlockSpec((bm, D_MODEL), lambda i, be: (i, 0)),
    )
    return pl.pallas_call(
        _moe_kernel,
        grid_spec=grid_spec,
        out_shape=jax.ShapeDtypeStruct((rows, D_MODEL), F32),
        compiler_params=pltpu.CompilerParams(dimension_semantics=("arbitrary",), vmem_limit_bytes=VMEM_LIMIT),
        name="moe",
    )(block_exp, xg, row_gate, p["w_gate_up"], p["b_gate_up"], p["w_down"], p["b_down"])


def _moe(hn, top_idx, gate, p, bm):
    n = hn.shape[0]
    nk = n * TOP_K
    e_flat = top_idx.reshape(-1)
    onehot = (e_flat[:, None] == jnp.arange(N_EXPERTS, dtype=jnp.int32)[None, :]).astype(jnp.int32)
    csum = jnp.cumsum(onehot, axis=0)
    counts = csum[-1]
    rank = jnp.take_along_axis(csum, e_flat[:, None], axis=1)[:, 0] - 1
    padded = (counts + bm - 1) // bm * bm
    pends = jnp.cumsum(padded)
    pstarts = pends - padded
    dest = pstarts[e_flat] + rank
    n_blocks = -(-nk // bm) + N_EXPERTS
    rows = n_blocks * bm
    tok_flat = jnp.arange(nk, dtype=jnp.int32) // TOP_K
    row_tok = jnp.full((rows,), n, jnp.int32).at[dest].set(tok_flat)
    row_gate = jnp.zeros((rows,), F32).at[dest].set(gate.reshape(-1))
    block_exp = jnp.clip(jnp.searchsorted(pends, jnp.arange(n_blocks, dtype=jnp.int32) * bm, side="right"),
                         0, N_EXPERTS - 1).astype(jnp.int32)
    x_pad = jnp.concatenate([hn, jnp.zeros((1, D_MODEL), hn.dtype)], axis=0)
    out = _moe_rows(x_pad[row_tok], row_gate[:, None], block_exp, p, bm)
    return out[dest.reshape(n, TOP_K)].sum(axis=1)


def _prep_params(norm_mix, w_in, q_norm, k_norm, idx_k_norm, shift_mu, w0, w_lora_up, a0, a_lora_up, g_lora_up, k_k,
                 k_a, r_k, ln_x_w, ln_x_b, w_proj_a, w_proj_b, w_out, norm_ffn, router_w, router_b, w_gate_up,
                 b_gate_up, w_down, b_down):
    splits = (ATTN_W, KV_W, KV_W, IDX_W, D_IDX, N_IDX_HEADS, RWKV_PROJ_W, D_MODEL, D_MODEL)
    cuts = np.cumsum(splits)[:-1].tolist()
    wq, wk, wv, wiq, wik, wiw, wpr, wga, wgb = jnp.split(w_in.astype(BF16), cuts, axis=-1)
    wikw = jnp.concatenate([wik, wiw, jnp.zeros((D_MODEL, LANES - D_IDX - N_IDX_HEADS), BF16)], axis=-1)
    row = lambda z: z.reshape(1, -1).astype(F32)
    return dict(
        norm_mix=row(norm_mix), wq=wq, wk=wk, wv=wv, wiq=wiq, wikw=wikw, wpr=wpr, wga=wga, wgb=wgb,
        q_norm_t=row(jnp.tile(q_norm, N_Q_HEADS)), k_norm_t=row(jnp.tile(k_norm, N_KV_HEADS)),
        ik_norm_t=row(jnp.concatenate([idx_k_norm, jnp.ones((LANES - D_IDX,), F32)])),
        g512=_group_ones(RWKV_W, RWKV_HEAD), g128=_group_ones(KV_W, HEAD_DIM),
        shift_mu=row(shift_mu), w0=row(w0), w_lora_up=w_lora_up, a0=row(a0), a_lora_up=a_lora_up,
        g_lora_up=g_lora_up, k_k=row(k_k), k_a=row(k_a), r_k=row(r_k), ln_x_w=row(ln_x_w), ln_x_b=row(ln_x_b),
        w_proj_a=w_proj_a.astype(BF16), w_proj_b=w_proj_b.astype(BF16), w_out=w_out.astype(BF16),
        norm_ffn=row(norm_ffn), router_w=router_w, router_b=row(router_b),
        w_gate_up=w_gate_up.astype(BF16), b_gate_up=b_gate_up[:, None, :], w_down=w_down.astype(BF16),
        b_down=b_down[:, None, :],
    )


def _heads_major(z, b, t, nh, hd):
    return z.reshape(b, t, nh, hd).transpose(0, 2, 1, 3)


def _pad_axis(z, axis, size):
    pad = [(0, 0)] * z.ndim
    pad[axis] = (0, size - z.shape[axis])
    return jnp.pad(z, pad)


def _group(x, p, *, tm):
    b, t, _ = x.shape
    q, k, v, iq, ikw, pr, ga, gb = _in_proj(x.reshape(b * t, D_MODEL), p, tm)
    return dict(b=b, t=t, q=q, k=k, v=v, iq=iq, ik=ikw[:, :D_IDX], iw=ikw[:, D_IDX:D_IDX + N_IDX_HEADS],
                pr=pr.reshape(b, t, RWKV_PROJ_W), ga=ga, gb=gb)


def _attend(g, ik_all, k_all, v_all, *, tq, tk, q_offset, top):
    b, t = g["b"], g["t"]
    tp = -(-t // tq) * tq
    n_keys = -(-ik_all.shape[1] // tk) * tk
    iq_hm = _pad_axis(_heads_major(g["iq"], b, t, N_IDX_HEADS, D_IDX), 2, tp)
    q_hm = _pad_axis(_heads_major(g["q"] * ATTN_SCALE, b, t, N_Q_HEADS, HEAD_DIM), 2, tp)
    iw = _pad_axis(g["iw"].reshape(b, t, N_IDX_HEADS), 1, tp)
    ikt = _pad_axis(ik_all.astype(BF16).transpose(0, 2, 1), 2, n_keys)
    kt = _pad_axis(k_all.astype(BF16).transpose(0, 2, 3, 1), 3, n_keys)
    v_gm = _pad_axis(v_all.astype(BF16).transpose(0, 2, 1, 3), 2, n_keys)
    oa = _dsa(iq_hm, iw, q_hm, ikt, kt, v_gm, tq=tq, tk=tk, q_offset=q_offset, top=top)
    return oa[:, :t].reshape(b * t, ATTN_W)


def _mix(g, shift0, wkv0, p, *, tt, chunk):
    b, t = g["b"], g["t"]
    tp = -(-t // tt) * tt
    ob, z = _rwkv(_pad_axis(g["pr"], 1, tp), shift0[:, None, :], jnp.swapaxes(wkv0, -1, -2), p,
                  tt=tt, chunk=chunk, t_valid=min(t, tt) if tp != t else tt)
    return ob[:, :t].reshape(b * t, RWKV_W), jnp.swapaxes(z, -1, -2)


def kernel(x_prompt, x_sample, cache_k, cache_v, cache_idx_k, page_table, state_wkv, state_shift, norm_mix, w_in, q_norm, k_norm, idx_k_norm, shift_mu, w0, w_lora_up, a0, a_lora_up, g_lora_up, k_k, k_a, r_k, ln_x_w, ln_x_b, w_proj_a, w_proj_b, w_out, norm_ffn, router_w, router_b, w_gate_up, b_gate_up, w_down, b_down):
    depth = norm_mix.shape[0]
    assert depth == 1
    params = (norm_mix, w_in, q_norm, k_norm, idx_k_norm, shift_mu, w0, w_lora_up, a0, a_lora_up, g_lora_up, k_k, k_a,
              r_k, ln_x_w, ln_x_b, w_proj_a, w_proj_b, w_out, norm_ffn, router_w, router_b, w_gate_up, b_gate_up,
              w_down, b_down)
    p = _prep_params(*[z[0] for z in params])
    bp, sp, _ = x_prompt.shape
    bs, ts, _ = x_sample.shape
    n_p, n_s = bp * sp, bs * ts
    past = page_table.shape[1] * PAGE_SIZE

    gp = _group(x_prompt, p, tm=min(256, n_p))
    k_p = gp["k"].reshape(bp, sp, N_KV_HEADS, HEAD_DIM)
    v_p = gp["v"].reshape(bp, sp, N_KV_HEADS, HEAD_DIM)
    ik_p = gp["ik"].reshape(bp, sp, D_IDX)
    oa_p = _attend(gp, ik_p, k_p, v_p, tq=min(128, sp), tk=min(512, sp), q_offset=0, top=min(TOPK_MAX, sp // 4))
    ob_p, wkv_p = _mix(gp, jnp.zeros((bp, RWKV_PROJ_W), F32),
                       jnp.zeros((bp, N_RWKV_HEADS, RWKV_HEAD, RWKV_HEAD), F32), p,
                       tt=min(256, sp), chunk=min(64, sp))

    gs = _group(x_sample, p, tm=min(256, n_s))
    k_s = gs["k"].reshape(bs, ts, N_KV_HEADS, HEAD_DIM)
    v_s = gs["v"].reshape(bs, ts, N_KV_HEADS, HEAD_DIM)
    ik_s = gs["ik"].reshape(bs, ts, D_IDX)
    ik_all = jnp.concatenate([cache_idx_k[0][page_table].reshape(bs, past, D_IDX), ik_s], axis=1)
    k_all = jnp.concatenate([cache_k[0][page_table].reshape(bs, past, N_KV_HEADS, HEAD_DIM), k_s], axis=1)
    v_all = jnp.concatenate([cache_v[0][page_table].reshape(bs, past, N_KV_HEADS, HEAD_DIM), v_s], axis=1)
    oa_s = _attend(gs, ik_all, k_all, v_all, tq=16, tk=5 * LANES, q_offset=past, top=min(TOPK_MAX, (past + ts) // 4))
    ob_s, wkv_s = _mix(gs, state_shift[0], state_wkv[0], p, tt=8, chunk=8)

    h_p, hn_p, idx_p, gate_p = _merge(x_prompt.reshape(n_p, D_MODEL), oa_p, ob_p, gp["ga"], gp["gb"], p,
                                      tm=min(256, n_p))
    h_s, hn_s, idx_s, gate_s = _merge(x_sample.reshape(n_s, D_MODEL), oa_s, ob_s, gs["ga"], gs["gb"], p,
                                      tm=min(256, n_s))
    hn = jnp.concatenate([hn_p, hn_s], axis=0)
    top_idx = jnp.concatenate([idx_p[:, :TOP_K], idx_s[:, :TOP_K]], axis=0)
    gate = jnp.concatenate([gate_p[:, :TOP_K], gate_s[:, :TOP_K]], axis=0)
    f = _moe(hn, top_idx, gate, p, bm=512)
    y_p = (h_p + f[:n_p]).reshape(bp, sp, D_MODEL)
    y_s = (h_s + f[n_p:]).reshape(bs, ts, D_MODEL)

    st = lambda z: z[None]
    return (y_p, y_s, st(k_p), st(v_p), st(ik_p), st(wkv_p), st(gp["pr"][:, -1]),
            st(k_s), st(v_s), st(ik_s), st(wkv_s), st(gs["pr"][:, -1]))
```

```python
import functools

import jax
import jax.numpy as jnp
import numpy as np
from jax import lax
from jax.experimental import pallas as pl
from jax.experimental.pallas import tpu as pltpu

D_MODEL = 1024
PAGE_SIZE = 128
HEAD_DIM = 64
N_Q_HEADS = 8
N_KV_HEADS = 2
Q_PER_KV = N_Q_HEADS // N_KV_HEADS
ATTN_W = N_Q_HEADS * HEAD_DIM
KV_W = N_KV_HEADS * HEAD_DIM
ATTN_SCALE = HEAD_DIM ** -0.5
N_IDX_HEADS = 8
D_IDX = 64
IDX_W = N_IDX_HEADS * D_IDX
IDX_SCALE = (N_IDX_HEADS * D_IDX) ** -0.5
TOPK_MAX = 256
RWKV_HEAD = 64
N_RWKV_HEADS = 8
RWKV_W = N_RWKV_HEADS * RWKV_HEAD
W_LORA = 64
A_LORA = 64
G_LORA = 128
RWKV_PROJ_W = 3 * RWKV_W + W_LORA + A_LORA + G_LORA
LNX_EPS = 64e-5
N_EXPERTS = 32
TOP_K = 4
D_FF = 1024
SWIGLU_LIMIT = 7.0
SWIGLU_ALPHA = 1.702
NORM_EPS = 1e-6

LANES = 128
VMEM_LIMIT = 56 * 1024 * 1024
INT_MIN = -(2 ** 31)
NEG_BIG = -1e30
F32 = jnp.float32
BF16 = jnp.bfloat16
HI = lax.Precision.HIGHEST


def _dot(a, b, precision=None):
    return jnp.dot(a, b, preferred_element_type=F32, precision=precision)


def _dot_nt(a, b, precision=None):
    return lax.dot_general(a, b, (((1,), (1,)), ((), ())), preferred_element_type=F32, precision=precision)


def _dot_tn(a, b, precision=None):
    return lax.dot_general(a, b, (((0,), (0,)), ((), ())), preferred_element_type=F32, precision=precision)


def _group_indicator(width, group):
    r = np.arange(width) // group
    return jnp.asarray((r[:, None] == np.arange(LANES)[None, :]).astype(np.float32))


def _split(x, terms=2):
    parts = []
    for _ in range(terms - 1):
        hi = x.astype(BF16)
        parts.append(hi)
        x = x - hi.astype(F32)
    parts.append(x.astype(BF16))
    return parts


def _mm(a, b, dot=None):
    dot = dot or _dot
    return dot(a[0], b[0]) + (dot(a[0], b[1]) + dot(a[1], b[0]))


def _group_sum(x, ge, get):
    s = sum(_dot(part, ge) for part in _split(x))
    return sum(_dot(part, get) for part in _split(s))


def _const_spec(shape):
    nd = len(shape)
    return pl.BlockSpec(shape, lambda *_: (0,) * nd)


def _inproj_kernel(x_ref, g_ref, wq_ref, wk_ref, wv_ref, wiq_ref, wikw_ref, wpr_ref, wga_ref, wgb_ref,
                   qn_ref, kn_ref, ikn_ref, ge_ref, get_ref, ge2_ref, ge2t_ref,
                   q_out, k_out, kb_out, v_out, vb_out, iq_out, ik_out, ikb_out, iw_out, pr_out, ga_out, gb_out):
    x = x_ref[...]
    ms = jnp.mean(x * x, axis=-1, keepdims=True)
    xn = (x * lax.rsqrt(ms + NORM_EPS) * g_ref[...]).astype(BF16)

    q = _dot(xn, wq_ref[...])
    qs = _group_sum(q * q, ge_ref[...], get_ref[...]) * (1.0 / HEAD_DIM)
    q_out[...] = (q * lax.rsqrt(qs + NORM_EPS) * qn_ref[...]).astype(q_out.dtype)

    k = _dot(xn, wk_ref[...])
    ks = _group_sum(k * k, ge2_ref[...], ge2t_ref[...]) * (1.0 / HEAD_DIM)
    kn = k * lax.rsqrt(ks + NORM_EPS) * kn_ref[...]
    k_out[...] = kn
    kb_out[...] = kn.astype(kb_out.dtype)

    v = _dot(xn, wv_ref[...])
    v_out[...] = v
    vb_out[...] = v.astype(vb_out.dtype)
    iq_out[...] = _dot(xn, wiq_ref[...]).astype(iq_out.dtype)

    ikw = _dot(xn, wikw_ref[...])
    lane = lax.broadcasted_iota(jnp.int32, ikw.shape, 1)
    iks = jnp.sum(jnp.where(lane < D_IDX, ikw * ikw, 0.0), axis=-1, keepdims=True) * (1.0 / D_IDX)
    ikn = ikw[:, :D_IDX] * lax.rsqrt(iks + NORM_EPS) * ikn_ref[...]
    ik_out[...] = ikn
    ikb_out[...] = ikn.astype(ikb_out.dtype)
    iw_out[...] = ikw[:, D_IDX:D_IDX + N_IDX_HEADS]

    pr_out[...] = _dot(xn, wpr_ref[...])
    ga_out[...] = jax.nn.sigmoid(_dot(xn, wga_ref[...])).astype(ga_out.dtype)
    gb_out[...] = jax.nn.sigmoid(_dot(xn, wgb_ref[...])).astype(gb_out.dtype)


def _in_proj(x2d, p, tm):
    n = x2d.shape[0]
    widths = (ATTN_W, KV_W, KV_W, KV_W, KV_W, IDX_W, D_IDX, D_IDX, N_IDX_HEADS, RWKV_PROJ_W, D_MODEL, D_MODEL)
    dtypes = (BF16, F32, BF16, F32, BF16, BF16, F32, BF16, F32, F32, BF16, BF16)
    row = lambda w: pl.BlockSpec((tm, w), lambda i: (i, 0))
    consts = (p["norm_mix"], p["wq"], p["wk"], p["wv"], p["wiq"], p["wikw"], p["wpr"], p["wga"], p["wgb"],
              p["q_norm_t"], p["k_norm_t"], p["ik_norm_t"], p["ge"], p["get"], p["ge2"], p["ge2t"])
    return pl.pallas_call(
        _inproj_kernel,
        grid=(n // tm,),
        in_specs=[row(D_MODEL)] + [_const_spec(c.shape) for c in consts],
        out_specs=[row(w) for w in widths],
        out_shape=[jax.ShapeDtypeStruct((n, w), d) for w, d in zip(widths, dtypes)],
        compiler_params=pltpu.CompilerParams(dimension_semantics=("parallel",), vmem_limit_bytes=VMEM_LIMIT),
        name="in_proj",
    )(x2d, *consts)


def _stack_heads(x, heads, width):
    return jnp.concatenate([x[:, h * width:(h + 1) * width] for h in heads], axis=0)


def _dsa_kernel(iq_ref, iw_ref, q_ref, ik_ref, k_ref, v_ref, o_ref, keys_ref, *, tq, tk, q_offset, n_keys, top):
    qi = pl.program_id(1)
    q_base = q_offset + qi * tq
    n_kt = jnp.minimum((q_base + tq + tk - 1) // tk, n_keys // tk)
    q_pos = q_base + lax.broadcasted_iota(jnp.int32, (tq, 1), 0)
    lane_pos = lax.broadcasted_iota(jnp.int32, (tq, tk), 1)

    iq = _stack_heads(iq_ref[0], range(N_IDX_HEADS), D_IDX)
    iw = iw_ref[0]

    def score_tile(kt, carry):
        start = pl.multiple_of(kt * tk, tk)
        d = _dot_nt(iq, ik_ref[0, pl.ds(start, tk), :])
        acc = jnp.zeros((tq, tk), F32)
        for h in range(N_IDX_HEADS):
            acc = acc + jnp.maximum(d[h * tq:(h + 1) * tq], 0.0) * iw[:, h:h + 1]
        sc = acc * IDX_SCALE
        sc = jnp.where(sc == 0.0, 0.0, sc)
        bits = pltpu.bitcast(sc, jnp.int32)
        key = bits ^ ((bits >> 31) & 0x7FFFFFFF)
        key = jnp.where(start + lane_pos <= q_pos, key, INT_MIN)
        keys_ref[:, pl.ds(start, tk)] = key
        return carry

    lax.fori_loop(0, n_kt, score_tile, 0)

    def count(pred):
        def body(kt, c):
            start = pl.multiple_of(kt * tk, tk)
            m = pred(keys_ref[:, pl.ds(start, tk)], start).astype(jnp.int32)
            part = m[:, 0:LANES]
            for j in range(1, tk // LANES):
                part = part + m[:, j * LANES:(j + 1) * LANES]
            return c + part
        c = lax.fori_loop(0, n_kt, body, jnp.zeros((tq, LANES), jnp.int32))
        return jnp.sum(c, axis=-1, keepdims=True)

    c0 = count(lambda k, s: k >= 0)
    t = jnp.where(c0 >= top, 0, INT_MIN).astype(jnp.int32)

    def bit_step(i, t):
        cand = t | jnp.left_shift(jnp.int32(1), 30 - i)
        c = count(lambda k, s: k >= cand)
        return jnp.where(c >= top, cand, t)

    t = lax.fori_loop(0, 31, bit_step, t)
    t = jnp.maximum(t, INT_MIN + 1)

    c_ge = count(lambda k, s: k >= t)
    excess = c_ge > top

    @pl.when(jnp.max(excess.astype(jnp.int32)) > 0)
    def _():
        keep = top - count(lambda k, s: k > t)

        def idx_step(i, lim):
            cand = lim | jnp.left_shift(jnp.int32(1), 14 - i)
            c = count(lambda k, s: ((k == t) & (s + lane_pos < cand)))
            return jnp.where(c <= keep, cand, lim)

        lim = lax.fori_loop(0, 15, idx_step, jnp.zeros((tq, 1), jnp.int32))

        def demote(kt, carry):
            start = pl.multiple_of(kt * tk, tk)
            k = keys_ref[:, pl.ds(start, tk)]
            drop = (k == t) & (start + lane_pos >= lim) & excess
            keys_ref[:, pl.ds(start, tk)] = jnp.where(drop, t - 1, k)
            return carry

        lax.fori_loop(0, n_kt, demote, 0)

    gq = Q_PER_KV * tq
    q_all = q_ref[0] * ATTN_SCALE
    qs = [_stack_heads(q_all, range(g * Q_PER_KV, (g + 1) * Q_PER_KV), HEAD_DIM) for g in range(N_KV_HEADS)]

    def attn_tile(kt, carry):
        start = pl.multiple_of(kt * tk, tk)
        bias = jnp.where(keys_ref[:, pl.ds(start, tk)] >= t, 0.0, NEG_BIG)
        k_t = k_ref[0, pl.ds(start, tk), :]
        v_t = v_ref[0, pl.ds(start, tk), :]
        new = []
        for g in range(N_KV_HEADS):
            m_i, l_i, acc = carry[g]
            s = _dot_nt(qs[g], k_t[:, g * HEAD_DIM:(g + 1) * HEAD_DIM])
            s = (s.reshape(Q_PER_KV, tq, tk) + bias[None]).reshape(gq, tk)
            m_n = jnp.maximum(m_i, jnp.max(s, axis=-1, keepdims=True))
            alpha = jnp.exp(m_i - m_n)
            pm = jnp.exp(s - m_n)
            l_n = alpha * l_i + jnp.sum(pm, axis=-1, keepdims=True)
            acc_n = alpha * acc + _dot(pm.astype(BF16), v_t[:, g * HEAD_DIM:(g + 1) * HEAD_DIM])
            new.append((m_n, l_n, acc_n))
        return tuple(new)

    init = tuple((jnp.full((gq, 1), NEG_BIG, F32), jnp.zeros((gq, 1), F32), jnp.zeros((gq, HEAD_DIM), F32))
                 for _ in range(N_KV_HEADS))
    res = lax.fori_loop(0, n_kt, attn_tile, init)
    for g in range(N_KV_HEADS):
        _, l_i, acc = res[g]
        og = acc / l_i
        for j in range(Q_PER_KV):
            h = g * Q_PER_KV + j
            o_ref[0, :, h * HEAD_DIM:(h + 1) * HEAD_DIM] = og[j * tq:(j + 1) * tq].astype(o_ref.dtype)


def _dsa(iq, iw, q, ik, k, v, *, tq, tk, q_offset, top):
    b, sq, _ = iq.shape
    n_keys = ik.shape[1]
    assert sq % tq == 0 and n_keys % tk == 0 and tk % LANES == 0
    kern = functools.partial(_dsa_kernel, tq=tq, tk=tk, q_offset=q_offset, n_keys=n_keys, top=top)
    return pl.pallas_call(
        kern,
        grid=(b, sq // tq),
        in_specs=[
            pl.BlockSpec((1, tq, IDX_W), lambda bi, qi: (bi, qi, 0)),
            pl.BlockSpec((1, tq, N_IDX_HEADS), lambda bi, qi: (bi, qi, 0)),
            pl.BlockSpec((1, tq, ATTN_W), lambda bi, qi: (bi, qi, 0)),
            pl.BlockSpec((1, n_keys, D_IDX), lambda bi, qi: (bi, 0, 0)),
            pl.BlockSpec((1, n_keys, KV_W), lambda bi, qi: (bi, 0, 0)),
            pl.BlockSpec((1, n_keys, KV_W), lambda bi, qi: (bi, 0, 0)),
        ],
        out_specs=pl.BlockSpec((1, tq, ATTN_W), lambda bi, qi: (bi, qi, 0)),
        out_shape=jax.ShapeDtypeStruct((b, sq, ATTN_W), BF16),
        scratch_shapes=[pltpu.VMEM((tq, n_keys), jnp.int32)],
        compiler_params=pltpu.CompilerParams(dimension_semantics=("parallel", "arbitrary"),
                                             vmem_limit_bytes=VMEM_LIMIT),
        name="dsa",
    )(iq, iw, q, ik, k, v)


def _rwkv_kernel(pr_ref, sh0_ref, z0_ref, mu_ref, w0_ref, wup_ref, a0_ref, aup_ref, gup_ref, kk_ref, ka_ref,
                 rk_ref, lnw_ref, lnb_ref, ge_ref, get_ref,
                 ob_ref, zout_ref,
                 z_scr, prev_scr, r_s, k_s, v_s, a_s, b_s, lw_s, y_s, *, tt, chunk, t_valid):
    ti = pl.program_id(1)
    n_t = pl.num_programs(1)
    nh, hd = N_RWKV_HEADS, RWKV_HEAD

    @pl.when(ti == 0)
    def _():
        z_scr[...] = z0_ref[0]
        prev_scr[...] = sh0_ref[0]

    gsum = lambda z: _group_sum(z, ge_ref[...], get_ref[...])

    pr = pr_ref[0]
    row = lax.broadcasted_iota(jnp.int32, (tt, 1), 0)
    prev = jnp.where(row == 0, prev_scr[...], pltpu.roll(pr, 1, 0))
    prev_scr[...] = pr[tt - 1:tt]
    m = pr + mu_ref[...] * (prev - pr)
    r = m[:, 0:RWKV_W]
    k = m[:, RWKV_W:2 * RWKV_W]
    v = m[:, 2 * RWKV_W:3 * RWKV_W]
    o = 3 * RWKV_W
    wd = m[:, o:o + W_LORA]
    ad = m[:, o + W_LORA:o + W_LORA + A_LORA]
    gd = m[:, o + W_LORA + A_LORA:]

    u = -(w0_ref[...] + _dot(jnp.tanh(wd), wup_ref[...], HI))
    softplus = jnp.maximum(u, 0.0) + jnp.log(1.0 + jnp.exp(-jnp.abs(u)))
    lw = -jnp.exp(-softplus - 0.5)
    a = jax.nn.sigmoid(a0_ref[...] + _dot(ad, aup_ref[...], HI))
    gate = _dot(jax.nn.sigmoid(gd), gup_ref[...], HI)
    kk = k * kk_ref[...]
    kk = kk / jnp.maximum(jnp.sqrt(gsum(kk * kk)), 1e-12)
    k2 = k * (1.0 + (a - 1.0) * ka_ref[...])
    bonus = gsum(r * k2 * rk_ref[...])
    av = -kk
    bv = kk * a
    if t_valid < tt:
        ok = (row < t_valid).astype(F32)
        k2, v, av, bv, lw = k2 * ok, v * ok, av * ok, bv * ok, lw * ok
    r_s[...] = r
    k_s[...] = k2
    v_s[...] = v
    a_s[...] = av
    b_s[...] = bv
    lw_s[...] = lw

    c = chunk
    ri = lax.broadcasted_iota(jnp.int32, (c, c), 0)
    ci = lax.broadcasted_iota(jnp.int32, (c, c), 1)
    tri_incl = (ci <= ri)
    tri_strict = (ci < ri)
    ltri = tri_incl.astype(BF16)
    eye_h = (lax.broadcasted_iota(jnp.int32, (hd, hd), 0) == lax.broadcasted_iota(jnp.int32, (hd, hd), 1))
    n_double = max(int(np.ceil(np.log2(c))), 1)
    heads = range(nh)
    hsl = [slice(h * hd, (h + 1) * hd) for h in heads]

    def chunk_body(ci_, carry):
        s0 = pl.multiple_of(ci_ * c, c)
        sl = pl.ds(s0, c)
        lwc = lw_s[sl, :]
        cum = sum(_dot(ltri, part) for part in _split(lwc, 3))
        cum_end = cum[c - 1:c, :]
        g_end = jnp.exp(cum_end)
        at = a_s[sl, :] * jnp.exp(cum - lwc)
        rt = r_s[sl, :] * jnp.exp(cum)
        g_inv = jnp.exp(-cum)
        g_tail = jnp.exp(cum_end - cum)
        bt = b_s[sl, :] * g_inv
        kt = k_s[sl, :] * g_inv
        bc = b_s[sl, :] * g_tail
        kc = k_s[sl, :] * g_tail
        vc = v_s[sl, :]

        left = [_split(jnp.concatenate([at[:, s], rt[:, s]], axis=0)) for s in hsl]
        right = [_split(jnp.concatenate([bt[:, s], kt[:, s]], axis=0)) for s in hsl]
        amat = [_mm(left[h], right[h], _dot_nt) for h in heads]
        a_ab = [jnp.where(tri_strict, amat[h][:c, :c], 0.0) for h in heads]
        a_ak = [_split(jnp.where(tri_strict, amat[h][:c, c:], 0.0)) for h in heads]
        a_rb = [_split(jnp.where(tri_incl, amat[h][c:, :c], 0.0)) for h in heads]
        a_rk = [_split(jnp.where(tri_incl, amat[h][c:, c:], 0.0)) for h in heads]
        vh = [_split(vc[:, s]) for s in hsl]
        akv = [_mm(a_ak[h], vh[h]) for h in heads]
        uu = [jnp.concatenate([at[:, hsl[h]], akv[h]], axis=1) for h in heads]
        pw = a_ab
        for step in range(n_double):
            pws = [_split(z) for z in pw]
            uus = [_split(z) for z in uu]
            uu = [uu[h] + _mm(pws[h], uus[h]) for h in heads]
            if step + 1 < n_double:
                pw = [_mm(pws[h], pws[h]) for h in heads]
        uus = [_split(z) for z in uu]
        x1 = [_mm(a_rb[h], uus[h]) for h in heads]
        x2 = [_mm(a_rk[h], vh[h]) for h in heads]
        mn = [_mm(_split(bc[:, hsl[h]]), uus[h], _dot_tn) for h in heads]
        nk = [_mm(_split(kc[:, hsl[h]]), vh[h], _dot_tn) for h in heads]
        pm = []
        for h in heads:
            p2 = rt[:, hsl[h]] + x1[h][:, :hd]
            mh = jnp.where(eye_h, g_end[:, hsl[h]], 0.0) + mn[h][:, :hd]
            pm.append(_split(jnp.concatenate([p2, mh], axis=0)))
        res = [_mm(pm[h], _split(z_scr[h])) for h in heads]
        for h in heads:
            y_s[sl, hsl[h]] = res[h][:c] + x1[h][:, hd:] + x2[h]
            z_scr[h] = res[h][c:] + mn[h][:, hd:] + nk[h]
        return carry

    lax.fori_loop(0, tt // c, chunk_body, 0)

    y = y_s[...]
    mean = gsum(y) * (1.0 / hd)
    dlt = y - mean
    var = gsum(dlt * dlt) * (1.0 / hd)
    yn = dlt * lax.rsqrt(var + LNX_EPS) * lnw_ref[...] + lnb_ref[...]
    ob_ref[0] = ((yn + bonus * v_s[...]) * gate).astype(ob_ref.dtype)

    @pl.when(ti == n_t - 1)
    def _():
        zout_ref[0] = z_scr[...]


def _rwkv(pr, shift0, z0, p, *, tt, chunk, t_valid):
    b, t, _ = pr.shape
    assert t % tt == 0 and tt % chunk == 0
    consts = (p["shift_mu"], p["w0"], p["w_lora_up"], p["a0"], p["a_lora_up"], p["g_lora_up"], p["k_k"], p["k_a"],
              p["r_k"], p["ln_x_w"], p["ln_x_b"], p["ge"], p["get"])
    kern = functools.partial(_rwkv_kernel, tt=tt, chunk=chunk, t_valid=t_valid)
    wide = lambda: pltpu.VMEM((tt, RWKV_W), F32)
    return pl.pallas_call(
        kern,
        grid=(b, t // tt),
        in_specs=[
            pl.BlockSpec((1, tt, RWKV_PROJ_W), lambda bi, ti: (bi, ti, 0)),
            pl.BlockSpec((1, 1, RWKV_PROJ_W), lambda bi, ti: (bi, 0, 0)),
            pl.BlockSpec((1, N_RWKV_HEADS, RWKV_HEAD, RWKV_HEAD), lambda bi, ti: (bi, 0, 0, 0)),
        ] + [_const_spec(c.shape) for c in consts],
        out_specs=[
            pl.BlockSpec((1, tt, RWKV_W), lambda bi, ti: (bi, ti, 0)),
            pl.BlockSpec((1, N_RWKV_HEADS, RWKV_HEAD, RWKV_HEAD), lambda bi, ti: (bi, 0, 0, 0)),
        ],
        out_shape=[jax.ShapeDtypeStruct((b, t, RWKV_W), BF16),
                   jax.ShapeDtypeStruct((b, N_RWKV_HEADS, RWKV_HEAD, RWKV_HEAD), F32)],
        scratch_shapes=[pltpu.VMEM((N_RWKV_HEADS, RWKV_HEAD, RWKV_HEAD), F32),
                        pltpu.VMEM((1, RWKV_PROJ_W), F32),
                        wide(), wide(), wide(), wide(), wide(), wide(), wide()],
        compiler_params=pltpu.CompilerParams(dimension_semantics=("parallel", "arbitrary"),
                                             vmem_limit_bytes=VMEM_LIMIT),
        name="rwkv",
    )(pr, shift0, z0, *consts)


def _merge_kernel(x_ref, oa_ref, ob_ref, ga_ref, gb_ref, wa_ref, wb_ref, wo_ref, nf_ref, rw_ref, rb_ref,
                  h_out, hn_out, idx_out, gate_out):
    ma = _dot(oa_ref[...], wa_ref[...])
    mb = _dot(ob_ref[...], wb_ref[...])
    mm = ga_ref[...].astype(F32) * ma + gb_ref[...].astype(F32) * mb
    h = x_ref[...] + _dot(mm.astype(BF16), wo_ref[...])
    h_out[...] = h
    ms = jnp.mean(h * h, axis=-1, keepdims=True)
    hn = h * lax.rsqrt(ms + NORM_EPS) * nf_ref[...]
    hn_out[...] = hn.astype(hn_out.dtype)
    logits = _dot(hn, rw_ref[...], HI) + rb_ref[...]
    tm = logits.shape[0]
    lane = lax.broadcasted_iota(jnp.int32, logits.shape, 1)
    wide = lax.broadcasted_iota(jnp.int32, (tm, LANES), 1)
    idx_w = jnp.zeros((tm, LANES), jnp.int32)
    val_w = jnp.full((tm, LANES), -jnp.inf, F32)
    for kth in range(TOP_K):
        mx = jnp.max(logits, axis=-1, keepdims=True)
        ix = jnp.min(jnp.where(logits == mx, lane, N_EXPERTS), axis=-1, keepdims=True)
        idx_w = jnp.where(wide == kth, ix, idx_w)
        val_w = jnp.where(wide == kth, mx, val_w)
        logits = jnp.where(lane == ix, -jnp.inf, logits)
    e = jnp.exp(val_w - jnp.max(val_w, axis=-1, keepdims=True))
    idx_out[...] = idx_w
    gate_out[...] = e / jnp.sum(e, axis=-1, keepdims=True)


def _merge(x2d, oa, ob, ga, gb, p, tm):
    n = x2d.shape[0]
    row = lambda w: pl.BlockSpec((tm, w), lambda i: (i, 0))
    consts = (p["w_proj_a"], p["w_proj_b"], p["w_out"], p["norm_ffn"], p["router_w"], p["router_b"])
    return pl.pallas_call(
        _merge_kernel,
        grid=(n // tm,),
        in_specs=[row(D_MODEL), row(ATTN_W), row(RWKV_W), row(D_MODEL), row(D_MODEL)]
        + [_const_spec(c.shape) for c in consts],
        out_specs=[row(D_MODEL), row(D_MODEL), row(LANES), row(LANES)],
        out_shape=[jax.ShapeDtypeStruct((n, D_MODEL), F32), jax.ShapeDtypeStruct((n, D_MODEL), BF16),
                   jax.ShapeDtypeStruct((n, LANES), jnp.int32), jax.ShapeDtypeStruct((n, LANES), F32)],
        compiler_params=pltpu.CompilerParams(dimension_semantics=("parallel",), vmem_limit_bytes=VMEM_LIMIT),
        name="merge",
    )(x2d, oa, ob, ga, gb, *consts)


def _moe_kernel(be_ref, nb_ref, x_ref, wgu_ref, bgu_ref, wd_ref, bd_ref, o_ref, wgu_s, wd_s):
    i = pl.program_id(0)
    used = i < nb_ref[0]

    @pl.when(used & ((i == 0) | (be_ref[i] != be_ref[jnp.maximum(i - 1, 0)])))
    def _():
        wgu_s[...] = wgu_ref[0].astype(wgu_s.dtype)
        wd_s[...] = wd_ref[0].astype(wd_s.dtype)

    @pl.when(used)
    def _():
        hcat = _dot(x_ref[...], wgu_s[...]) + bgu_ref[0]
        glu = jnp.minimum(hcat[:, :D_FF], SWIGLU_LIMIT)
        lin = jnp.clip(hcat[:, D_FF:], -SWIGLU_LIMIT, SWIGLU_LIMIT)
        act = glu * jax.nn.sigmoid(SWIGLU_ALPHA * glu) * (lin + 1.0)
        o_ref[...] = _dot(act.astype(BF16), wd_s[...]) + bd_ref[0]

    @pl.when(jnp.logical_not(used))
    def _():
        o_ref[...] = jnp.zeros(o_ref.shape, o_ref.dtype)


def _moe_rows(xg, block_exp, n_used, p, bm):
    rows = xg.shape[0]
    grid_spec = pltpu.PrefetchScalarGridSpec(
        num_scalar_prefetch=2,
        grid=(rows // bm,),
        in_specs=[
            pl.BlockSpec((bm, D_MODEL), lambda i, be, nb: (i, 0)),
            pl.BlockSpec((1, D_MODEL, 2 * D_FF), lambda i, be, nb: (be[i], 0, 0)),
            pl.BlockSpec((1, 1, 2 * D_FF), lambda i, be, nb: (be[i], 0, 0)),
            pl.BlockSpec((1, D_FF, D_MODEL), lambda i, be, nb: (be[i], 0, 0)),
            pl.BlockSpec((1, 1, D_MODEL), lambda i, be, nb: (be[i], 0, 0)),
        ],
        out_specs=pl.BlockSpec((bm, D_MODEL), lambda i, be, nb: (i, 0)),
        scratch_shapes=[pltpu.VMEM((D_MODEL, 2 * D_FF), BF16), pltpu.VMEM((D_FF, D_MODEL), BF16)],
    )
    return pl.pallas_call(
        _moe_kernel,
        grid_spec=grid_spec,
        out_shape=jax.ShapeDtypeStruct((rows, D_MODEL), F32),
        compiler_params=pltpu.CompilerParams(dimension_semantics=("arbitrary",), vmem_limit_bytes=VMEM_LIMIT),
        name="moe",
    )(block_exp, n_used, xg, p["w_gate_up"], p["b_gate_up"], p["w_down"], p["b_down"])


def _moe(hn, top_idx, gate, p, bm):
    n = hn.shape[0]
    nk = n * TOP_K
    e_flat = top_idx.reshape(-1)
    onehot = (e_flat[:, None] == jnp.arange(N_EXPERTS, dtype=jnp.int32)[None, :]).astype(jnp.int32)
    csum = jnp.cumsum(onehot, axis=0)
    counts = csum[-1]
    rank = jnp.take_along_axis(csum, e_flat[:, None], axis=1)[:, 0] - 1
    padded = (counts + bm - 1) // bm * bm
    pends = jnp.cumsum(padded)
    pstarts = pends - padded
    dest = pstarts[e_flat] + rank
    n_blocks = -(-nk // bm) + N_EXPERTS
    rows = n_blocks * bm
    tok_flat = jnp.arange(nk, dtype=jnp.int32) // TOP_K
    row_tok = jnp.zeros((rows,), jnp.int32).at[dest].set(tok_flat, unique_indices=True)
    block_start = jnp.arange(n_blocks, dtype=jnp.int32) * bm
    block_exp = jnp.minimum(jnp.sum((pends[None, :] <= block_start[:, None]).astype(jnp.int32), axis=1),
                            N_EXPERTS - 1)
    n_used = (pends[-1:] // bm).astype(jnp.int32)
    out = _moe_rows(hn[row_tok], block_exp, n_used, p, bm)
    return (out[dest.reshape(n, TOP_K)] * gate[:, :, None]).sum(axis=1)


def _prep_params(norm_mix, w_in, q_norm, k_norm, idx_k_norm, shift_mu, w0, w_lora_up, a0, a_lora_up, g_lora_up, k_k,
                 k_a, r_k, ln_x_w, ln_x_b, w_proj_a, w_proj_b, w_out, norm_ffn, router_w, router_b, w_gate_up,
                 b_gate_up, w_down, b_down):
    splits = (ATTN_W, KV_W, KV_W, IDX_W, D_IDX, N_IDX_HEADS, RWKV_PROJ_W, D_MODEL, D_MODEL)
    cuts = np.cumsum(splits)[:-1].tolist()
    wq, wk, wv, wiq, wik, wiw, wpr, wga, wgb = jnp.split(w_in.astype(BF16), cuts, axis=-1)
    wikw = jnp.concatenate([wik, wiw, jnp.zeros((D_MODEL, LANES - D_IDX - N_IDX_HEADS), BF16)], axis=-1)
    row = lambda z: z.reshape(1, -1).astype(F32)
    ge = _group_indicator(RWKV_W, RWKV_HEAD).astype(BF16)
    ge2 = _group_indicator(KV_W, HEAD_DIM).astype(BF16)
    return dict(
        norm_mix=row(norm_mix), wq=wq, wk=wk, wv=wv, wiq=wiq, wikw=wikw, wpr=wpr, wga=wga, wgb=wgb,
        q_norm_t=row(jnp.tile(q_norm, N_Q_HEADS)), k_norm_t=row(jnp.tile(k_norm, N_KV_HEADS)),
        ik_norm_t=row(idx_k_norm),
        ge=ge, get=ge.T, ge2=ge2, ge2t=ge2.T,
        shift_mu=row(shift_mu), w0=row(w0), w_lora_up=w_lora_up, a0=row(a0), a_lora_up=a_lora_up,
        g_lora_up=g_lora_up, k_k=row(k_k), k_a=row(k_a), r_k=row(r_k), ln_x_w=row(ln_x_w), ln_x_b=row(ln_x_b),
        w_proj_a=w_proj_a.astype(BF16), w_proj_b=w_proj_b.astype(BF16), w_out=w_out.astype(BF16),
        norm_ffn=row(norm_ffn), router_w=router_w, router_b=row(router_b),
        w_gate_up=w_gate_up, b_gate_up=b_gate_up[:, None, :], w_down=w_down, b_down=b_down[:, None, :],
    )


def _pad_axis(z, axis, size):
    if z.shape[axis] == size:
        return z
    pad = [(0, 0)] * z.ndim
    pad[axis] = (0, size - z.shape[axis])
    return jnp.pad(z, pad)


def _group(x, p, *, tm):
    b, t, _ = x.shape
    names = ("q", "k", "kb", "v", "vb", "iq", "ik", "ikb", "iw", "pr", "ga", "gb")
    g = dict(zip(names, _in_proj(x.reshape(b * t, D_MODEL), p, tm)), b=b, t=t)
    g["pr"] = g["pr"].reshape(b, t, RWKV_PROJ_W)
    return g


def _attend(g, ik_all, k_all, v_all, *, tq, tk, q_offset, top):
    b, t = g["b"], g["t"]
    tp = -(-t // tq) * tq
    n_keys = -(-ik_all.shape[1] // tk) * tk
    seq = lambda z: _pad_axis(z.reshape(b, t, -1), 1, tp)
    keys = lambda z: _pad_axis(z, 1, n_keys)
    oa = _dsa(seq(g["iq"]), seq(g["iw"]), seq(g["q"]), keys(ik_all), keys(k_all), keys(v_all),
              tq=tq, tk=tk, q_offset=q_offset, top=top)
    return oa[:, :t].reshape(b * t, ATTN_W)


def _mix(g, shift0, wkv0, p, *, tt, chunk):
    b, t = g["b"], g["t"]
    tp = -(-t // tt) * tt
    ob, z = _rwkv(_pad_axis(g["pr"], 1, tp), shift0[:, None, :], jnp.swapaxes(wkv0, -1, -2), p,
                  tt=tt, chunk=chunk, t_valid=min(t, tt) if tp != t else tt)
    return ob[:, :t].reshape(b * t, RWKV_W), jnp.swapaxes(z, -1, -2)


def kernel(x_prompt, x_sample, cache_k, cache_v, cache_idx_k, page_table, state_wkv, state_shift, norm_mix, w_in, q_norm, k_norm, idx_k_norm, shift_mu, w0, w_lora_up, a0, a_lora_up, g_lora_up, k_k, k_a, r_k, ln_x_w, ln_x_b, w_proj_a, w_proj_b, w_out, norm_ffn, router_w, router_b, w_gate_up, b_gate_up, w_down, b_down):
    depth = norm_mix.shape[0]
    assert depth == 1
    params = (norm_mix, w_in, q_norm, k_norm, idx_k_norm, shift_mu, w0, w_lora_up, a0, a_lora_up, g_lora_up, k_k, k_a,
              r_k, ln_x_w, ln_x_b, w_proj_a, w_proj_b, w_out, norm_ffn, router_w, router_b, w_gate_up, b_gate_up,
              w_down, b_down)
    p = _prep_params(*[z[0] for z in params])
    bp, sp, _ = x_prompt.shape
    bs, ts, _ = x_sample.shape
    n_p, n_s = bp * sp, bs * ts
    past = page_table.shape[1] * PAGE_SIZE

    gp = _group(x_prompt, p, tm=min(256, n_p))
    k_p = gp["k"].reshape(bp, sp, N_KV_HEADS, HEAD_DIM)
    v_p = gp["v"].reshape(bp, sp, N_KV_HEADS, HEAD_DIM)
    ik_p = gp["ik"].reshape(bp, sp, D_IDX)
    oa_p = _attend(gp, gp["ikb"].reshape(bp, sp, D_IDX), gp["kb"].reshape(bp, sp, KV_W),
                   gp["vb"].reshape(bp, sp, KV_W), tq=min(128, sp), tk=min(512, sp), q_offset=0,
                   top=min(TOPK_MAX, sp // 4))
    ob_p, wkv_p = _mix(gp, jnp.zeros((bp, RWKV_PROJ_W), F32),
                       jnp.zeros((bp, N_RWKV_HEADS, RWKV_HEAD, RWKV_HEAD), F32), p,
                       tt=min(256, sp), chunk=min(64, sp))

    gs = _group(x_sample, p, tm=min(256, n_s))
    k_s = gs["k"].reshape(bs, ts, N_KV_HEADS, HEAD_DIM)
    v_s = gs["v"].reshape(bs, ts, N_KV_HEADS, HEAD_DIM)
    ik_s = gs["ik"].reshape(bs, ts, D_IDX)
    paged = lambda cache, w: cache[0][page_table].reshape(bs, past, w).astype(BF16)
    ik_all = jnp.concatenate([paged(cache_idx_k, D_IDX), gs["ikb"].reshape(bs, ts, D_IDX)], axis=1)
    k_all = jnp.concatenate([paged(cache_k, KV_W), gs["kb"].reshape(bs, ts, KV_W)], axis=1)
    v_all = jnp.concatenate([paged(cache_v, KV_W), gs["vb"].reshape(bs, ts, KV_W)], axis=1)
    oa_s = _attend(gs, ik_all, k_all, v_all, tq=16, tk=5 * LANES, q_offset=past, top=min(TOPK_MAX, (past + ts) // 4))
    ob_s, wkv_s = _mix(gs, state_shift[0], state_wkv[0], p, tt=8, chunk=8)

    h_p, hn_p, idx_p, gate_p = _merge(x_prompt.reshape(n_p, D_MODEL), oa_p, ob_p, gp["ga"], gp["gb"], p,
                                      tm=min(256, n_p))
    h_s, hn_s, idx_s, gate_s = _merge(x_sample.reshape(n_s, D_MODEL), oa_s, ob_s, gs["ga"], gs["gb"], p,
                                      tm=min(256, n_s))
    hn = jnp.concatenate([hn_p, hn_s], axis=0)
    top_idx = jnp.concatenate([idx_p[:, :TOP_K], idx_s[:, :TOP_K]], axis=0)
    gate = jnp.concatenate([gate_p[:, :TOP_K], gate_s[:, :TOP_K]], axis=0)
    f = _moe(hn, top_idx, gate, p, bm=512)
    y_p = (h_p + f[:n_p]).reshape(bp, sp, D_MODEL)
    y_s = (h_s + f[n_p:]).reshape(bs, ts, D_MODEL)

    st = lambda z: z[None]
    return (y_p, y_s, st(k_p), st(v_p), st(ik_p), st(wkv_p), st(gp["pr"][:, -1]),
            st(k_s), st(v_s), st(ik_s), st(wkv_s), st(gs["pr"][:, -1]))
```

```python
import functools

import jax
import jax.numpy as jnp
import numpy as np
from jax import lax
from jax.experimental import pallas as pl
from jax.experimental.pallas import tpu as pltpu

D_MODEL = 1024
PAGE_SIZE = 128
HEAD_DIM = 64
N_Q_HEADS = 8
N_KV_HEADS = 2
Q_PER_KV = N_Q_HEADS // N_KV_HEADS
ATTN_W = N_Q_HEADS * HEAD_DIM
KV_W = N_KV_HEADS * HEAD_DIM
ATTN_SCALE = HEAD_DIM ** -0.5
N_IDX_HEADS = 8
D_IDX = 64
IDX_W = N_IDX_HEADS * D_IDX
IDX_SCALE = (N_IDX_HEADS * D_IDX) ** -0.5
TOPK_MAX = 256
RWKV_HEAD = 64
N_RWKV_HEADS = 8
RWKV_W = N_RWKV_HEADS * RWKV_HEAD
W_LORA = 64
A_LORA = 64
G_LORA = 128
RWKV_PROJ_W = 3 * RWKV_W + W_LORA + A_LORA + G_LORA
LNX_EPS = 64e-5
N_EXPERTS = 32
TOP_K = 4
D_FF = 1024
SWIGLU_LIMIT = 7.0
SWIGLU_ALPHA = 1.702
NORM_EPS = 1e-6

LANES = 128
VMEM_LIMIT = 56 * 1024 * 1024
INT_MIN = -(2 ** 31)
NEG_BIG = -1e30
F32 = jnp.float32
BF16 = jnp.bfloat16
HI = lax.Precision.HIGHEST


def _dot(a, b, precision=None):
    return jnp.dot(a, b, preferred_element_type=F32, precision=precision)


def _dot_nt(a, b, precision=None):
    return lax.dot_general(a, b, (((1,), (1,)), ((), ())), preferred_element_type=F32, precision=precision)


def _dot_tn(a, b, precision=None):
    return lax.dot_general(a, b, (((0,), (0,)), ((), ())), preferred_element_type=F32, precision=precision)


def _group_indicator(width, group):
    r = np.arange(width) // group
    return jnp.asarray((r[:, None] == np.arange(LANES)[None, :]).astype(np.float32))


def _split(x, terms=2):
    parts = []
    for _ in range(terms - 1):
        hi = x.astype(BF16)
        parts.append(hi)
        x = x - hi.astype(F32)
    parts.append(x.astype(BF16))
    return parts


def _mm(a, b, dot=None):
    dot = dot or _dot
    return dot(a[0], b[0]) + (dot(a[0], b[1]) + dot(a[1], b[0]))


def _group_sum(x, ge, get):
    s = sum(_dot(part, ge) for part in _split(x))
    return sum(_dot(part, get) for part in _split(s))


def _const_spec(shape):
    nd = len(shape)
    return pl.BlockSpec(shape, lambda *_: (0,) * nd)


def _inproj_kernel(x_ref, g_ref, wq_ref, wk_ref, wv_ref, wiq_ref, wikw_ref, wpr_ref, wga_ref, wgb_ref,
                   qn_ref, kn_ref, ikn_ref, ge_ref, get_ref, ge2_ref, ge2t_ref,
                   q_out, k_out, kb_out, v_out, vb_out, iq_out, ik_out, ikb_out, iw_out, pr_out, ga_out, gb_out):
    x = x_ref[...]
    ms = jnp.mean(x * x, axis=-1, keepdims=True)
    xn = (x * lax.rsqrt(ms + NORM_EPS) * g_ref[...]).astype(BF16)

    q = _dot(xn, wq_ref[...])
    qs = _group_sum(q * q, ge_ref[...], get_ref[...]) * (1.0 / HEAD_DIM)
    q_out[...] = (q * lax.rsqrt(qs + NORM_EPS) * qn_ref[...]).astype(q_out.dtype)

    k = _dot(xn, wk_ref[...])
    ks = _group_sum(k * k, ge2_ref[...], ge2t_ref[...]) * (1.0 / HEAD_DIM)
    kn = k * lax.rsqrt(ks + NORM_EPS) * kn_ref[...]
    k_out[...] = kn
    kb_out[...] = kn.astype(kb_out.dtype)

    v = _dot(xn, wv_ref[...])
    v_out[...] = v
    vb_out[...] = v.astype(vb_out.dtype)
    iq_out[...] = _dot(xn, wiq_ref[...]).astype(iq_out.dtype)

    ikw = _dot(xn, wikw_ref[...])
    lane = lax.broadcasted_iota(jnp.int32, ikw.shape, 1)
    iks = jnp.sum(jnp.where(lane < D_IDX, ikw * ikw, 0.0), axis=-1, keepdims=True) * (1.0 / D_IDX)
    ikn = ikw[:, :D_IDX] * lax.rsqrt(iks + NORM_EPS) * ikn_ref[...]
    ik_out[...] = ikn
    ikb_out[...] = ikn.astype(ikb_out.dtype)
    iw_out[...] = ikw[:, D_IDX:D_IDX + N_IDX_HEADS]

    pr_out[...] = _dot(xn, wpr_ref[...])
    ga_out[...] = jax.nn.sigmoid(_dot(xn, wga_ref[...])).astype(ga_out.dtype)
    gb_out[...] = jax.nn.sigmoid(_dot(xn, wgb_ref[...])).astype(gb_out.dtype)


def _in_proj(x2d, p, tm):
    n = x2d.shape[0]
    widths = (ATTN_W, KV_W, KV_W, KV_W, KV_W, IDX_W, D_IDX, D_IDX, N_IDX_HEADS, RWKV_PROJ_W, D_MODEL, D_MODEL)
    dtypes = (BF16, F32, BF16, F32, BF16, BF16, F32, BF16, F32, F32, BF16, BF16)
    row = lambda w: pl.BlockSpec((tm, w), lambda i: (i, 0))
    consts = (p["norm_mix"], p["wq"], p["wk"], p["wv"], p["wiq"], p["wikw"], p["wpr"], p["wga"], p["wgb"],
              p["q_norm_t"], p["k_norm_t"], p["ik_norm_t"], p["ge"], p["get"], p["ge2"], p["ge2t"])
    return pl.pallas_call(
        _inproj_kernel,
        grid=(n // tm,),
        in_specs=[row(D_MODEL)] + [_const_spec(c.shape) for c in consts],
        out_specs=[row(w) for w in widths],
        out_shape=[jax.ShapeDtypeStruct((n, w), d) for w, d in zip(widths, dtypes)],
        compiler_params=pltpu.CompilerParams(dimension_semantics=("parallel",), vmem_limit_bytes=VMEM_LIMIT),
        name="in_proj",
    )(x2d, *consts)


def _stack_heads(x, heads, width):
    return jnp.concatenate([x[:, h * width:(h + 1) * width] for h in heads], axis=0)


def _score_keys(d, iw, tq):
    acc = jnp.zeros((tq, d.shape[1]), F32)
    for h in range(N_IDX_HEADS):
        acc = acc + jnp.maximum(d[h * tq:(h + 1) * tq], 0.0) * iw[:, h:h + 1]
    sc = acc * IDX_SCALE
    sc = jnp.where(sc == 0.0, 0.0, sc)
    bits = pltpu.bitcast(sc, jnp.int32)
    return bits ^ ((bits >> 31) & 0x7FFFFFFF)


def _select_threshold(keys_ref, n_kt, *, tq, tk, top):
    lane_pos = lax.broadcasted_iota(jnp.int32, (tq, tk), 1)

    def count(pred):
        def body(kt, c):
            start = pl.multiple_of(kt * tk, tk)
            m = pred(keys_ref[:, pl.ds(start, tk)], start).astype(jnp.int32)
            part = m[:, 0:LANES]
            for j in range(1, tk // LANES):
                part = part + m[:, j * LANES:(j + 1) * LANES]
            return c + part
        c = lax.fori_loop(0, n_kt, body, jnp.zeros((tq, LANES), jnp.int32))
        return jnp.sum(c, axis=-1, keepdims=True)

    c0 = count(lambda k, s: k >= 0)
    t = jnp.where(c0 >= top, 0, INT_MIN).astype(jnp.int32)

    def bit_step(i, t):
        cand = t | jnp.left_shift(jnp.int32(1), 30 - i)
        c = count(lambda k, s: k >= cand)
        return jnp.where(c >= top, cand, t)

    t = lax.fori_loop(0, 31, bit_step, t)
    t = jnp.maximum(t, INT_MIN + 1)

    excess = count(lambda k, s: k >= t) > top

    @pl.when(jnp.max(excess.astype(jnp.int32)) > 0)
    def _():
        keep = top - count(lambda k, s: k > t)

        def idx_step(i, lim):
            cand = lim | jnp.left_shift(jnp.int32(1), 14 - i)
            c = count(lambda k, s: ((k == t) & (s + lane_pos < cand)))
            return jnp.where(c <= keep, cand, lim)

        lim = lax.fori_loop(0, 15, idx_step, jnp.zeros((tq, 1), jnp.int32))

        def demote(kt, carry):
            start = pl.multiple_of(kt * tk, tk)
            k = keys_ref[:, pl.ds(start, tk)]
            drop = (k == t) & (start + lane_pos >= lim) & excess
            keys_ref[:, pl.ds(start, tk)] = jnp.where(drop, t - 1, k)
            return carry

        lax.fori_loop(0, n_kt, demote, 0)

    return t


def _softmax_step(state, s, bias, pv, tq):
    m_i, l_i, acc = state
    gq, tk = s.shape
    s = (s.reshape(Q_PER_KV, tq, tk) + bias[None]).reshape(gq, tk)
    m_n = jnp.maximum(m_i, jnp.max(s, axis=-1, keepdims=True))
    alpha = jnp.exp(m_i - m_n)
    pm = jnp.exp(s - m_n)
    l_n = alpha * l_i + jnp.sum(pm, axis=-1, keepdims=True)
    return m_n, l_n, alpha * acc + pv(pm.astype(BF16))


def _softmax_init(tq):
    gq = Q_PER_KV * tq
    return tuple((jnp.full((gq, 1), NEG_BIG, F32), jnp.zeros((gq, 1), F32), jnp.zeros((gq, HEAD_DIM), F32))
                 for _ in range(N_KV_HEADS))


def _write_heads(o_ref, res, tq):
    for g in range(N_KV_HEADS):
        _, l_i, acc = res[g]
        og = acc / l_i
        for j in range(Q_PER_KV):
            h = g * Q_PER_KV + j
            o_ref[0, :, h * HEAD_DIM:(h + 1) * HEAD_DIM] = og[j * tq:(j + 1) * tq].astype(o_ref.dtype)


def _dsa_kernel(iq_ref, iw_ref, q_ref, ik_ref, k_ref, v_ref, o_ref, keys_ref, *, tq, tk, q_offset, n_keys, top):
    qi = pl.program_id(1)
    q_base = q_offset + qi * tq
    n_kt = jnp.minimum((q_base + tq + tk - 1) // tk, n_keys // tk)
    q_pos = q_base + lax.broadcasted_iota(jnp.int32, (tq, 1), 0)
    lane_pos = lax.broadcasted_iota(jnp.int32, (tq, tk), 1)

    iq = _stack_heads(iq_ref[0], range(N_IDX_HEADS), D_IDX)
    iw = iw_ref[0]

    def score_tile(kt, carry):
        start = pl.multiple_of(kt * tk, tk)
        key = _score_keys(_dot_nt(iq, ik_ref[0, pl.ds(start, tk), :]), iw, tq)
        keys_ref[:, pl.ds(start, tk)] = jnp.where(start + lane_pos <= q_pos, key, INT_MIN)
        return carry

    lax.fori_loop(0, n_kt, score_tile, 0)

    t = _select_threshold(keys_ref, n_kt, tq=tq, tk=tk, top=top)

    q_all = q_ref[0] * ATTN_SCALE
    qs = [_stack_heads(q_all, range(g * Q_PER_KV, (g + 1) * Q_PER_KV), HEAD_DIM) for g in range(N_KV_HEADS)]

    def attn_tile(kt, carry):
        start = pl.multiple_of(kt * tk, tk)
        bias = jnp.where(keys_ref[:, pl.ds(start, tk)] >= t, 0.0, NEG_BIG)
        k_t = k_ref[0, pl.ds(start, tk), :]
        v_t = v_ref[0, pl.ds(start, tk), :]
        new = []
        for g in range(N_KV_HEADS):
            hs = slice(g * HEAD_DIM, (g + 1) * HEAD_DIM)
            new.append(_softmax_step(carry[g], _dot_nt(qs[g], k_t[:, hs]), bias,
                                     lambda p, hs=hs: _dot(p, v_t[:, hs]), tq))
        return tuple(new)

    _write_heads(o_ref, lax.fori_loop(0, n_kt, attn_tile, _softmax_init(tq)), tq)


def _dsa(iq, iw, q, ik, k, v, *, tq, tk, q_offset, top):
    b, sq, _ = iq.shape
    n_keys = ik.shape[1]
    assert sq % tq == 0 and n_keys % tk == 0 and tk % LANES == 0
    kern = functools.partial(_dsa_kernel, tq=tq, tk=tk, q_offset=q_offset, n_keys=n_keys, top=top)
    return pl.pallas_call(
        kern,
        grid=(b, sq // tq),
        in_specs=[
            pl.BlockSpec((1, tq, IDX_W), lambda bi, qi: (bi, qi, 0)),
            pl.BlockSpec((1, tq, N_IDX_HEADS), lambda bi, qi: (bi, qi, 0)),
            pl.BlockSpec((1, tq, ATTN_W), lambda bi, qi: (bi, qi, 0)),
            pl.BlockSpec((1, n_keys, D_IDX), lambda bi, qi: (bi, 0, 0)),
            pl.BlockSpec((1, n_keys, KV_W), lambda bi, qi: (bi, 0, 0)),
            pl.BlockSpec((1, n_keys, KV_W), lambda bi, qi: (bi, 0, 0)),
        ],
        out_specs=pl.BlockSpec((1, tq, ATTN_W), lambda bi, qi: (bi, qi, 0)),
        out_shape=jax.ShapeDtypeStruct((b, sq, ATTN_W), BF16),
        scratch_shapes=[pltpu.VMEM((tq, n_keys), jnp.int32)],
        compiler_params=pltpu.CompilerParams(dimension_semantics=("parallel", "arbitrary"),
                                             vmem_limit_bytes=VMEM_LIMIT),
        name="dsa",
    )(iq, iw, q, ik, k, v)


def _dsa_paged_kernel(pt_ref, iq_ref, iw_ref, q_ref, ikn_ref, kn_ref, vn_ref, cik_ref, ck_ref, cv_ref, o_ref,
                      ik_buf, k_buf, v_buf, sems, keys_ref, *, tq, n_pages, ppt, tk_sel, top):
    b = pl.program_id(0)
    slot = b % 2
    past = n_pages * PAGE_SIZE
    tk = ppt * PAGE_SIZE

    def page_copies(bi, sl, p):
        page = pt_ref[bi * n_pages + p]
        return (pltpu.make_async_copy(cik_ref.at[page], ik_buf.at[sl, p], sems.at[sl, 0]),
                pltpu.make_async_copy(ck_ref.at[page], k_buf.at[sl, p], sems.at[sl, 1]),
                pltpu.make_async_copy(cv_ref.at[page], v_buf.at[sl, p], sems.at[sl, 2]))

    def fetch(bi, sl):
        def body(p, carry):
            for cp in page_copies(bi, sl, p):
                cp.start()
            return carry
        lax.fori_loop(0, n_pages, body, 0)

    @pl.when(b == 0)
    def _():
        fetch(0, 0)

    @pl.when(b + 1 < pl.num_programs(0))
    def _():
        fetch(b + 1, 1 - slot)

    def wait_page(p, carry):
        for cp in page_copies(b, slot, p):
            cp.wait()
        return carry

    lax.fori_loop(0, n_pages, wait_page, 0)

    def page_cols(buf, i, idx=()):
        return jnp.concatenate([buf[(slot, i * ppt + j) + idx] for j in range(ppt)], axis=1).astype(BF16)

    iq = _stack_heads(iq_ref[0], range(N_IDX_HEADS), D_IDX)
    iw = iw_ref[0]
    row = lax.broadcasted_iota(jnp.int32, (tq, PAGE_SIZE), 0)
    lane = lax.broadcasted_iota(jnp.int32, (tq, PAGE_SIZE), 1)

    def score_tile(i, carry):
        start = pl.multiple_of(i * tk, tk)
        keys_ref[:, pl.ds(start, tk)] = _score_keys(_dot(iq, page_cols(ik_buf, i)), iw, tq)
        return carry

    lax.fori_loop(0, n_pages // ppt, score_tile, 0)
    new_keys = _score_keys(_dot(iq, ikn_ref[0]), iw, tq)
    keys_ref[:, past:past + PAGE_SIZE] = jnp.where(lane <= row, new_keys, INT_MIN)

    t = _select_threshold(keys_ref, (past + PAGE_SIZE) // tk_sel, tq=tq, tk=tk_sel, top=top)

    q_all = q_ref[0] * ATTN_SCALE
    qs = [_stack_heads(q_all, range(g * Q_PER_KV, (g + 1) * Q_PER_KV), HEAD_DIM) for g in range(N_KV_HEADS)]

    def attn_tile(i, carry):
        start = pl.multiple_of(i * tk, tk)
        bias = jnp.where(keys_ref[:, pl.ds(start, tk)] >= t, 0.0, NEG_BIG)
        new = []
        for g in range(N_KV_HEADS):
            v_t = page_cols(v_buf, i, (g,))
            new.append(_softmax_step(carry[g], _dot(qs[g], page_cols(k_buf, i, (g,))), bias,
                                     lambda p, v_t=v_t: _dot_nt(p, v_t), tq))
        return tuple(new)

    res = lax.fori_loop(0, n_pages // ppt, attn_tile, _softmax_init(tq))
    bias = jnp.where(keys_ref[:, past:past + PAGE_SIZE] >= t, 0.0, NEG_BIG)
    res = tuple(_softmax_step(res[g], _dot(qs[g], kn_ref[0, g]), bias,
                              lambda p, g=g: _dot_nt(p, vn_ref[0, g]), tq) for g in range(N_KV_HEADS))
    _write_heads(o_ref, res, tq)


def _dsa_paged(iq, iw, q, ikn, kn, vn, cik_t, ck_t, cv_t, page_table, *, tq, top):
    b = iq.shape[0]
    n_pages = page_table.shape[1]
    ppt = next(d for d in (4, 2, 1) if n_pages % d == 0)
    n_lane_tiles = n_pages + 1
    tk_sel = LANES * next(d for d in range(8, 0, -1) if n_lane_tiles % d == 0)
    kern = functools.partial(_dsa_paged_kernel, tq=tq, n_pages=n_pages, ppt=ppt, tk_sel=tk_sel, top=top)
    per_seq = lambda *blk: pl.BlockSpec((1,) + blk, lambda bi, pt: (bi,) + (0,) * len(blk))
    grid_spec = pltpu.PrefetchScalarGridSpec(
        num_scalar_prefetch=1,
        grid=(b,),
        in_specs=[per_seq(tq, IDX_W), per_seq(tq, N_IDX_HEADS), per_seq(tq, ATTN_W),
                  per_seq(D_IDX, PAGE_SIZE), per_seq(N_KV_HEADS, HEAD_DIM, PAGE_SIZE),
                  per_seq(N_KV_HEADS, HEAD_DIM, PAGE_SIZE),
                  pl.BlockSpec(memory_space=pl.ANY), pl.BlockSpec(memory_space=pl.ANY),
                  pl.BlockSpec(memory_space=pl.ANY)],
        out_specs=per_seq(tq, ATTN_W),
        scratch_shapes=[pltpu.VMEM((2, n_pages, D_IDX, PAGE_SIZE), F32),
                        pltpu.VMEM((2, n_pages, N_KV_HEADS, HEAD_DIM, PAGE_SIZE), F32),
                        pltpu.VMEM((2, n_pages, N_KV_HEADS, HEAD_DIM, PAGE_SIZE), F32),
                        pltpu.SemaphoreType.DMA((2, 3)),
                        pltpu.VMEM((tq, (n_pages + 1) * PAGE_SIZE), jnp.int32)],
    )
    return pl.pallas_call(
        kern,
        grid_spec=grid_spec,
        out_shape=jax.ShapeDtypeStruct((b, tq, ATTN_W), BF16),
        compiler_params=pltpu.CompilerParams(dimension_semantics=("arbitrary",), vmem_limit_bytes=VMEM_LIMIT),
        name="dsa_paged",
    )(page_table.reshape(-1), iq, iw, q, ikn, kn, vn, cik_t, ck_t, cv_t)


def _rwkv_kernel(pr_ref, sh0_ref, z0_ref, mu_ref, w0_ref, wup_ref, a0_ref, aup_ref, gup_ref, kk_ref, ka_ref,
                 rk_ref, lnw_ref, lnb_ref, ge_ref, get_ref,
                 ob_ref, zout_ref,
                 z_scr, prev_scr, r_s, k_s, v_s, a_s, b_s, lw_s, y_s, *, tt, chunk, t_valid):
    ti = pl.program_id(1)
    n_t = pl.num_programs(1)
    nh, hd = N_RWKV_HEADS, RWKV_HEAD

    @pl.when(ti == 0)
    def _():
        z_scr[...] = z0_ref[0]
        prev_scr[...] = sh0_ref[0]

    gsum = lambda z: _group_sum(z, ge_ref[...], get_ref[...])

    pr = pr_ref[0]
    row = lax.broadcasted_iota(jnp.int32, (tt, 1), 0)
    prev = jnp.where(row == 0, prev_scr[...], pltpu.roll(pr, 1, 0))
    prev_scr[...] = pr[tt - 1:tt]
    m = pr + mu_ref[...] * (prev - pr)
    r = m[:, 0:RWKV_W]
    k = m[:, RWKV_W:2 * RWKV_W]
    v = m[:, 2 * RWKV_W:3 * RWKV_W]
    o = 3 * RWKV_W
    wd = m[:, o:o + W_LORA]
    ad = m[:, o + W_LORA:o + W_LORA + A_LORA]
    gd = m[:, o + W_LORA + A_LORA:]

    u = -(w0_ref[...] + _dot(jnp.tanh(wd), wup_ref[...], HI))
    softplus = jnp.maximum(u, 0.0) + jnp.log(1.0 + jnp.exp(-jnp.abs(u)))
    lw = -jnp.exp(-softplus - 0.5)
    a = jax.nn.sigmoid(a0_ref[...] + _dot(ad, aup_ref[...], HI))
    gate = _dot(jax.nn.sigmoid(gd), gup_ref[...], HI)
    kk = k * kk_ref[...]
    kk = kk / jnp.maximum(jnp.sqrt(gsum(kk * kk)), 1e-12)
    k2 = k * (1.0 + (a - 1.0) * ka_ref[...])
    bonus = gsum(r * k2 * rk_ref[...])
    av = -kk
    bv = kk * a
    if t_valid < tt:
        ok = (row < t_valid).astype(F32)
        k2, v, av, bv, lw = k2 * ok, v * ok, av * ok, bv * ok, lw * ok
    r_s[...] = r
    k_s[...] = k2
    v_s[...] = v
    a_s[...] = av
    b_s[...] = bv
    lw_s[...] = lw

    c = chunk
    ri = lax.broadcasted_iota(jnp.int32, (c, c), 0)
    ci = lax.broadcasted_iota(jnp.int32, (c, c), 1)
    tri_incl = (ci <= ri)
    tri_strict = (ci < ri)
    ltri = tri_incl.astype(BF16)
    eye_h = (lax.broadcasted_iota(jnp.int32, (hd, hd), 0) == lax.broadcasted_iota(jnp.int32, (hd, hd), 1))
    n_double = max(int(np.ceil(np.log2(c))), 1)
    heads = range(nh)
    hsl = [slice(h * hd, (h + 1) * hd) for h in heads]

    def chunk_body(ci_, carry):
        s0 = pl.multiple_of(ci_ * c, c)
        sl = pl.ds(s0, c)
        lwc = lw_s[sl, :]
        cum = sum(_dot(ltri, part) for part in _split(lwc, 3))
        cum_end = cum[c - 1:c, :]
        g_end = jnp.exp(cum_end)
        at = a_s[sl, :] * jnp.exp(cum - lwc)
        rt = r_s[sl, :] * jnp.exp(cum)
        g_inv = jnp.exp(-cum)
        g_tail = jnp.exp(cum_end - cum)
        bt = b_s[sl, :] * g_inv
        kt = k_s[sl, :] * g_inv
        bc = b_s[sl, :] * g_tail
        kc = k_s[sl, :] * g_tail
        vc = v_s[sl, :]

        left = [_split(jnp.concatenate([at[:, s], rt[:, s]], axis=0)) for s in hsl]
        right = [_split(jnp.concatenate([bt[:, s], kt[:, s]], axis=0)) for s in hsl]
        amat = [_mm(left[h], right[h], _dot_nt) for h in heads]
        a_ab = [jnp.where(tri_strict, amat[h][:c, :c], 0.0) for h in heads]
        a_ak = [_split(jnp.where(tri_strict, amat[h][:c, c:], 0.0)) for h in heads]
        a_rb = [_split(jnp.where(tri_incl, amat[h][c:, :c], 0.0)) for h in heads]
        a_rk = [_split(jnp.where(tri_incl, amat[h][c:, c:], 0.0)) for h in heads]
        vh = [_split(vc[:, s]) for s in hsl]
        akv = [_mm(a_ak[h], vh[h]) for h in heads]
        uu = [jnp.concatenate([at[:, hsl[h]], akv[h]], axis=1) for h in heads]
        pw = a_ab
        for step in range(n_double):
            pws = [_split(z) for z in pw]
            uus = [_split(z) for z in uu]
            uu = [uu[h] + _mm(pws[h], uus[h]) for h in heads]
            if step + 1 < n_double:
                pw = [_mm(pws[h], pws[h]) for h in heads]
        uus = [_split(z) for z in uu]
        x1 = [_mm(a_rb[h], uus[h]) for h in heads]
        x2 = [_mm(a_rk[h], vh[h]) for h in heads]
        mn = [_mm(_split(bc[:, hsl[h]]), uus[h], _dot_tn) for h in heads]
        nk = [_mm(_split(kc[:, hsl[h]]), vh[h], _dot_tn) for h in heads]
        pm = []
        for h in heads:
            p2 = rt[:, hsl[h]] + x1[h][:, :hd]
            mh = jnp.where(eye_h, g_end[:, hsl[h]], 0.0) + mn[h][:, :hd]
            pm.append(_split(jnp.concatenate([p2, mh], axis=0)))
        res = [_mm(pm[h], _split(z_scr[h])) for h in heads]
        for h in heads:
            y_s[sl, hsl[h]] = res[h][:c] + x1[h][:, hd:] + x2[h]
            z_scr[h] = res[h][c:] + mn[h][:, hd:] + nk[h]
        return carry

    lax.fori_loop(0, tt // c, chunk_body, 0)

    y = y_s[...]
    mean = gsum(y) * (1.0 / hd)
    dlt = y - mean
    var = gsum(dlt * dlt) * (1.0 / hd)
    yn = dlt * lax.rsqrt(var + LNX_EPS) * lnw_ref[...] + lnb_ref[...]
    ob_ref[0] = ((yn + bonus * v_s[...]) * gate).astype(ob_ref.dtype)

    @pl.when(ti == n_t - 1)
    def _():
        zout_ref[0] = z_scr[...]


def _rwkv(pr, shift0, z0, p, *, tt, chunk, t_valid):
    b, t, _ = pr.shape
    assert t % tt == 0 and tt % chunk == 0
    consts = (p["shift_mu"], p["w0"], p["w_lora_up"], p["a0"], p["a_lora_up"], p["g_lora_up"], p["k_k"], p["k_a"],
              p["r_k"], p["ln_x_w"], p["ln_x_b"], p["ge"], p["get"])
    kern = functools.partial(_rwkv_kernel, tt=tt, chunk=chunk, t_valid=t_valid)
    wide = lambda: pltpu.VMEM((tt, RWKV_W), F32)
    return pl.pallas_call(
        kern,
        grid=(b, t // tt),
        in_specs=[
            pl.BlockSpec((1, tt, RWKV_PROJ_W), lambda bi, ti: (bi, ti, 0)),
            pl.BlockSpec((1, 1, RWKV_PROJ_W), lambda bi, ti: (bi, 0, 0)),
            pl.BlockSpec((1, N_RWKV_HEADS, RWKV_HEAD, RWKV_HEAD), lambda bi, ti: (bi, 0, 0, 0)),
        ] + [_const_spec(c.shape) for c in consts],
        out_specs=[
            pl.BlockSpec((1, tt, RWKV_W), lambda bi, ti: (bi, ti, 0)),
            pl.BlockSpec((1, N_RWKV_HEADS, RWKV_HEAD, RWKV_HEAD), lambda bi, ti: (bi, 0, 0, 0)),
        ],
        out_shape=[jax.ShapeDtypeStruct((b, t, RWKV_W), BF16),
                   jax.ShapeDtypeStruct((b, N_RWKV_HEADS, RWKV_HEAD, RWKV_HEAD), F32)],
        scratch_shapes=[pltpu.VMEM((N_RWKV_HEADS, RWKV_HEAD, RWKV_HEAD), F32),
                        pltpu.VMEM((1, RWKV_PROJ_W), F32),
                        wide(), wide(), wide(), wide(), wide(), wide(), wide()],
        compiler_params=pltpu.CompilerParams(dimension_semantics=("parallel", "arbitrary"),
                                             vmem_limit_bytes=VMEM_LIMIT),
        name="rwkv",
    )(pr, shift0, z0, *consts)


def _merge_kernel(x_ref, oa_ref, ob_ref, ga_ref, gb_ref, wa_ref, wb_ref, wo_ref, nf_ref, rw_ref, rb_ref,
                  h_out, hn_out, idx_out, gate_out):
    ma = _dot(oa_ref[...], wa_ref[...])
    mb = _dot(ob_ref[...], wb_ref[...])
    mm = ga_ref[...].astype(F32) * ma + gb_ref[...].astype(F32) * mb
    h = x_ref[...] + _dot(mm.astype(BF16), wo_ref[...])
    h_out[...] = h
    ms = jnp.mean(h * h, axis=-1, keepdims=True)
    hn = h * lax.rsqrt(ms + NORM_EPS) * nf_ref[...]
    hn_out[...] = hn.astype(hn_out.dtype)
    logits = _dot(hn, rw_ref[...], HI) + rb_ref[...]
    tm = logits.shape[0]
    lane = lax.broadcasted_iota(jnp.int32, logits.shape, 1)
    wide = lax.broadcasted_iota(jnp.int32, (tm, LANES), 1)
    idx_w = jnp.zeros((tm, LANES), jnp.int32)
    val_w = jnp.full((tm, LANES), -jnp.inf, F32)
    for kth in range(TOP_K):
        mx = jnp.max(logits, axis=-1, keepdims=True)
        ix = jnp.min(jnp.where(logits == mx, lane, N_EXPERTS), axis=-1, keepdims=True)
        idx_w = jnp.where(wide == kth, ix, idx_w)
        val_w = jnp.where(wide == kth, mx, val_w)
        logits = jnp.where(lane == ix, -jnp.inf, logits)
    e = jnp.exp(val_w - jnp.max(val_w, axis=-1, keepdims=True))
    idx_out[...] = idx_w
    gate_out[...] = e / jnp.sum(e, axis=-1, keepdims=True)


def _merge(x2d, oa, ob, ga, gb, p, tm):
    n = x2d.shape[0]
    row = lambda w: pl.BlockSpec((tm, w), lambda i: (i, 0))
    consts = (p["w_proj_a"], p["w_proj_b"], p["w_out"], p["norm_ffn"], p["router_w"], p["router_b"])
    return pl.pallas_call(
        _merge_kernel,
        grid=(n // tm,),
        in_specs=[row(D_MODEL), row(ATTN_W), row(RWKV_W), row(D_MODEL), row(D_MODEL)]
        + [_const_spec(c.shape) for c in consts],
        out_specs=[row(D_MODEL), row(D_MODEL), row(LANES), row(LANES)],
        out_shape=[jax.ShapeDtypeStruct((n, D_MODEL), F32), jax.ShapeDtypeStruct((n, D_MODEL), BF16),
                   jax.ShapeDtypeStruct((n, LANES), jnp.int32), jax.ShapeDtypeStruct((n, LANES), F32)],
        compiler_params=pltpu.CompilerParams(dimension_semantics=("parallel",), vmem_limit_bytes=VMEM_LIMIT),
        name="merge",
    )(x2d, oa, ob, ga, gb, *consts)


def _moe_kernel(be_ref, nb_ref, x_ref, wgu_ref, bgu_ref, wd_ref, bd_ref, o_ref, wgu_s, wd_s):
    i = pl.program_id(0)
    used = i < nb_ref[0]

    @pl.when(used & ((i == 0) | (be_ref[i] != be_ref[jnp.maximum(i - 1, 0)])))
    def _():
        wgu_s[...] = wgu_ref[0].astype(wgu_s.dtype)
        wd_s[...] = wd_ref[0].astype(wd_s.dtype)

    @pl.when(used)
    def _():
        hcat = _dot(x_ref[...], wgu_s[...]) + bgu_ref[0]
        glu = jnp.minimum(hcat[:, :D_FF], SWIGLU_LIMIT)
        lin = jnp.clip(hcat[:, D_FF:], -SWIGLU_LIMIT, SWIGLU_LIMIT)
        act = glu * jax.nn.sigmoid(SWIGLU_ALPHA * glu) * (lin + 1.0)
        o_ref[...] = _dot(act.astype(BF16), wd_s[...]) + bd_ref[0]

    @pl.when(jnp.logical_not(used))
    def _():
        o_ref[...] = jnp.zeros(o_ref.shape, o_ref.dtype)


def _moe_rows(xg, block_exp, n_used, p, bm):
    rows = xg.shape[0]
    grid_spec = pltpu.PrefetchScalarGridSpec(
        num_scalar_prefetch=2,
        grid=(rows // bm,),
        in_specs=[
            pl.BlockSpec((bm, D_MODEL), lambda i, be, nb: (i, 0)),
            pl.BlockSpec((1, D_MODEL, 2 * D_FF), lambda i, be, nb: (be[i], 0, 0)),
            pl.BlockSpec((1, 1, 2 * D_FF), lambda i, be, nb: (be[i], 0, 0)),
            pl.BlockSpec((1, D_FF, D_MODEL), lambda i, be, nb: (be[i], 0, 0)),
            pl.BlockSpec((1, 1, D_MODEL), lambda i, be, nb: (be[i], 0, 0)),
        ],
        out_specs=pl.BlockSpec((bm, D_MODEL), lambda i, be, nb: (i, 0)),
        scratch_shapes=[pltpu.VMEM((D_MODEL, 2 * D_FF), BF16), pltpu.VMEM((D_FF, D_MODEL), BF16)],
    )
    return pl.pallas_call(
        _moe_kernel,
        grid_spec=grid_spec,
        out_shape=jax.ShapeDtypeStruct((rows, D_MODEL), F32),
        compiler_params=pltpu.CompilerParams(dimension_semantics=("arbitrary",), vmem_limit_bytes=VMEM_LIMIT),
        name="moe",
    )(block_exp, n_used, xg, p["w_gate_up"], p["b_gate_up"], p["w_down"], p["b_down"])


def _moe(hn, top_idx, gate, p, bm):
    n = hn.shape[0]
    nk = n * TOP_K
    e_flat = top_idx.reshape(-1)
    onehot = (e_flat[:, None] == jnp.arange(N_EXPERTS, dtype=jnp.int32)[None, :]).astype(jnp.int32)
    csum = jnp.cumsum(onehot, axis=0)
    counts = csum[-1]
    rank = jnp.take_along_axis(csum, e_flat[:, None], axis=1)[:, 0] - 1
    padded = (counts + bm - 1) // bm * bm
    pends = jnp.cumsum(padded)
    pstarts = pends - padded
    dest = pstarts[e_flat] + rank
    n_blocks = -(-nk // bm) + N_EXPERTS
    rows = n_blocks * bm
    tok_flat = jnp.arange(nk, dtype=jnp.int32) // TOP_K
    row_tok = jnp.zeros((rows,), jnp.int32).at[dest].set(tok_flat, unique_indices=True)
    block_start = jnp.arange(n_blocks, dtype=jnp.int32) * bm
    block_exp = jnp.minimum(jnp.sum((pends[None, :] <= block_start[:, None]).astype(jnp.int32), axis=1),
                            N_EXPERTS - 1)
    n_used = (pends[-1:] // bm).astype(jnp.int32)
    out = _moe_rows(hn[row_tok], block_exp, n_used, p, bm)
    dest_k = dest.reshape(n, TOP_K).T
    return (out[dest_k] * gate.T[:, :, None]).sum(axis=0)


def _prep_params(norm_mix, w_in, q_norm, k_norm, idx_k_norm, shift_mu, w0, w_lora_up, a0, a_lora_up, g_lora_up, k_k,
                 k_a, r_k, ln_x_w, ln_x_b, w_proj_a, w_proj_b, w_out, norm_ffn, router_w, router_b, w_gate_up,
                 b_gate_up, w_down, b_down):
    splits = (ATTN_W, KV_W, KV_W, IDX_W, D_IDX, N_IDX_HEADS, RWKV_PROJ_W, D_MODEL, D_MODEL)
    cuts = np.cumsum(splits)[:-1].tolist()
    wq, wk, wv, wiq, wik, wiw, wpr, wga, wgb = jnp.split(w_in.astype(BF16), cuts, axis=-1)
    wikw = jnp.concatenate([wik, wiw, jnp.zeros((D_MODEL, LANES - D_IDX - N_IDX_HEADS), BF16)], axis=-1)
    row = lambda z: z.reshape(1, -1).astype(F32)
    ge = _group_indicator(RWKV_W, RWKV_HEAD).astype(BF16)
    ge2 = _group_indicator(KV_W, HEAD_DIM).astype(BF16)
    return dict(
        norm_mix=row(norm_mix), wq=wq, wk=wk, wv=wv, wiq=wiq, wikw=wikw, wpr=wpr, wga=wga, wgb=wgb,
        q_norm_t=row(jnp.tile(q_norm, N_Q_HEADS)), k_norm_t=row(jnp.tile(k_norm, N_KV_HEADS)),
        ik_norm_t=row(idx_k_norm),
        ge=ge, get=ge.T, ge2=ge2, ge2t=ge2.T,
        shift_mu=row(shift_mu), w0=row(w0), w_lora_up=w_lora_up, a0=row(a0), a_lora_up=a_lora_up,
        g_lora_up=g_lora_up, k_k=row(k_k), k_a=row(k_a), r_k=row(r_k), ln_x_w=row(ln_x_w), ln_x_b=row(ln_x_b),
        w_proj_a=w_proj_a.astype(BF16), w_proj_b=w_proj_b.astype(BF16), w_out=w_out.astype(BF16),
        norm_ffn=row(norm_ffn), router_w=router_w, router_b=row(router_b),
        w_gate_up=w_gate_up, b_gate_up=b_gate_up[:, None, :], w_down=w_down, b_down=b_down[:, None, :],
    )


def _pad_axis(z, axis, size):
    if z.shape[axis] == size:
        return z
    pad = [(0, 0)] * z.ndim
    pad[axis] = (0, size - z.shape[axis])
    return jnp.pad(z, pad)


def _group(x, p, *, tm):
    b, t, _ = x.shape
    names = ("q", "k", "kb", "v", "vb", "iq", "ik", "ikb", "iw", "pr", "ga", "gb")
    g = dict(zip(names, _in_proj(x.reshape(b * t, D_MODEL), p, tm)), b=b, t=t)
    g["pr"] = g["pr"].reshape(b, t, RWKV_PROJ_W)
    return g


def _attend(g, ik_all, k_all, v_all, *, tq, tk, q_offset, top):
    b, t = g["b"], g["t"]
    tp = -(-t // tq) * tq
    n_keys = -(-ik_all.shape[1] // tk) * tk
    seq = lambda z: _pad_axis(z.reshape(b, t, -1), 1, tp)
    keys = lambda z: _pad_axis(z, 1, n_keys)
    oa = _dsa(seq(g["iq"]), seq(g["iw"]), seq(g["q"]), keys(ik_all), keys(k_all), keys(v_all),
              tq=tq, tk=tk, q_offset=q_offset, top=top)
    return oa[:, :t].reshape(b * t, ATTN_W)


def _mix(g, shift0, wkv0, p, *, tt, chunk):
    b, t = g["b"], g["t"]
    tp = -(-t // tt) * tt
    ob, z = _rwkv(_pad_axis(g["pr"], 1, tp), shift0[:, None, :], jnp.swapaxes(wkv0, -1, -2), p,
                  tt=tt, chunk=chunk, t_valid=min(t, tt) if tp != t else tt)
    return ob[:, :t].reshape(b * t, RWKV_W), jnp.swapaxes(z, -1, -2)


def kernel(x_prompt, x_sample, cache_k, cache_v, cache_idx_k, page_table, state_wkv, state_shift, norm_mix, w_in, q_norm, k_norm, idx_k_norm, shift_mu, w0, w_lora_up, a0, a_lora_up, g_lora_up, k_k, k_a, r_k, ln_x_w, ln_x_b, w_proj_a, w_proj_b, w_out, norm_ffn, router_w, router_b, w_gate_up, b_gate_up, w_down, b_down):
    depth = norm_mix.shape[0]
    assert depth == 1
    params = (norm_mix, w_in, q_norm, k_norm, idx_k_norm, shift_mu, w0, w_lora_up, a0, a_lora_up, g_lora_up, k_k, k_a,
              r_k, ln_x_w, ln_x_b, w_proj_a, w_proj_b, w_out, norm_ffn, router_w, router_b, w_gate_up, b_gate_up,
              w_down, b_down)
    p = _prep_params(*[z[0] for z in params])
    bp, sp, _ = x_prompt.shape
    bs, ts, _ = x_sample.shape
    n_p, n_s = bp * sp, bs * ts
    past = page_table.shape[1] * PAGE_SIZE

    gp = _group(x_prompt, p, tm=min(256, n_p))
    k_p = gp["k"].reshape(bp, sp, N_KV_HEADS, HEAD_DIM)
    v_p = gp["v"].reshape(bp, sp, N_KV_HEADS, HEAD_DIM)
    ik_p = gp["ik"].reshape(bp, sp, D_IDX)
    oa_p = _attend(gp, gp["ikb"].reshape(bp, sp, D_IDX), gp["kb"].reshape(bp, sp, KV_W),
                   gp["vb"].reshape(bp, sp, KV_W), tq=min(128, sp), tk=min(512, sp), q_offset=0,
                   top=min(TOPK_MAX, sp // 4))
    ob_p, wkv_p = _mix(gp, jnp.zeros((bp, RWKV_PROJ_W), F32),
                       jnp.zeros((bp, N_RWKV_HEADS, RWKV_HEAD, RWKV_HEAD), F32), p,
                       tt=min(256, sp), chunk=min(64, sp))

    gs = _group(x_sample, p, tm=min(256, n_s))
    k_s = gs["k"].reshape(bs, ts, N_KV_HEADS, HEAD_DIM)
    v_s = gs["v"].reshape(bs, ts, N_KV_HEADS, HEAD_DIM)
    ik_s = gs["ik"].reshape(bs, ts, D_IDX)
    tq_s = 16
    seq = lambda z: _pad_axis(z.reshape(bs, ts, -1), 1, tq_s)
    new_t = lambda z, *hd: _pad_axis(jnp.moveaxis(z.reshape((bs, ts) + hd), 1, -1), len(hd) + 1, PAGE_SIZE)
    oa_s = _dsa_paged(seq(gs["iq"]), seq(gs["iw"]), seq(gs["q"]),
                      new_t(gs["ikb"], D_IDX), new_t(gs["kb"], N_KV_HEADS, HEAD_DIM),
                      new_t(gs["vb"], N_KV_HEADS, HEAD_DIM),
                      jnp.transpose(cache_idx_k[0], (0, 2, 1)), jnp.transpose(cache_k[0], (0, 2, 3, 1)),
                      jnp.transpose(cache_v[0], (0, 2, 3, 1)), page_table,
                      tq=tq_s, top=min(TOPK_MAX, (past + ts) // 4))[:, :ts].reshape(n_s, ATTN_W)
    ob_s, wkv_s = _mix(gs, state_shift[0], state_wkv[0], p, tt=8, chunk=8)

    h_p, hn_p, idx_p, gate_p = _merge(x_prompt.reshape(n_p, D_MODEL), oa_p, ob_p, gp["ga"], gp["gb"], p,
                                      tm=min(256, n_p))
    h_s, hn_s, idx_s, gate_s = _merge(x_sample.reshape(n_s, D_MODEL), oa_s, ob_s, gs["ga"], gs["gb"], p,
                                      tm=min(256, n_s))
    hn = jnp.concatenate([hn_p, hn_s], axis=0)
    top_idx = jnp.concatenate([idx_p[:, :TOP_K], idx_s[:, :TOP_K]], axis=0)
    gate = jnp.concatenate([gate_p[:, :TOP_K], gate_s[:, :TOP_K]], axis=0)
    f = _moe(hn, top_idx, gate, p, bm=512)
    y_p = (h_p + f[:n_p]).reshape(bp, sp, D_MODEL)
    y_s = (h_s + f[n_p:]).reshape(bs, ts, D_MODEL)

    st = lambda z: z[None]
    return (y_p, y_s, st(k_p), st(v_p), st(ik_p), st(wkv_p), st(gp["pr"][:, -1]),
            st(k_s), st(v_s), st(ik_s), st(wkv_s), st(gs["pr"][:, -1]))
```

```python
import functools

import jax
import jax.numpy as jnp
import numpy as np
from jax import lax
from jax.experimental import pallas as pl
from jax.experimental.pallas import tpu as pltpu

D_MODEL = 1024
PAGE_SIZE = 128
HEAD_DIM = 64
N_Q_HEADS = 8
N_KV_HEADS = 2
Q_PER_KV = N_Q_HEADS // N_KV_HEADS
ATTN_W = N_Q_HEADS * HEAD_DIM
KV_W = N_KV_HEADS * HEAD_DIM
ATTN_SCALE = HEAD_DIM ** -0.5
N_IDX_HEADS = 8
D_IDX = 64
IDX_W = N_IDX_HEADS * D_IDX
IDX_SCALE = (N_IDX_HEADS * D_IDX) ** -0.5
TOPK_MAX = 256
RWKV_HEAD = 64
N_RWKV_HEADS = 8
RWKV_W = N_RWKV_HEADS * RWKV_HEAD
W_LORA = 64
A_LORA = 64
G_LORA = 128
RWKV_PROJ_W = 3 * RWKV_W + W_LORA + A_LORA + G_LORA
LNX_EPS = 64e-5
N_EXPERTS = 32
TOP_K = 4
D_FF = 1024
SWIGLU_LIMIT = 7.0
SWIGLU_ALPHA = 1.702
NORM_EPS = 1e-6

LANES = 128
VMEM_LIMIT = 56 * 1024 * 1024
INT_MIN = -(2 ** 31)
NEG_BIG = -1e30
F32 = jnp.float32
BF16 = jnp.bfloat16
HI = lax.Precision.HIGHEST


def _dot(a, b, precision=None):
    return jnp.dot(a, b, preferred_element_type=F32, precision=precision)


def _dot_nt(a, b, precision=None):
    return lax.dot_general(a, b, (((1,), (1,)), ((), ())), preferred_element_type=F32, precision=precision)


def _dot_tn(a, b, precision=None):
    return lax.dot_general(a, b, (((0,), (0,)), ((), ())), preferred_element_type=F32, precision=precision)


def _group_indicator(width, group):
    r = np.arange(width) // group
    return jnp.asarray((r[:, None] == np.arange(LANES)[None, :]).astype(np.float32))


def _split(x, terms=2):
    parts = []
    for _ in range(terms - 1):
        hi = x.astype(BF16)
        parts.append(hi)
        x = x - hi.astype(F32)
    parts.append(x.astype(BF16))
    return parts


def _mm(a, b, dot=None):
    dot = dot or _dot
    return dot(a[0], b[0]) + (dot(a[0], b[1]) + dot(a[1], b[0]))


def _group_sum(x, ge, get):
    s = sum(_dot(part, ge) for part in _split(x))
    return sum(_dot(part, get) for part in _split(s))


def _const_spec(shape):
    nd = len(shape)
    return pl.BlockSpec(shape, lambda *_: (0,) * nd)


def _inproj_kernel(x_ref, g_ref, wq_ref, wk_ref, wv_ref, wiq_ref, wikw_ref, wpr_ref, wga_ref, wgb_ref,
                   qn_ref, kn_ref, ikn_ref, ge_ref, get_ref, ge2_ref, ge2t_ref,
                   q_out, k_out, kb_out, v_out, vb_out, iq_out, ik_out, ikb_out, iw_out, pr_out, ga_out, gb_out):
    x = x_ref[...]
    ms = jnp.mean(x * x, axis=-1, keepdims=True)
    xn = (x * lax.rsqrt(ms + NORM_EPS) * g_ref[...]).astype(BF16)

    q = _dot(xn, wq_ref[...])
    qs = _group_sum(q * q, ge_ref[...], get_ref[...]) * (1.0 / HEAD_DIM)
    q_out[...] = (q * lax.rsqrt(qs + NORM_EPS) * qn_ref[...]).astype(q_out.dtype)

    k = _dot(xn, wk_ref[...])
    ks = _group_sum(k * k, ge2_ref[...], ge2t_ref[...]) * (1.0 / HEAD_DIM)
    kn = k * lax.rsqrt(ks + NORM_EPS) * kn_ref[...]
    for g in range(N_KV_HEADS):
        k_out[:, g, :] = kn[:, g * HEAD_DIM:(g + 1) * HEAD_DIM]
    kb_out[...] = kn.astype(kb_out.dtype)

    v = _dot(xn, wv_ref[...])
    for g in range(N_KV_HEADS):
        v_out[:, g, :] = v[:, g * HEAD_DIM:(g + 1) * HEAD_DIM]
    vb_out[...] = v.astype(vb_out.dtype)
    iq_out[...] = _dot(xn, wiq_ref[...]).astype(iq_out.dtype)

    ikw = _dot(xn, wikw_ref[...])
    lane = lax.broadcasted_iota(jnp.int32, ikw.shape, 1)
    iks = jnp.sum(jnp.where(lane < D_IDX, ikw * ikw, 0.0), axis=-1, keepdims=True) * (1.0 / D_IDX)
    ikn = ikw[:, :D_IDX] * lax.rsqrt(iks + NORM_EPS) * ikn_ref[...]
    ik_out[...] = ikn
    ikb_out[...] = ikn.astype(ikb_out.dtype)
    iw_out[...] = ikw[:, D_IDX:D_IDX + N_IDX_HEADS]

    pr_out[...] = _dot(xn, wpr_ref[...])
    ga_out[...] = jax.nn.sigmoid(_dot(xn, wga_ref[...])).astype(ga_out.dtype)
    gb_out[...] = jax.nn.sigmoid(_dot(xn, wgb_ref[...])).astype(gb_out.dtype)


def _in_proj(x2d, p, tm):
    n = x2d.shape[0]
    widths = (ATTN_W, KV_W, KV_W, KV_W, KV_W, IDX_W, D_IDX, D_IDX, N_IDX_HEADS, RWKV_PROJ_W, D_MODEL, D_MODEL)
    dtypes = (BF16, F32, BF16, F32, BF16, BF16, F32, BF16, F32, F32, BF16, BF16)
    row = lambda w: pl.BlockSpec((tm, w), lambda i: (i, 0))
    heads_spec = pl.BlockSpec((tm, N_KV_HEADS, HEAD_DIM), lambda i: (i, 0, 0))
    consts = (p["norm_mix"], p["wq"], p["wk"], p["wv"], p["wiq"], p["wikw"], p["wpr"], p["wga"], p["wgb"],
              p["q_norm_t"], p["k_norm_t"], p["ik_norm_t"], p["ge"], p["get"], p["ge2"], p["ge2t"])
    return pl.pallas_call(
        _inproj_kernel,
        grid=(n // tm,),
        in_specs=[row(D_MODEL)] + [_const_spec(c.shape) for c in consts],
        out_specs=[heads_spec if i in (1, 3) else row(w) for i, w in enumerate(widths)],
        out_shape=[jax.ShapeDtypeStruct((n, N_KV_HEADS, HEAD_DIM) if i in (1, 3) else (n, w), d)
                   for i, (w, d) in enumerate(zip(widths, dtypes))],
        compiler_params=pltpu.CompilerParams(dimension_semantics=("parallel",), vmem_limit_bytes=VMEM_LIMIT),
        name="in_proj",
    )(x2d, *consts)


def _stack_heads(x, heads, width):
    return jnp.concatenate([x[:, h * width:(h + 1) * width] for h in heads], axis=0)


def _score_keys(d, iw, tq):
    acc = jnp.zeros((tq, d.shape[1]), F32)
    for h in range(N_IDX_HEADS):
        acc = acc + jnp.maximum(d[h * tq:(h + 1) * tq], 0.0) * iw[:, h:h + 1]
    sc = acc * IDX_SCALE
    sc = jnp.where(sc == 0.0, 0.0, sc)
    bits = pltpu.bitcast(sc, jnp.int32)
    return bits ^ ((bits >> 31) & 0x7FFFFFFF)


def _select_threshold(keys_ref, n_kt, *, tq, tk, top):
    lane_pos = lax.broadcasted_iota(jnp.int32, (tq, tk), 1)

    def count(pred):
        def body(kt, c):
            start = pl.multiple_of(kt * tk, tk)
            m = pred(keys_ref[:, pl.ds(start, tk)], start).astype(jnp.int32)
            part = m[:, 0:LANES]
            for j in range(1, tk // LANES):
                part = part + m[:, j * LANES:(j + 1) * LANES]
            return c + part
        c = lax.fori_loop(0, n_kt, body, jnp.zeros((tq, LANES), jnp.int32))
        return jnp.sum(c, axis=-1, keepdims=True)

    c0 = count(lambda k, s: k >= 0)
    t = jnp.where(c0 >= top, 0, INT_MIN).astype(jnp.int32)

    def bit_step(i, t):
        cand = t | jnp.left_shift(jnp.int32(1), 30 - i)
        c = count(lambda k, s: k >= cand)
        return jnp.where(c >= top, cand, t)

    t = lax.fori_loop(0, 31, bit_step, t)
    t = jnp.maximum(t, INT_MIN + 1)

    excess = count(lambda k, s: k >= t) > top

    @pl.when(jnp.max(excess.astype(jnp.int32)) > 0)
    def _():
        keep = top - count(lambda k, s: k > t)

        def idx_step(i, lim):
            cand = lim | jnp.left_shift(jnp.int32(1), 14 - i)
            c = count(lambda k, s: ((k == t) & (s + lane_pos < cand)))
            return jnp.where(c <= keep, cand, lim)

        lim = lax.fori_loop(0, 15, idx_step, jnp.zeros((tq, 1), jnp.int32))

        def demote(kt, carry):
            start = pl.multiple_of(kt * tk, tk)
            k = keys_ref[:, pl.ds(start, tk)]
            drop = (k == t) & (start + lane_pos >= lim) & excess
            keys_ref[:, pl.ds(start, tk)] = jnp.where(drop, t - 1, k)
            return carry

        lax.fori_loop(0, n_kt, demote, 0)

    return t


def _softmax_step(state, s, bias, pv, tq):
    m_i, l_i, acc = state
    gq, tk = s.shape
    s = (s.reshape(Q_PER_KV, tq, tk) + bias[None]).reshape(gq, tk)
    m_n = jnp.maximum(m_i, jnp.max(s, axis=-1, keepdims=True))
    alpha = jnp.exp(m_i - m_n)
    pm = jnp.exp(s - m_n)
    l_n = alpha * l_i + jnp.sum(pm, axis=-1, keepdims=True)
    return m_n, l_n, alpha * acc + pv(pm.astype(BF16))


def _softmax_init(tq):
    gq = Q_PER_KV * tq
    return tuple((jnp.full((gq, 1), NEG_BIG, F32), jnp.zeros((gq, 1), F32), jnp.zeros((gq, HEAD_DIM), F32))
                 for _ in range(N_KV_HEADS))


def _write_heads(o_ref, res, tq):
    for g in range(N_KV_HEADS):
        _, l_i, acc = res[g]
        og = acc / l_i
        for j in range(Q_PER_KV):
            h = g * Q_PER_KV + j
            o_ref[0, :, h * HEAD_DIM:(h + 1) * HEAD_DIM] = og[j * tq:(j + 1) * tq].astype(o_ref.dtype)


def _dsa_kernel(iq_ref, iw_ref, q_ref, ik_ref, k_ref, v_ref, o_ref, keys_ref, *, tq, tk, q_offset, n_keys, top):
    qi = pl.program_id(1)
    q_base = q_offset + qi * tq
    n_kt = jnp.minimum((q_base + tq + tk - 1) // tk, n_keys // tk)
    q_pos = q_base + lax.broadcasted_iota(jnp.int32, (tq, 1), 0)
    lane_pos = lax.broadcasted_iota(jnp.int32, (tq, tk), 1)

    iq = _stack_heads(iq_ref[0], range(N_IDX_HEADS), D_IDX)
    iw = iw_ref[0]

    def score_tile(kt, carry):
        start = pl.multiple_of(kt * tk, tk)
        key = _score_keys(_dot_nt(iq, ik_ref[0, pl.ds(start, tk), :]), iw, tq)
        keys_ref[:, pl.ds(start, tk)] = jnp.where(start + lane_pos <= q_pos, key, INT_MIN)
        return carry

    lax.fori_loop(0, n_kt, score_tile, 0)

    t = _select_threshold(keys_ref, n_kt, tq=tq, tk=tk, top=top)

    q_all = q_ref[0] * ATTN_SCALE
    qs = [_stack_heads(q_all, range(g * Q_PER_KV, (g + 1) * Q_PER_KV), HEAD_DIM) for g in range(N_KV_HEADS)]

    def attn_tile(kt, carry):
        start = pl.multiple_of(kt * tk, tk)
        bias = jnp.where(keys_ref[:, pl.ds(start, tk)] >= t, 0.0, NEG_BIG)
        k_t = k_ref[0, pl.ds(start, tk), :]
        v_t = v_ref[0, pl.ds(start, tk), :]
        new = []
        for g in range(N_KV_HEADS):
            hs = slice(g * HEAD_DIM, (g + 1) * HEAD_DIM)
            new.append(_softmax_step(carry[g], _dot_nt(qs[g], k_t[:, hs]), bias,
                                     lambda p, hs=hs: _dot(p, v_t[:, hs]), tq))
        return tuple(new)

    _write_heads(o_ref, lax.fori_loop(0, n_kt, attn_tile, _softmax_init(tq)), tq)


def _dsa(iq, iw, q, ik, k, v, *, tq, tk, q_offset, top):
    b, sq, _ = iq.shape
    n_keys = ik.shape[1]
    assert sq % tq == 0 and n_keys % tk == 0 and tk % LANES == 0
    kern = functools.partial(_dsa_kernel, tq=tq, tk=tk, q_offset=q_offset, n_keys=n_keys, top=top)
    return pl.pallas_call(
        kern,
        grid=(b, sq // tq),
        in_specs=[
            pl.BlockSpec((1, tq, IDX_W), lambda bi, qi: (bi, qi, 0)),
            pl.BlockSpec((1, tq, N_IDX_HEADS), lambda bi, qi: (bi, qi, 0)),
            pl.BlockSpec((1, tq, ATTN_W), lambda bi, qi: (bi, qi, 0)),
            pl.BlockSpec((1, n_keys, D_IDX), lambda bi, qi: (bi, 0, 0)),
            pl.BlockSpec((1, n_keys, KV_W), lambda bi, qi: (bi, 0, 0)),
            pl.BlockSpec((1, n_keys, KV_W), lambda bi, qi: (bi, 0, 0)),
        ],
        out_specs=pl.BlockSpec((1, tq, ATTN_W), lambda bi, qi: (bi, qi, 0)),
        out_shape=jax.ShapeDtypeStruct((b, sq, ATTN_W), BF16),
        scratch_shapes=[pltpu.VMEM((tq, n_keys), jnp.int32)],
        compiler_params=pltpu.CompilerParams(dimension_semantics=("parallel", "arbitrary"),
                                             vmem_limit_bytes=VMEM_LIMIT),
        name="dsa",
    )(iq, iw, q, ik, k, v)


def _dsa_paged_kernel(pt_ref, iq_ref, iw_ref, q_ref, ikn_ref, kn_ref, vn_ref, cik_ref, ck_ref, cv_ref, o_ref,
                      ik_buf, k_buf, v_buf, sems, keys_ref, *, tq, n_pages, ppt, tk_sel, top):
    b = pl.program_id(0)
    slot = b % 2
    past = n_pages * PAGE_SIZE
    tk = ppt * PAGE_SIZE

    def page_copies(bi, sl, p):
        page = pt_ref[bi * n_pages + p]
        return (pltpu.make_async_copy(cik_ref.at[page], ik_buf.at[sl, p], sems.at[sl, 0]),
                pltpu.make_async_copy(ck_ref.at[page], k_buf.at[sl, p], sems.at[sl, 1]),
                pltpu.make_async_copy(cv_ref.at[page], v_buf.at[sl, p], sems.at[sl, 2]))

    def fetch(bi, sl):
        def body(p, carry):
            for cp in page_copies(bi, sl, p):
                cp.start()
            return carry
        lax.fori_loop(0, n_pages, body, 0)

    @pl.when(b == 0)
    def _():
        fetch(0, 0)

    @pl.when(b + 1 < pl.num_programs(0))
    def _():
        fetch(b + 1, 1 - slot)

    def wait_page(p, carry):
        for cp in page_copies(b, slot, p):
            cp.wait()
        return carry

    lax.fori_loop(0, n_pages, wait_page, 0)

    def page_cols(buf, i, idx=()):
        return jnp.concatenate([buf[(slot, i * ppt + j) + idx] for j in range(ppt)], axis=1).astype(BF16)

    iq = _stack_heads(iq_ref[0], range(N_IDX_HEADS), D_IDX)
    iw = iw_ref[0]
    row = lax.broadcasted_iota(jnp.int32, (tq, PAGE_SIZE), 0)
    lane = lax.broadcasted_iota(jnp.int32, (tq, PAGE_SIZE), 1)

    def score_tile(i, carry):
        start = pl.multiple_of(i * tk, tk)
        keys_ref[:, pl.ds(start, tk)] = _score_keys(_dot(iq, page_cols(ik_buf, i)), iw, tq)
        return carry

    lax.fori_loop(0, n_pages // ppt, score_tile, 0)
    new_keys = _score_keys(_dot(iq, ikn_ref[0]), iw, tq)
    keys_ref[:, past:past + PAGE_SIZE] = jnp.where(lane <= row, new_keys, INT_MIN)

    t = _select_threshold(keys_ref, (past + PAGE_SIZE) // tk_sel, tq=tq, tk=tk_sel, top=top)

    q_all = q_ref[0] * ATTN_SCALE
    qs = [_stack_heads(q_all, range(g * Q_PER_KV, (g + 1) * Q_PER_KV), HEAD_DIM) for g in range(N_KV_HEADS)]

    def attn_tile(i, carry):
        start = pl.multiple_of(i * tk, tk)
        bias = jnp.where(keys_ref[:, pl.ds(start, tk)] >= t, 0.0, NEG_BIG)
        new = []
        for g in range(N_KV_HEADS):
            v_t = page_cols(v_buf, i, (g,))
            new.append(_softmax_step(carry[g], _dot(qs[g], page_cols(k_buf, i, (g,))), bias,
                                     lambda p, v_t=v_t: _dot_nt(p, v_t), tq))
        return tuple(new)

    res = lax.fori_loop(0, n_pages // ppt, attn_tile, _softmax_init(tq))
    bias = jnp.where(keys_ref[:, past:past + PAGE_SIZE] >= t, 0.0, NEG_BIG)
    res = tuple(_softmax_step(res[g], _dot(qs[g], kn_ref[0, g]), bias,
                              lambda p, g=g: _dot_nt(p, vn_ref[0, g]), tq) for g in range(N_KV_HEADS))
    _write_heads(o_ref, res, tq)


def _dsa_paged(iq, iw, q, ikn, kn, vn, cik_t, ck_t, cv_t, page_table, *, tq, top):
    b = iq.shape[0]
    n_pages = page_table.shape[1]
    ppt = next(d for d in (4, 2, 1) if n_pages % d == 0)
    n_lane_tiles = n_pages + 1
    tk_sel = LANES * next(d for d in range(8, 0, -1) if n_lane_tiles % d == 0)
    kern = functools.partial(_dsa_paged_kernel, tq=tq, n_pages=n_pages, ppt=ppt, tk_sel=tk_sel, top=top)
    per_seq = lambda *blk: pl.BlockSpec((1,) + blk, lambda bi, pt: (bi,) + (0,) * len(blk))
    grid_spec = pltpu.PrefetchScalarGridSpec(
        num_scalar_prefetch=1,
        grid=(b,),
        in_specs=[per_seq(tq, IDX_W), per_seq(tq, N_IDX_HEADS), per_seq(tq, ATTN_W),
                  per_seq(D_IDX, PAGE_SIZE), per_seq(N_KV_HEADS, HEAD_DIM, PAGE_SIZE),
                  per_seq(N_KV_HEADS, HEAD_DIM, PAGE_SIZE),
                  pl.BlockSpec(memory_space=pl.ANY), pl.BlockSpec(memory_space=pl.ANY),
                  pl.BlockSpec(memory_space=pl.ANY)],
        out_specs=per_seq(tq, ATTN_W),
        scratch_shapes=[pltpu.VMEM((2, n_pages, D_IDX, PAGE_SIZE), F32),
                        pltpu.VMEM((2, n_pages, N_KV_HEADS, HEAD_DIM, PAGE_SIZE), F32),
                        pltpu.VMEM((2, n_pages, N_KV_HEADS, HEAD_DIM, PAGE_SIZE), F32),
                        pltpu.SemaphoreType.DMA((2, 3)),
                        pltpu.VMEM((tq, (n_pages + 1) * PAGE_SIZE), jnp.int32)],
    )
    return pl.pallas_call(
        kern,
        grid_spec=grid_spec,
        out_shape=jax.ShapeDtypeStruct((b, tq, ATTN_W), BF16),
        compiler_params=pltpu.CompilerParams(dimension_semantics=("arbitrary",), vmem_limit_bytes=VMEM_LIMIT),
        name="dsa_paged",
    )(page_table.reshape(-1), iq, iw, q, ikn, kn, vn, cik_t, ck_t, cv_t)


def _rwkv_kernel(pr_ref, sh0_ref, z0_ref, mu_ref, w0_ref, wup_ref, a0_ref, aup_ref, gup_ref, kk_ref, ka_ref,
                 rk_ref, lnw_ref, lnb_ref, ge_ref, get_ref,
                 ob_ref, zout_ref,
                 z_scr, prev_scr, r_s, k_s, v_s, a_s, b_s, lw_s, y_s, *, tt, chunk, t_valid):
    ti = pl.program_id(1)
    n_t = pl.num_programs(1)
    nh, hd = N_RWKV_HEADS, RWKV_HEAD

    @pl.when(ti == 0)
    def _():
        z_scr[...] = z0_ref[0]
        prev_scr[...] = sh0_ref[0]

    gsum = lambda z: _group_sum(z, ge_ref[...], get_ref[...])

    pr = pr_ref[0]
    row = lax.broadcasted_iota(jnp.int32, (tt, 1), 0)
    prev = jnp.where(row == 0, prev_scr[...], pltpu.roll(pr, 1, 0))
    prev_scr[...] = pr[tt - 1:tt]
    m = pr + mu_ref[...] * (prev - pr)
    r = m[:, 0:RWKV_W]
    k = m[:, RWKV_W:2 * RWKV_W]
    v = m[:, 2 * RWKV_W:3 * RWKV_W]
    o = 3 * RWKV_W
    wd = m[:, o:o + W_LORA]
    ad = m[:, o + W_LORA:o + W_LORA + A_LORA]
    gd = m[:, o + W_LORA + A_LORA:]

    lora = lambda z, w_ref: _mm(_split(z), _split(w_ref[...]))
    u = -(w0_ref[...] + lora(jnp.tanh(wd), wup_ref))
    softplus = jnp.maximum(u, 0.0) + jnp.log(1.0 + jnp.exp(-jnp.abs(u)))
    lw = -jnp.exp(-softplus - 0.5)
    a = jax.nn.sigmoid(a0_ref[...] + lora(ad, aup_ref))
    gate = lora(jax.nn.sigmoid(gd), gup_ref)
    kk = k * kk_ref[...]
    kk = kk / jnp.maximum(jnp.sqrt(gsum(kk * kk)), 1e-12)
    k2 = k * (1.0 + (a - 1.0) * ka_ref[...])
    bonus = gsum(r * k2 * rk_ref[...])
    av = -kk
    bv = kk * a
    if t_valid < tt:
        ok = (row < t_valid).astype(F32)
        k2, v, av, bv, lw = k2 * ok, v * ok, av * ok, bv * ok, lw * ok
    r_s[...] = r
    k_s[...] = k2
    v_s[...] = v
    a_s[...] = av
    b_s[...] = bv
    lw_s[...] = lw

    c = chunk
    ri = lax.broadcasted_iota(jnp.int32, (c, c), 0)
    ci = lax.broadcasted_iota(jnp.int32, (c, c), 1)
    tri_incl = (ci <= ri)
    tri_strict = (ci < ri)
    ltri = tri_incl.astype(BF16)
    eye_h = (lax.broadcasted_iota(jnp.int32, (hd, hd), 0) == lax.broadcasted_iota(jnp.int32, (hd, hd), 1))
    n_double = max(int(np.ceil(np.log2(c))), 1)
    heads = range(nh)
    hsl = [slice(h * hd, (h + 1) * hd) for h in heads]

    def chunk_body(ci_, carry):
        s0 = pl.multiple_of(ci_ * c, c)
        sl = pl.ds(s0, c)
        lwc = lw_s[sl, :]
        cum = sum(_dot(ltri, part) for part in _split(lwc, 3))
        cum_end = cum[c - 1:c, :]
        g_end = jnp.exp(cum_end)
        at = a_s[sl, :] * jnp.exp(cum - lwc)
        rt = r_s[sl, :] * jnp.exp(cum)
        g_inv = jnp.exp(-cum)
        g_tail = jnp.exp(cum_end - cum)
        bt = b_s[sl, :] * g_inv
        kt = k_s[sl, :] * g_inv
        bc = b_s[sl, :] * g_tail
        kc = k_s[sl, :] * g_tail
        vc = v_s[sl, :]

        left = [_split(jnp.concatenate([at[:, s], rt[:, s]], axis=0)) for s in hsl]
        right = [_split(jnp.concatenate([bt[:, s], kt[:, s]], axis=0)) for s in hsl]
        amat = [_mm(left[h], right[h], _dot_nt) for h in heads]
        a_ab = [jnp.where(tri_strict, amat[h][:c, :c], 0.0) for h in heads]
        a_ak = [_split(jnp.where(tri_strict, amat[h][:c, c:], 0.0)) for h in heads]
        a_rb = [_split(jnp.where(tri_incl, amat[h][c:, :c], 0.0)) for h in heads]
        a_rk = [_split(jnp.where(tri_incl, amat[h][c:, c:], 0.0)) for h in heads]
        vh = [_split(vc[:, s]) for s in hsl]
        akv = [_mm(a_ak[h], vh[h]) for h in heads]
        uu = [jnp.concatenate([at[:, hsl[h]], akv[h]], axis=1) for h in heads]
        pw = a_ab
        for step in range(n_double):
            pws = [_split(z) for z in pw]
            uus = [_split(z) for z in uu]
            uu = [uu[h] + _mm(pws[h], uus[h]) for h in heads]
            if step + 1 < n_double:
                pw = [_mm(pws[h], pws[h]) for h in heads]
        uus = [_split(z) for z in uu]
        x1 = [_mm(a_rb[h], uus[h]) for h in heads]
        x2 = [_mm(a_rk[h], vh[h]) for h in heads]
        mn = [_mm(_split(bc[:, hsl[h]]), uus[h], _dot_tn) for h in heads]
        nk = [_mm(_split(kc[:, hsl[h]]), vh[h], _dot_tn) for h in heads]
        pm = []
        for h in heads:
            p2 = rt[:, hsl[h]] + x1[h][:, :hd]
            mh = jnp.where(eye_h, g_end[:, hsl[h]], 0.0) + mn[h][:, :hd]
            pm.append(_split(jnp.concatenate([p2, mh], axis=0)))
        res = [_mm(pm[h], _split(z_scr[h])) for h in heads]
        for h in heads:
            y_s[sl, hsl[h]] = res[h][:c] + x1[h][:, hd:] + x2[h]
            z_scr[h] = res[h][c:] + mn[h][:, hd:] + nk[h]
        return carry

    lax.fori_loop(0, tt // c, chunk_body, 0)

    y = y_s[...]
    mean = gsum(y) * (1.0 / hd)
    dlt = y - mean
    var = gsum(dlt * dlt) * (1.0 / hd)
    yn = dlt * lax.rsqrt(var + LNX_EPS) * lnw_ref[...] + lnb_ref[...]
    ob_ref[0] = ((yn + bonus * v_s[...]) * gate).astype(ob_ref.dtype)

    @pl.when(ti == n_t - 1)
    def _():
        zout_ref[0] = z_scr[...]


def _rwkv(pr, shift0, z0, p, *, tt, chunk, t_valid):
    b, t, _ = pr.shape
    assert t % tt == 0 and tt % chunk == 0
    consts = (p["shift_mu"], p["w0"], p["w_lora_up"], p["a0"], p["a_lora_up"], p["g_lora_up"], p["k_k"], p["k_a"],
              p["r_k"], p["ln_x_w"], p["ln_x_b"], p["ge"], p["get"])
    kern = functools.partial(_rwkv_kernel, tt=tt, chunk=chunk, t_valid=t_valid)
    wide = lambda: pltpu.VMEM((tt, RWKV_W), F32)
    return pl.pallas_call(
        kern,
        grid=(b, t // tt),
        in_specs=[
            pl.BlockSpec((1, tt, RWKV_PROJ_W), lambda bi, ti: (bi, ti, 0)),
            pl.BlockSpec((1, 1, RWKV_PROJ_W), lambda bi, ti: (bi, 0, 0)),
            pl.BlockSpec((1, N_RWKV_HEADS, RWKV_HEAD, RWKV_HEAD), lambda bi, ti: (bi, 0, 0, 0)),
        ] + [_const_spec(c.shape) for c in consts],
        out_specs=[
            pl.BlockSpec((1, tt, RWKV_W), lambda bi, ti: (bi, ti, 0)),
            pl.BlockSpec((1, N_RWKV_HEADS, RWKV_HEAD, RWKV_HEAD), lambda bi, ti: (bi, 0, 0, 0)),
        ],
        out_shape=[jax.ShapeDtypeStruct((b, t, RWKV_W), BF16),
                   jax.ShapeDtypeStruct((b, N_RWKV_HEADS, RWKV_HEAD, RWKV_HEAD), F32)],
        scratch_shapes=[pltpu.VMEM((N_RWKV_HEADS, RWKV_HEAD, RWKV_HEAD), F32),
                        pltpu.VMEM((1, RWKV_PROJ_W), F32),
                        wide(), wide(), wide(), wide(), wide(), wide(), wide()],
        compiler_params=pltpu.CompilerParams(dimension_semantics=("parallel", "arbitrary"),
                                             vmem_limit_bytes=VMEM_LIMIT),
        name="rwkv",
    )(pr, shift0, z0, *consts)


def _merge_kernel(x_ref, oa_ref, ob_ref, ga_ref, gb_ref, wa_ref, wb_ref, wo_ref, nf_ref, rwt_ref, rb_ref,
                  h_out, hn_out, idx_out, gate_out):
    ma = _dot(oa_ref[...], wa_ref[...])
    mb = _dot(ob_ref[...], wb_ref[...])
    mm = ga_ref[...].astype(F32) * ma + gb_ref[...].astype(F32) * mb
    h = x_ref[...] + _dot(mm.astype(BF16), wo_ref[...])
    h_out[...] = h
    ms = jnp.mean(h * h, axis=-1, keepdims=True)
    hn = h * lax.rsqrt(ms + NORM_EPS) * nf_ref[...]
    hn_out[...] = hn.astype(hn_out.dtype)
    logits = _mm(_split(rwt_ref[...]), _split(hn), _dot_nt) + rb_ref[...]
    tm = logits.shape[1]
    expert = lax.broadcasted_iota(jnp.int32, logits.shape, 0)
    vals, idxs = [], []
    for _ in range(TOP_K):
        mx = jnp.max(logits, axis=0, keepdims=True)
        ix = jnp.min(jnp.where(logits == mx, expert, N_EXPERTS), axis=0, keepdims=True)
        vals.append(mx)
        idxs.append(ix)
        logits = jnp.where(expert == ix, -jnp.inf, logits)
    es = [jnp.exp(v - vals[0]) for v in vals]
    den = es[0] + es[1] + es[2] + es[3]
    pad = 8 - TOP_K
    idx_out[...] = jnp.concatenate(idxs + [jnp.zeros((pad, tm), jnp.int32)], axis=0)
    gate_out[...] = jnp.concatenate([e / den for e in es] + [jnp.zeros((pad, tm), F32)], axis=0)


def _merge(x2d, oa, ob, ga, gb, p, tm):
    n = x2d.shape[0]
    row = lambda w: pl.BlockSpec((tm, w), lambda i: (i, 0))
    consts = (p["w_proj_a"], p["w_proj_b"], p["w_out"], p["norm_ffn"], p["router_w"], p["router_b"])
    return pl.pallas_call(
        _merge_kernel,
        grid=(n // tm,),
        in_specs=[row(D_MODEL), row(ATTN_W), row(RWKV_W), row(D_MODEL), row(D_MODEL)]
        + [_const_spec(c.shape) for c in consts],
        out_specs=[row(D_MODEL), row(D_MODEL)] + [pl.BlockSpec((8, tm), lambda i: (0, i))] * 2,
        out_shape=[jax.ShapeDtypeStruct((n, D_MODEL), F32), jax.ShapeDtypeStruct((n, D_MODEL), BF16),
                   jax.ShapeDtypeStruct((8, n), jnp.int32), jax.ShapeDtypeStruct((8, n), F32)],
        compiler_params=pltpu.CompilerParams(dimension_semantics=("parallel",), vmem_limit_bytes=VMEM_LIMIT),
        name="merge",
    )(x2d, oa, ob, ga, gb, *consts)


def _moe_kernel(be_ref, nb_ref, x_ref, wgu_ref, bgu_ref, wd_ref, bd_ref, o_ref, wgu_s, wd_s):
    i = pl.program_id(0)
    used = i < nb_ref[0]

    @pl.when(used & ((i == 0) | (be_ref[i] != be_ref[jnp.maximum(i - 1, 0)])))
    def _():
        wgu_s[...] = wgu_ref[0].astype(wgu_s.dtype)
        wd_s[...] = wd_ref[0].astype(wd_s.dtype)

    @pl.when(used)
    def _():
        hcat = _dot(x_ref[...], wgu_s[...]) + bgu_ref[0]
        glu = jnp.minimum(hcat[:, :D_FF], SWIGLU_LIMIT)
        lin = jnp.clip(hcat[:, D_FF:], -SWIGLU_LIMIT, SWIGLU_LIMIT)
        act = glu * jax.nn.sigmoid(SWIGLU_ALPHA * glu) * (lin + 1.0)
        o_ref[...] = (_dot(act.astype(BF16), wd_s[...]) + bd_ref[0]).astype(o_ref.dtype)

    @pl.when(jnp.logical_not(used))
    def _():
        o_ref[...] = jnp.zeros(o_ref.shape, o_ref.dtype)


def _moe_rows(xg, block_exp, n_used, p, bm):
    rows = xg.shape[0]
    grid_spec = pltpu.PrefetchScalarGridSpec(
        num_scalar_prefetch=2,
        grid=(rows // bm,),
        in_specs=[
            pl.BlockSpec((bm, D_MODEL), lambda i, be, nb: (i, 0)),
            pl.BlockSpec((1, D_MODEL, 2 * D_FF), lambda i, be, nb: (be[i], 0, 0)),
            pl.BlockSpec((1, 1, 2 * D_FF), lambda i, be, nb: (be[i], 0, 0)),
            pl.BlockSpec((1, D_FF, D_MODEL), lambda i, be, nb: (be[i], 0, 0)),
            pl.BlockSpec((1, 1, D_MODEL), lambda i, be, nb: (be[i], 0, 0)),
        ],
        out_specs=pl.BlockSpec((bm, D_MODEL), lambda i, be, nb: (i, 0)),
        scratch_shapes=[pltpu.VMEM((D_MODEL, 2 * D_FF), BF16), pltpu.VMEM((D_FF, D_MODEL), BF16)],
    )
    return pl.pallas_call(
        _moe_kernel,
        grid_spec=grid_spec,
        out_shape=jax.ShapeDtypeStruct((rows, D_MODEL), BF16),
        compiler_params=pltpu.CompilerParams(dimension_semantics=("arbitrary",), vmem_limit_bytes=VMEM_LIMIT),
        name="moe",
    )(block_exp, n_used, xg, p["w_gate_up"], p["b_gate_up"], p["w_down"], p["b_down"])


def _moe(hn, top_idx, gate, p, bm):
    n = hn.shape[0]
    nk = n * TOP_K
    e_flat = top_idx.reshape(-1)
    onehot = (e_flat[:, None] == jnp.arange(N_EXPERTS, dtype=jnp.int32)[None, :]).astype(jnp.int32)
    csum = jnp.cumsum(onehot, axis=0)
    counts = csum[-1]
    rank = jnp.take_along_axis(csum, e_flat[:, None], axis=1)[:, 0] - 1
    padded = (counts + bm - 1) // bm * bm
    pends = jnp.cumsum(padded)
    pstarts = pends - padded
    dest = pstarts[e_flat] + rank
    n_blocks = -(-nk // bm) + N_EXPERTS
    rows = n_blocks * bm
    tok_flat = jnp.arange(nk, dtype=jnp.int32) % n
    row_tok = jnp.zeros((rows,), jnp.int32).at[dest].set(tok_flat, unique_indices=True)
    block_start = jnp.arange(n_blocks, dtype=jnp.int32) * bm
    block_exp = jnp.minimum(jnp.sum((pends[None, :] <= block_start[:, None]).astype(jnp.int32), axis=1),
                            N_EXPERTS - 1)
    n_used = (pends[-1:] // bm).astype(jnp.int32)
    out = _moe_rows(hn[row_tok], block_exp, n_used, p, bm)
    return (out[dest.reshape(TOP_K, n)].astype(F32) * gate[:, :, None]).sum(axis=0)


def _prep_params(norm_mix, w_in, q_norm, k_norm, idx_k_norm, shift_mu, w0, w_lora_up, a0, a_lora_up, g_lora_up, k_k,
                 k_a, r_k, ln_x_w, ln_x_b, w_proj_a, w_proj_b, w_out, norm_ffn, router_w, router_b, w_gate_up,
                 b_gate_up, w_down, b_down):
    splits = (ATTN_W, KV_W, KV_W, IDX_W, D_IDX, N_IDX_HEADS, RWKV_PROJ_W, D_MODEL, D_MODEL)
    cuts = np.cumsum(splits)[:-1].tolist()
    wq, wk, wv, wiq, wik, wiw, wpr, wga, wgb = jnp.split(w_in.astype(BF16), cuts, axis=-1)
    wikw = jnp.concatenate([wik, wiw, jnp.zeros((D_MODEL, LANES - D_IDX - N_IDX_HEADS), BF16)], axis=-1)
    row = lambda z: z.reshape(1, -1).astype(F32)
    ge = _group_indicator(RWKV_W, RWKV_HEAD).astype(BF16)
    ge2 = _group_indicator(KV_W, HEAD_DIM).astype(BF16)
    return dict(
        norm_mix=row(norm_mix), wq=wq, wk=wk, wv=wv, wiq=wiq, wikw=wikw, wpr=wpr, wga=wga, wgb=wgb,
        q_norm_t=row(jnp.tile(q_norm, N_Q_HEADS)), k_norm_t=row(jnp.tile(k_norm, N_KV_HEADS)),
        ik_norm_t=row(idx_k_norm),
        ge=ge, get=ge.T, ge2=ge2, ge2t=ge2.T,
        shift_mu=row(shift_mu), w0=row(w0), w_lora_up=w_lora_up, a0=row(a0), a_lora_up=a_lora_up,
        g_lora_up=g_lora_up, k_k=row(k_k), k_a=row(k_a), r_k=row(r_k), ln_x_w=row(ln_x_w), ln_x_b=row(ln_x_b),
        w_proj_a=w_proj_a.astype(BF16), w_proj_b=w_proj_b.astype(BF16), w_out=w_out.astype(BF16),
        norm_ffn=row(norm_ffn), router_w=router_w.T, router_b=router_b.reshape(-1, 1),
        w_gate_up=w_gate_up, b_gate_up=b_gate_up[:, None, :], w_down=w_down, b_down=b_down[:, None, :],
    )


def _pad_axis(z, axis, size):
    if z.shape[axis] == size:
        return z
    pad = [(0, 0)] * z.ndim
    pad[axis] = (0, size - z.shape[axis])
    return jnp.pad(z, pad)


def _group(x, p, *, tm):
    b, t, _ = x.shape
    names = ("q", "k", "kb", "v", "vb", "iq", "ik", "ikb", "iw", "pr", "ga", "gb")
    g = dict(zip(names, _in_proj(x.reshape(b * t, D_MODEL), p, tm)), b=b, t=t)
    g["pr"] = g["pr"].reshape(b, t, RWKV_PROJ_W)
    return g


def _attend(g, ik_all, k_all, v_all, *, tq, tk, q_offset, top):
    b, t = g["b"], g["t"]
    tp = -(-t // tq) * tq
    n_keys = -(-ik_all.shape[1] // tk) * tk
    seq = lambda z: _pad_axis(z.reshape(b, t, -1), 1, tp)
    keys = lambda z: _pad_axis(z, 1, n_keys)
    oa = _dsa(seq(g["iq"]), seq(g["iw"]), seq(g["q"]), keys(ik_all), keys(k_all), keys(v_all),
              tq=tq, tk=tk, q_offset=q_offset, top=top)
    return oa[:, :t].reshape(b * t, ATTN_W)


def _mix(g, shift0, wkv0, p, *, tt, chunk):
    b, t = g["b"], g["t"]
    tp = -(-t // tt) * tt
    ob, z = _rwkv(_pad_axis(g["pr"], 1, tp), shift0[:, None, :], jnp.swapaxes(wkv0, -1, -2), p,
                  tt=tt, chunk=chunk, t_valid=min(t, tt) if tp != t else tt)
    return ob[:, :t].reshape(b * t, RWKV_W), jnp.swapaxes(z, -1, -2)


def kernel(x_prompt, x_sample, cache_k, cache_v, cache_idx_k, page_table, state_wkv, state_shift, norm_mix, w_in, q_norm, k_norm, idx_k_norm, shift_mu, w0, w_lora_up, a0, a_lora_up, g_lora_up, k_k, k_a, r_k, ln_x_w, ln_x_b, w_proj_a, w_proj_b, w_out, norm_ffn, router_w, router_b, w_gate_up, b_gate_up, w_down, b_down):
    depth = norm_mix.shape[0]
    assert depth == 1
    params = (norm_mix, w_in, q_norm, k_norm, idx_k_norm, shift_mu, w0, w_lora_up, a0, a_lora_up, g_lora_up, k_k, k_a,
              r_k, ln_x_w, ln_x_b, w_proj_a, w_proj_b, w_out, norm_ffn, router_w, router_b, w_gate_up, b_gate_up,
              w_down, b_down)
    p = _prep_params(*[z[0] for z in params])
    bp, sp, _ = x_prompt.shape
    bs, ts, _ = x_sample.shape
    n_p, n_s = bp * sp, bs * ts
    past = page_table.shape[1] * PAGE_SIZE

    gp = _group(x_prompt, p, tm=min(256, n_p))
    k_p = gp["k"].reshape(bp, sp, N_KV_HEADS, HEAD_DIM)
    v_p = gp["v"].reshape(bp, sp, N_KV_HEADS, HEAD_DIM)
    ik_p = gp["ik"].reshape(bp, sp, D_IDX)
    oa_p = _attend(gp, gp["ikb"].reshape(bp, sp, D_IDX), gp["kb"].reshape(bp, sp, KV_W),
                   gp["vb"].reshape(bp, sp, KV_W), tq=min(128, sp), tk=min(512, sp), q_offset=0,
                   top=min(TOPK_MAX, sp // 4))
    ob_p, wkv_p = _mix(gp, jnp.zeros((bp, RWKV_PROJ_W), F32),
                       jnp.zeros((bp, N_RWKV_HEADS, RWKV_HEAD, RWKV_HEAD), F32), p,
                       tt=min(256, sp), chunk=min(64, sp))

    gs = _group(x_sample, p, tm=min(256, n_s))
    k_s = gs["k"].reshape(bs, ts, N_KV_HEADS, HEAD_DIM)
    v_s = gs["v"].reshape(bs, ts, N_KV_HEADS, HEAD_DIM)
    ik_s = gs["ik"].reshape(bs, ts, D_IDX)
    tq_s = 16
    seq = lambda z: _pad_axis(z.reshape(bs, ts, -1), 1, tq_s)
    new_t = lambda z, *hd: _pad_axis(jnp.moveaxis(z.reshape((bs, ts) + hd), 1, -1), len(hd) + 1, PAGE_SIZE)
    oa_s = _dsa_paged(seq(gs["iq"]), seq(gs["iw"]), seq(gs["q"]),
                      new_t(gs["ikb"], D_IDX), new_t(gs["kb"], N_KV_HEADS, HEAD_DIM),
                      new_t(gs["vb"], N_KV_HEADS, HEAD_DIM),
                      jnp.transpose(cache_idx_k[0], (0, 2, 1)), jnp.transpose(cache_k[0], (0, 2, 3, 1)),
                      jnp.transpose(cache_v[0], (0, 2, 3, 1)), page_table,
                      tq=tq_s, top=min(TOPK_MAX, (past + ts) // 4))[:, :ts].reshape(n_s, ATTN_W)
    ob_s, wkv_s = _mix(gs, state_shift[0], state_wkv[0], p, tt=8, chunk=8)

    h_p, hn_p, idx_p, gate_p = _merge(x_prompt.reshape(n_p, D_MODEL), oa_p, ob_p, gp["ga"], gp["gb"], p,
                                      tm=min(256, n_p))
    h_s, hn_s, idx_s, gate_s = _merge(x_sample.reshape(n_s, D_MODEL), oa_s, ob_s, gs["ga"], gs["gb"], p,
                                      tm=min(256, n_s))
    hn = jnp.concatenate([hn_p, hn_s], axis=0)
    top_idx = jnp.concatenate([idx_p[:TOP_K], idx_s[:TOP_K]], axis=1)
    gate = jnp.concatenate([gate_p[:TOP_K], gate_s[:TOP_K]], axis=1)
    f = _moe(hn, top_idx, gate, p, bm=512)
    y_p = (h_p + f[:n_p]).reshape(bp, sp, D_MODEL)
    y_s = (h_s + f[n_p:]).reshape(bs, ts, D_MODEL)

    st = lambda z: z[None]
    return (y_p, y_s, st(k_p), st(v_p), st(ik_p), st(wkv_p), st(gp["pr"][:, -1]),
            st(k_s), st(v_s), st(ik_s), st(wkv_s), st(gs["pr"][:, -1]))
```

```python
import functools

import jax
import jax.numpy as jnp
import numpy as np
from jax import lax
from jax.experimental import pallas as pl
from jax.experimental.pallas import tpu as pltpu

D_MODEL = 1024
PAGE_SIZE = 128
HEAD_DIM = 64
N_Q_HEADS = 8
N_KV_HEADS = 2
Q_PER_KV = N_Q_HEADS // N_KV_HEADS
ATTN_W = N_Q_HEADS * HEAD_DIM
KV_W = N_KV_HEADS * HEAD_DIM
ATTN_SCALE = HEAD_DIM ** -0.5
N_IDX_HEADS = 8
D_IDX = 64
IDX_W = N_IDX_HEADS * D_IDX
IDX_SCALE = (N_IDX_HEADS * D_IDX) ** -0.5
TOPK_MAX = 256
RWKV_HEAD = 64
N_RWKV_HEADS = 8
RWKV_W = N_RWKV_HEADS * RWKV_HEAD
W_LORA = 64
A_LORA = 64
G_LORA = 128
RWKV_PROJ_W = 3 * RWKV_W + W_LORA + A_LORA + G_LORA
LNX_EPS = 64e-5
N_EXPERTS = 32
TOP_K = 4
D_FF = 1024
SWIGLU_LIMIT = 7.0
SWIGLU_ALPHA = 1.702
NORM_EPS = 1e-6

LANES = 128
VMEM_LIMIT = 56 * 1024 * 1024
INT_MIN = -(2 ** 31)
NEG_BIG = -1e30
F32 = jnp.float32
BF16 = jnp.bfloat16
HI = lax.Precision.HIGHEST


def _dot(a, b, precision=None):
    return jnp.dot(a, b, preferred_element_type=F32, precision=precision)


def _dot_nt(a, b, precision=None):
    return lax.dot_general(a, b, (((1,), (1,)), ((), ())), preferred_element_type=F32, precision=precision)


def _dot_tn(a, b, precision=None):
    return lax.dot_general(a, b, (((0,), (0,)), ((), ())), preferred_element_type=F32, precision=precision)


def _group_indicator(width, group):
    r = np.arange(width) // group
    return jnp.asarray((r[:, None] == np.arange(LANES)[None, :]).astype(np.float32))


def _split(x, terms=2):
    parts = []
    for _ in range(terms - 1):
        hi = x.astype(BF16)
        parts.append(hi)
        x = x - hi.astype(F32)
    parts.append(x.astype(BF16))
    return parts


def _mm(a, b, dot=None):
    dot = dot or _dot
    return dot(a[0], b[0]) + (dot(a[0], b[1]) + dot(a[1], b[0]))


def _group_sum(x, ge, get):
    s = sum(_dot(part, ge) for part in _split(x))
    return sum(_dot(part, get) for part in _split(s))


def _const_spec(shape):
    nd = len(shape)
    return pl.BlockSpec(shape, lambda *_: (0,) * nd)


def _inproj_kernel(x_ref, g_ref, wq_ref, wk_ref, wv_ref, wiq_ref, wikw_ref, wpr_ref, wga_ref, wgb_ref,
                   qn_ref, kn_ref, ikn_ref, ge_ref, get_ref, ge2_ref, ge2t_ref,
                   q_out, k_out, kb_out, v_out, vb_out, iq_out, ik_out, ikb_out, iw_out, pr_out, ga_out, gb_out):
    x = x_ref[...]
    ms = jnp.mean(x * x, axis=-1, keepdims=True)
    xn = (x * lax.rsqrt(ms + NORM_EPS) * g_ref[...]).astype(BF16)

    q = _dot(xn, wq_ref[...])
    qs = _group_sum(q * q, ge_ref[...], get_ref[...]) * (1.0 / HEAD_DIM)
    q_out[...] = (q * lax.rsqrt(qs + NORM_EPS) * qn_ref[...]).astype(q_out.dtype)

    k = _dot(xn, wk_ref[...])
    ks = _group_sum(k * k, ge2_ref[...], ge2t_ref[...]) * (1.0 / HEAD_DIM)
    kn = k * lax.rsqrt(ks + NORM_EPS) * kn_ref[...]
    for g in range(N_KV_HEADS):
        k_out[:, g, :] = kn[:, g * HEAD_DIM:(g + 1) * HEAD_DIM]
    kb_out[...] = kn.astype(kb_out.dtype)

    v = _dot(xn, wv_ref[...])
    for g in range(N_KV_HEADS):
        v_out[:, g, :] = v[:, g * HEAD_DIM:(g + 1) * HEAD_DIM]
    vb_out[...] = v.astype(vb_out.dtype)
    iq_out[...] = _dot(xn, wiq_ref[...]).astype(iq_out.dtype)

    ikw = _dot(xn, wikw_ref[...])
    lane = lax.broadcasted_iota(jnp.int32, ikw.shape, 1)
    iks = jnp.sum(jnp.where(lane < D_IDX, ikw * ikw, 0.0), axis=-1, keepdims=True) * (1.0 / D_IDX)
    ikn = ikw[:, :D_IDX] * lax.rsqrt(iks + NORM_EPS) * ikn_ref[...]
    ik_out[...] = ikn
    ikb_out[...] = ikn.astype(ikb_out.dtype)
    iw_out[...] = ikw[:, D_IDX:D_IDX + N_IDX_HEADS]

    pr_out[...] = _dot(xn, wpr_ref[...])
    ga_out[...] = jax.nn.sigmoid(_dot(xn, wga_ref[...])).astype(ga_out.dtype)
    gb_out[...] = jax.nn.sigmoid(_dot(xn, wgb_ref[...])).astype(gb_out.dtype)


def _in_proj(x2d, p, tm):
    n = x2d.shape[0]
    widths = (ATTN_W, KV_W, KV_W, KV_W, KV_W, IDX_W, D_IDX, D_IDX, N_IDX_HEADS, RWKV_PROJ_W, D_MODEL, D_MODEL)
    dtypes = (BF16, F32, BF16, F32, BF16, BF16, F32, BF16, F32, F32, BF16, BF16)
    row = lambda w: pl.BlockSpec((tm, w), lambda i: (i, 0))
    heads_spec = pl.BlockSpec((tm, N_KV_HEADS, HEAD_DIM), lambda i: (i, 0, 0))
    consts = (p["norm_mix"], p["wq"], p["wk"], p["wv"], p["wiq"], p["wikw"], p["wpr"], p["wga"], p["wgb"],
              p["q_norm_t"], p["k_norm_t"], p["ik_norm_t"], p["ge"], p["get"], p["ge2"], p["ge2t"])
    return pl.pallas_call(
        _inproj_kernel,
        grid=(n // tm,),
        in_specs=[row(D_MODEL)] + [_const_spec(c.shape) for c in consts],
        out_specs=[heads_spec if i in (1, 3) else row(w) for i, w in enumerate(widths)],
        out_shape=[jax.ShapeDtypeStruct((n, N_KV_HEADS, HEAD_DIM) if i in (1, 3) else (n, w), d)
                   for i, (w, d) in enumerate(zip(widths, dtypes))],
        compiler_params=pltpu.CompilerParams(dimension_semantics=("parallel",), vmem_limit_bytes=VMEM_LIMIT),
        name="in_proj",
    )(x2d, *consts)


def _stack_heads(x, heads, width):
    return jnp.concatenate([x[:, h * width:(h + 1) * width] for h in heads], axis=0)


def _score_keys(d, iw, tq):
    acc = jnp.zeros((tq, d.shape[1]), F32)
    for h in range(N_IDX_HEADS):
        acc = acc + jnp.maximum(d[h * tq:(h + 1) * tq], 0.0) * iw[:, h:h + 1]
    sc = acc * IDX_SCALE
    sc = jnp.where(sc == 0.0, 0.0, sc)
    bits = pltpu.bitcast(sc, jnp.int32)
    return bits ^ ((bits >> 31) & 0x7FFFFFFF)


def _select_threshold(keys_ref, n_kt, *, tq, tk, top):
    lane_pos = lax.broadcasted_iota(jnp.int32, (tq, tk), 1)

    def count(pred):
        def body(kt, c):
            start = pl.multiple_of(kt * tk, tk)
            m = pred(keys_ref[:, pl.ds(start, tk)], start).astype(jnp.int32)
            part = m[:, 0:LANES]
            for j in range(1, tk // LANES):
                part = part + m[:, j * LANES:(j + 1) * LANES]
            return c + part
        c = lax.fori_loop(0, n_kt, body, jnp.zeros((tq, LANES), jnp.int32))
        return jnp.sum(c, axis=-1, keepdims=True)

    c0 = count(lambda k, s: k >= 0)
    t = jnp.where(c0 >= top, 0, INT_MIN).astype(jnp.int32)
    c_t = jnp.where(c0 >= top, c0, 0)

    def bit_step(i, carry):
        t, c_t = carry
        cand = t | jnp.left_shift(jnp.int32(1), 30 - i)
        c = count(lambda k, s: k >= cand)
        return jnp.where(c >= top, cand, t), jnp.where(c >= top, c, c_t)

    t, c_t = lax.fori_loop(0, 31, bit_step, (t, c_t))
    t = jnp.maximum(t, INT_MIN + 1)

    excess = c_t > top

    @pl.when(jnp.max(excess.astype(jnp.int32)) > 0)
    def _():
        keep = top - count(lambda k, s: k > t)

        def idx_step(i, lim):
            cand = lim | jnp.left_shift(jnp.int32(1), 14 - i)
            c = count(lambda k, s: ((k == t) & (s + lane_pos < cand)))
            return jnp.where(c <= keep, cand, lim)

        lim = lax.fori_loop(0, 15, idx_step, jnp.zeros((tq, 1), jnp.int32))

        def demote(kt, carry):
            start = pl.multiple_of(kt * tk, tk)
            k = keys_ref[:, pl.ds(start, tk)]
            drop = (k == t) & (start + lane_pos >= lim) & excess
            keys_ref[:, pl.ds(start, tk)] = jnp.where(drop, t - 1, k)
            return carry

        lax.fori_loop(0, n_kt, demote, 0)

    return t


def _softmax_step(state, s, bias, pv, tq):
    m_i, l_i, acc = state
    gq, tk = s.shape
    s = (s.reshape(Q_PER_KV, tq, tk) + bias[None]).reshape(gq, tk)
    m_n = jnp.maximum(m_i, jnp.max(s, axis=-1, keepdims=True))
    alpha = jnp.exp(m_i - m_n)
    pm = jnp.exp(s - m_n)
    l_n = alpha * l_i + jnp.sum(pm, axis=-1, keepdims=True)
    return m_n, l_n, alpha * acc + pv(pm.astype(BF16))


def _softmax_init(tq):
    gq = Q_PER_KV * tq
    return tuple((jnp.full((gq, 1), NEG_BIG, F32), jnp.zeros((gq, 1), F32), jnp.zeros((gq, HEAD_DIM), F32))
                 for _ in range(N_KV_HEADS))


def _write_heads(o_ref, res, tq):
    for g in range(N_KV_HEADS):
        _, l_i, acc = res[g]
        og = acc / l_i
        for j in range(Q_PER_KV):
            h = g * Q_PER_KV + j
            o_ref[0, :, h * HEAD_DIM:(h + 1) * HEAD_DIM] = og[j * tq:(j + 1) * tq].astype(o_ref.dtype)


def _dsa_kernel(iq_ref, iw_ref, q_ref, ik_ref, k_ref, v_ref, o_ref, keys_ref, *, tq, tk, q_offset, n_keys, top):
    qi = pl.program_id(1)
    q_base = q_offset + qi * tq
    n_kt = jnp.minimum((q_base + tq + tk - 1) // tk, n_keys // tk)
    q_pos = q_base + lax.broadcasted_iota(jnp.int32, (tq, 1), 0)
    lane_pos = lax.broadcasted_iota(jnp.int32, (tq, tk), 1)

    iq = _stack_heads(iq_ref[0], range(N_IDX_HEADS), D_IDX)
    iw = iw_ref[0]

    def score_tile(kt, carry):
        start = pl.multiple_of(kt * tk, tk)
        key = _score_keys(_dot_nt(iq, ik_ref[0, pl.ds(start, tk), :]), iw, tq)
        keys_ref[:, pl.ds(start, tk)] = jnp.where(start + lane_pos <= q_pos, key, INT_MIN)
        return carry

    lax.fori_loop(0, n_kt, score_tile, 0)

    t = _select_threshold(keys_ref, n_kt, tq=tq, tk=tk, top=top)

    q_all = q_ref[0] * ATTN_SCALE
    qs = [_stack_heads(q_all, range(g * Q_PER_KV, (g + 1) * Q_PER_KV), HEAD_DIM) for g in range(N_KV_HEADS)]

    def attn_tile(kt, carry):
        start = pl.multiple_of(kt * tk, tk)
        bias = jnp.where(keys_ref[:, pl.ds(start, tk)] >= t, 0.0, NEG_BIG)
        k_t = k_ref[0, pl.ds(start, tk), :]
        v_t = v_ref[0, pl.ds(start, tk), :]
        new = []
        for g in range(N_KV_HEADS):
            hs = slice(g * HEAD_DIM, (g + 1) * HEAD_DIM)
            new.append(_softmax_step(carry[g], _dot_nt(qs[g], k_t[:, hs]), bias,
                                     lambda p, hs=hs: _dot(p, v_t[:, hs]), tq))
        return tuple(new)

    _write_heads(o_ref, lax.fori_loop(0, n_kt, attn_tile, _softmax_init(tq)), tq)


def _dsa(iq, iw, q, ik, k, v, *, tq, tk, q_offset, top):
    b, sq, _ = iq.shape
    n_keys = ik.shape[1]
    assert sq % tq == 0 and n_keys % tk == 0 and tk % LANES == 0
    kern = functools.partial(_dsa_kernel, tq=tq, tk=tk, q_offset=q_offset, n_keys=n_keys, top=top)
    return pl.pallas_call(
        kern,
        grid=(b, sq // tq),
        in_specs=[
            pl.BlockSpec((1, tq, IDX_W), lambda bi, qi: (bi, qi, 0)),
            pl.BlockSpec((1, tq, N_IDX_HEADS), lambda bi, qi: (bi, qi, 0)),
            pl.BlockSpec((1, tq, ATTN_W), lambda bi, qi: (bi, qi, 0)),
            pl.BlockSpec((1, n_keys, D_IDX), lambda bi, qi: (bi, 0, 0)),
            pl.BlockSpec((1, n_keys, KV_W), lambda bi, qi: (bi, 0, 0)),
            pl.BlockSpec((1, n_keys, KV_W), lambda bi, qi: (bi, 0, 0)),
        ],
        out_specs=pl.BlockSpec((1, tq, ATTN_W), lambda bi, qi: (bi, qi, 0)),
        out_shape=jax.ShapeDtypeStruct((b, sq, ATTN_W), BF16),
        scratch_shapes=[pltpu.VMEM((tq, n_keys), jnp.int32)],
        compiler_params=pltpu.CompilerParams(dimension_semantics=("parallel", "arbitrary"),
                                             vmem_limit_bytes=VMEM_LIMIT),
        name="dsa",
    )(iq, iw, q, ik, k, v)


def _dsa_paged_kernel(pt_ref, iq_ref, iw_ref, q_ref, ikn_ref, kn_ref, vn_ref, cik_ref, ck_ref, cv_ref, o_ref,
                      ik_buf, k_buf, v_buf, sems, keys_ref, *, tq, n_pages, ppt, tk_sel, top):
    b = pl.program_id(0)
    slot = b % 2
    past = n_pages * PAGE_SIZE
    tk = ppt * PAGE_SIZE

    def page_copies(bi, sl, p):
        page = pt_ref[bi * n_pages + p]
        return (pltpu.make_async_copy(cik_ref.at[page], ik_buf.at[sl, p], sems.at[sl, 0]),
                pltpu.make_async_copy(ck_ref.at[page], k_buf.at[sl, p], sems.at[sl, 1]),
                pltpu.make_async_copy(cv_ref.at[page], v_buf.at[sl, p], sems.at[sl, 2]))

    def fetch(bi, sl):
        def body(p, carry):
            for cp in page_copies(bi, sl, p):
                cp.start()
            return carry
        lax.fori_loop(0, n_pages, body, 0)

    @pl.when(b == 0)
    def _():
        fetch(0, 0)

    @pl.when(b + 1 < pl.num_programs(0))
    def _():
        fetch(b + 1, 1 - slot)

    def wait_page(p, carry):
        for cp in page_copies(b, slot, p):
            cp.wait()
        return carry

    lax.fori_loop(0, n_pages, wait_page, 0)

    def page_cols(buf, i, idx=()):
        return jnp.concatenate([buf[(slot, i * ppt + j) + idx] for j in range(ppt)], axis=1).astype(BF16)

    iq = _stack_heads(iq_ref[0], range(N_IDX_HEADS), D_IDX)
    iw = iw_ref[0]
    row = lax.broadcasted_iota(jnp.int32, (tq, PAGE_SIZE), 0)
    lane = lax.broadcasted_iota(jnp.int32, (tq, PAGE_SIZE), 1)

    def score_tile(i, carry):
        start = pl.multiple_of(i * tk, tk)
        keys_ref[:, pl.ds(start, tk)] = _score_keys(_dot(iq, page_cols(ik_buf, i)), iw, tq)
        return carry

    lax.fori_loop(0, n_pages // ppt, score_tile, 0, unroll=2)
    new_keys = _score_keys(_dot(iq, ikn_ref[0]), iw, tq)
    keys_ref[:, past:past + PAGE_SIZE] = jnp.where(lane <= row, new_keys, INT_MIN)

    t = _select_threshold(keys_ref, (past + PAGE_SIZE) // tk_sel, tq=tq, tk=tk_sel, top=top)

    q_all = q_ref[0] * ATTN_SCALE
    qs = [_stack_heads(q_all, range(g * Q_PER_KV, (g + 1) * Q_PER_KV), HEAD_DIM) for g in range(N_KV_HEADS)]

    def attn_tile(i, carry):
        start = pl.multiple_of(i * tk, tk)
        bias = jnp.where(keys_ref[:, pl.ds(start, tk)] >= t, 0.0, NEG_BIG)
        new = []
        for g in range(N_KV_HEADS):
            v_t = page_cols(v_buf, i, (g,))
            new.append(_softmax_step(carry[g], _dot(qs[g], page_cols(k_buf, i, (g,))), bias,
                                     lambda p, v_t=v_t: _dot_nt(p, v_t), tq))
        return tuple(new)

    res = lax.fori_loop(0, n_pages // ppt, attn_tile, _softmax_init(tq), unroll=2)
    bias = jnp.where(keys_ref[:, past:past + PAGE_SIZE] >= t, 0.0, NEG_BIG)
    res = tuple(_softmax_step(res[g], _dot(qs[g], kn_ref[0, g]), bias,
                              lambda p, g=g: _dot_nt(p, vn_ref[0, g]), tq) for g in range(N_KV_HEADS))
    _write_heads(o_ref, res, tq)


def _dsa_paged(iq, iw, q, ikn, kn, vn, cik_t, ck_t, cv_t, page_table, *, tq, top):
    b = iq.shape[0]
    n_pages = page_table.shape[1]
    ppt = next(d for d in (4, 2, 1) if n_pages % d == 0)
    n_lane_tiles = n_pages + 1
    tk_sel = LANES * next(d for d in range(8, 0, -1) if n_lane_tiles % d == 0)
    kern = functools.partial(_dsa_paged_kernel, tq=tq, n_pages=n_pages, ppt=ppt, tk_sel=tk_sel, top=top)
    per_seq = lambda *blk: pl.BlockSpec((1,) + blk, lambda bi, pt: (bi,) + (0,) * len(blk))
    grid_spec = pltpu.PrefetchScalarGridSpec(
        num_scalar_prefetch=1,
        grid=(b,),
        in_specs=[per_seq(tq, IDX_W), per_seq(tq, N_IDX_HEADS), per_seq(tq, ATTN_W),
                  per_seq(D_IDX, PAGE_SIZE), per_seq(N_KV_HEADS, HEAD_DIM, PAGE_SIZE),
                  per_seq(N_KV_HEADS, HEAD_DIM, PAGE_SIZE),
                  pl.BlockSpec(memory_space=pl.ANY), pl.BlockSpec(memory_space=pl.ANY),
                  pl.BlockSpec(memory_space=pl.ANY)],
        out_specs=per_seq(tq, ATTN_W),
        scratch_shapes=[pltpu.VMEM((2, n_pages, D_IDX, PAGE_SIZE), F32),
                        pltpu.VMEM((2, n_pages, N_KV_HEADS, HEAD_DIM, PAGE_SIZE), F32),
                        pltpu.VMEM((2, n_pages, N_KV_HEADS, HEAD_DIM, PAGE_SIZE), F32),
                        pltpu.SemaphoreType.DMA((2, 3)),
                        pltpu.VMEM((tq, (n_pages + 1) * PAGE_SIZE), jnp.int32)],
    )
    return pl.pallas_call(
        kern,
        grid_spec=grid_spec,
        out_shape=jax.ShapeDtypeStruct((b, tq, ATTN_W), BF16),
        compiler_params=pltpu.CompilerParams(dimension_semantics=("arbitrary",), vmem_limit_bytes=VMEM_LIMIT),
        name="dsa_paged",
    )(page_table.reshape(-1), iq, iw, q, ikn, kn, vn, cik_t, ck_t, cv_t)


def _rwkv_kernel(pr_ref, sh0_ref, z0_ref, mu_ref, w0_ref, wup_ref, a0_ref, aup_ref, gup_ref, kk_ref, ka_ref,
                 rk_ref, lnw_ref, lnb_ref, ge_ref, get_ref,
                 ob_ref, zout_ref,
                 z_scr, prev_scr, r_s, k_s, v_s, a_s, b_s, lw_s, y_s, *, tt, chunk, t_valid):
    ti = pl.program_id(1)
    n_t = pl.num_programs(1)
    nh, hd = N_RWKV_HEADS, RWKV_HEAD

    @pl.when(ti == 0)
    def _():
        z_scr[...] = z0_ref[0]
        prev_scr[...] = sh0_ref[0]

    gsum = lambda z: _group_sum(z, ge_ref[...], get_ref[...])

    pr = pr_ref[0]
    row = lax.broadcasted_iota(jnp.int32, (tt, 1), 0)
    prev = jnp.where(row == 0, prev_scr[...], pltpu.roll(pr, 1, 0))
    prev_scr[...] = pr[tt - 1:tt]
    m = pr + mu_ref[...] * (prev - pr)
    r = m[:, 0:RWKV_W]
    k = m[:, RWKV_W:2 * RWKV_W]
    v = m[:, 2 * RWKV_W:3 * RWKV_W]
    o = 3 * RWKV_W
    wd = m[:, o:o + W_LORA]
    ad = m[:, o + W_LORA:o + W_LORA + A_LORA]
    gd = m[:, o + W_LORA + A_LORA:]

    lora = lambda z, w_ref: _mm(_split(z), _split(w_ref[...]))
    u = -(w0_ref[...] + lora(jnp.tanh(wd), wup_ref))
    softplus = jnp.maximum(u, 0.0) + jnp.log(1.0 + jnp.exp(-jnp.abs(u)))
    lw = -jnp.exp(-softplus - 0.5)
    a = jax.nn.sigmoid(a0_ref[...] + lora(ad, aup_ref))
    gate = lora(jax.nn.sigmoid(gd), gup_ref)
    kk = k * kk_ref[...]
    kk = kk / jnp.maximum(jnp.sqrt(gsum(kk * kk)), 1e-12)
    k2 = k * (1.0 + (a - 1.0) * ka_ref[...])
    bonus = gsum(r * k2 * rk_ref[...])
    av = -kk
    bv = kk * a
    if t_valid < tt:
        ok = (row < t_valid).astype(F32)
        k2, v, av, bv, lw = k2 * ok, v * ok, av * ok, bv * ok, lw * ok
    r_s[...] = r
    k_s[...] = k2
    v_s[...] = v
    a_s[...] = av
    b_s[...] = bv
    lw_s[...] = lw

    c = chunk
    ri = lax.broadcasted_iota(jnp.int32, (c, c), 0)
    ci = lax.broadcasted_iota(jnp.int32, (c, c), 1)
    tri_incl = (ci <= ri)
    tri_strict = (ci < ri)
    ltri = tri_incl.astype(BF16)
    eye_h = (lax.broadcasted_iota(jnp.int32, (hd, hd), 0) == lax.broadcasted_iota(jnp.int32, (hd, hd), 1))
    n_double = max(int(np.ceil(np.log2(c))), 1)
    heads = range(nh)
    hsl = [slice(h * hd, (h + 1) * hd) for h in heads]

    def chunk_body(ci_, carry):
        s0 = pl.multiple_of(ci_ * c, c)
        sl = pl.ds(s0, c)
        lwc = lw_s[sl, :]
        cum = sum(_dot(ltri, part) for part in _split(lwc, 3))
        cum_end = cum[c - 1:c, :]
        g_end = jnp.exp(cum_end)
        at = a_s[sl, :] * jnp.exp(cum - lwc)
        rt = r_s[sl, :] * jnp.exp(cum)
        g_inv = jnp.exp(-cum)
        g_tail = jnp.exp(cum_end - cum)
        bt = b_s[sl, :] * g_inv
        kt = k_s[sl, :] * g_inv
        bc = b_s[sl, :] * g_tail
        kc = k_s[sl, :] * g_tail
        vc = v_s[sl, :]

        left = [_split(jnp.concatenate([at[:, s], rt[:, s]], axis=0)) for s in hsl]
        right = [_split(jnp.concatenate([bt[:, s], kt[:, s]], axis=0)) for s in hsl]
        amat = [_mm(left[h], right[h], _dot_nt) for h in heads]
        a_ab = [jnp.where(tri_strict, amat[h][:c, :c], 0.0) for h in heads]
        a_ak = [_split(jnp.where(tri_strict, amat[h][:c, c:], 0.0)) for h in heads]
        a_rb = [_split(jnp.where(tri_incl, amat[h][c:, :c], 0.0)) for h in heads]
        a_rk = [_split(jnp.where(tri_incl, amat[h][c:, c:], 0.0)) for h in heads]
        vh = [_split(vc[:, s]) for s in hsl]
        akv = [_mm(a_ak[h], vh[h]) for h in heads]
        uu = [jnp.concatenate([at[:, hsl[h]], akv[h]], axis=1) for h in heads]
        pw = a_ab
        for step in range(n_double):
            pws = [_split(z) for z in pw]
            uus = [_split(z) for z in uu]
            uu = [uu[h] + _mm(pws[h], uus[h]) for h in heads]
            if step + 1 < n_double:
                pw = [_mm(pws[h], pws[h]) for h in heads]
        uus = [_split(z) for z in uu]
        x1 = [_mm(a_rb[h], uus[h]) for h in heads]
        x2 = [_mm(a_rk[h], vh[h]) for h in heads]
        mn = [_mm(_split(bc[:, hsl[h]]), uus[h], _dot_tn) for h in heads]
        nk = [_mm(_split(kc[:, hsl[h]]), vh[h], _dot_tn) for h in heads]
        pm = []
        for h in heads:
            p2 = rt[:, hsl[h]] + x1[h][:, :hd]
            mh = jnp.where(eye_h, g_end[:, hsl[h]], 0.0) + mn[h][:, :hd]
            pm.append(_split(jnp.concatenate([p2, mh], axis=0)))
        res = [_mm(pm[h], _split(z_scr[h])) for h in heads]
        for h in heads:
            y_s[sl, hsl[h]] = res[h][:c] + x1[h][:, hd:] + x2[h]
            z_scr[h] = res[h][c:] + mn[h][:, hd:] + nk[h]
        return carry

    lax.fori_loop(0, tt // c, chunk_body, 0)

    y = y_s[...]
    mean = gsum(y) * (1.0 / hd)
    dlt = y - mean
    var = gsum(dlt * dlt) * (1.0 / hd)
    yn = dlt * lax.rsqrt(var + LNX_EPS) * lnw_ref[...] + lnb_ref[...]
    ob_ref[0] = ((yn + bonus * v_s[...]) * gate).astype(ob_ref.dtype)

    @pl.when(ti == n_t - 1)
    def _():
        zout_ref[0] = z_scr[...]


def _rwkv(pr, shift0, z0, p, *, tt, chunk, t_valid):
    b, t, _ = pr.shape
    assert t % tt == 0 and tt % chunk == 0
    consts = (p["shift_mu"], p["w0"], p["w_lora_up"], p["a0"], p["a_lora_up"], p["g_lora_up"], p["k_k"], p["k_a"],
              p["r_k"], p["ln_x_w"], p["ln_x_b"], p["ge"], p["get"])
    kern = functools.partial(_rwkv_kernel, tt=tt, chunk=chunk, t_valid=t_valid)
    wide = lambda: pltpu.VMEM((tt, RWKV_W), F32)
    return pl.pallas_call(
        kern,
        grid=(b, t // tt),
        in_specs=[
            pl.BlockSpec((1, tt, RWKV_PROJ_W), lambda bi, ti: (bi, ti, 0)),
            pl.BlockSpec((1, 1, RWKV_PROJ_W), lambda bi, ti: (bi, 0, 0)),
            pl.BlockSpec((1, N_RWKV_HEADS, RWKV_HEAD, RWKV_HEAD), lambda bi, ti: (bi, 0, 0, 0)),
        ] + [_const_spec(c.shape) for c in consts],
        out_specs=[
            pl.BlockSpec((1, tt, RWKV_W), lambda bi, ti: (bi, ti, 0)),
            pl.BlockSpec((1, N_RWKV_HEADS, RWKV_HEAD, RWKV_HEAD), lambda bi, ti: (bi, 0, 0, 0)),
        ],
        out_shape=[jax.ShapeDtypeStruct((b, t, RWKV_W), BF16),
                   jax.ShapeDtypeStruct((b, N_RWKV_HEADS, RWKV_HEAD, RWKV_HEAD), F32)],
        scratch_shapes=[pltpu.VMEM((N_RWKV_HEADS, RWKV_HEAD, RWKV_HEAD), F32),
                        pltpu.VMEM((1, RWKV_PROJ_W), F32),
                        wide(), wide(), wide(), wide(), wide(), wide(), wide()],
        compiler_params=pltpu.CompilerParams(dimension_semantics=("parallel", "arbitrary"),
                                             vmem_limit_bytes=VMEM_LIMIT),
        name="rwkv",
    )(pr, shift0, z0, *consts)


def _merge_kernel(x_ref, oa_ref, ob_ref, ga_ref, gb_ref, wa_ref, wb_ref, wo_ref, nf_ref, rwt_ref, rb_ref,
                  h_out, hn_out, idx_out, gate_out):
    ma = _dot(oa_ref[...], wa_ref[...])
    mb = _dot(ob_ref[...], wb_ref[...])
    mm = ga_ref[...].astype(F32) * ma + gb_ref[...].astype(F32) * mb
    h = x_ref[...] + _dot(mm.astype(BF16), wo_ref[...])
    h_out[...] = h
    ms = jnp.mean(h * h, axis=-1, keepdims=True)
    hn = h * lax.rsqrt(ms + NORM_EPS) * nf_ref[...]
    hn_out[...] = hn.astype(hn_out.dtype)
    logits = _mm(_split(rwt_ref[...]), _split(hn), _dot_nt) + rb_ref[...]
    tm = logits.shape[1]
    expert = lax.broadcasted_iota(jnp.int32, logits.shape, 0)
    vals, idxs = [], []
    for _ in range(TOP_K):
        mx = jnp.max(logits, axis=0, keepdims=True)
        ix = jnp.min(jnp.where(logits == mx, expert, N_EXPERTS), axis=0, keepdims=True)
        vals.append(mx)
        idxs.append(ix)
        logits = jnp.where(expert == ix, -jnp.inf, logits)
    es = [jnp.exp(v - vals[0]) for v in vals]
    den = es[0] + es[1] + es[2] + es[3]
    pad = 8 - TOP_K
    idx_out[...] = jnp.concatenate(idxs + [jnp.zeros((pad, tm), jnp.int32)], axis=0)
    gate_out[...] = jnp.concatenate([e / den for e in es] + [jnp.zeros((pad, tm), F32)], axis=0)


def _merge(x2d, oa, ob, ga, gb, p, tm):
    n = x2d.shape[0]
    row = lambda w: pl.BlockSpec((tm, w), lambda i: (i, 0))
    consts = (p["w_proj_a"], p["w_proj_b"], p["w_out"], p["norm_ffn"], p["router_w"], p["router_b"])
    return pl.pallas_call(
        _merge_kernel,
        grid=(n // tm,),
        in_specs=[row(D_MODEL), row(ATTN_W), row(RWKV_W), row(D_MODEL), row(D_MODEL)]
        + [_const_spec(c.shape) for c in consts],
        out_specs=[row(D_MODEL), row(D_MODEL)] + [pl.BlockSpec((8, tm), lambda i: (0, i))] * 2,
        out_shape=[jax.ShapeDtypeStruct((n, D_MODEL), F32), jax.ShapeDtypeStruct((n, D_MODEL), BF16),
                   jax.ShapeDtypeStruct((8, n), jnp.int32), jax.ShapeDtypeStruct((8, n), F32)],
        compiler_params=pltpu.CompilerParams(dimension_semantics=("parallel",), vmem_limit_bytes=VMEM_LIMIT),
        name="merge",
    )(x2d, oa, ob, ga, gb, *consts)


def _moe_kernel(be_ref, nb_ref, x_ref, wgu_ref, bgu_ref, wd_ref, bd_ref, o_ref, wgu_s, wd_s):
    i = pl.program_id(0)
    used = i < nb_ref[0]

    @pl.when(used & ((i == 0) | (be_ref[i] != be_ref[jnp.maximum(i - 1, 0)])))
    def _():
        wgu_s[...] = wgu_ref[0].astype(wgu_s.dtype)
        wd_s[...] = wd_ref[0].astype(wd_s.dtype)

    @pl.when(used)
    def _():
        hcat = _dot(x_ref[...], wgu_s[...]) + bgu_ref[0]
        glu = jnp.minimum(hcat[:, :D_FF], SWIGLU_LIMIT)
        lin = jnp.clip(hcat[:, D_FF:], -SWIGLU_LIMIT, SWIGLU_LIMIT)
        act = glu * jax.nn.sigmoid(SWIGLU_ALPHA * glu) * (lin + 1.0)
        o_ref[...] = _dot(act.astype(BF16), wd_s[...]) + bd_ref[0]

    @pl.when(jnp.logical_not(used))
    def _():
        o_ref[...] = jnp.zeros(o_ref.shape, o_ref.dtype)


def _moe_rows(xg, block_exp, n_used, p, bm):
    rows = xg.shape[0]
    grid_spec = pltpu.PrefetchScalarGridSpec(
        num_scalar_prefetch=2,
        grid=(rows // bm,),
        in_specs=[
            pl.BlockSpec((bm, D_MODEL), lambda i, be, nb: (i, 0)),
            pl.BlockSpec((1, D_MODEL, 2 * D_FF), lambda i, be, nb: (be[i], 0, 0)),
            pl.BlockSpec((1, 1, 2 * D_FF), lambda i, be, nb: (be[i], 0, 0)),
            pl.BlockSpec((1, D_FF, D_MODEL), lambda i, be, nb: (be[i], 0, 0)),
            pl.BlockSpec((1, 1, D_MODEL), lambda i, be, nb: (be[i], 0, 0)),
        ],
        out_specs=pl.BlockSpec((bm, D_MODEL), lambda i, be, nb: (i, 0)),
        scratch_shapes=[pltpu.VMEM((D_MODEL, 2 * D_FF), BF16), pltpu.VMEM((D_FF, D_MODEL), BF16)],
    )
    return pl.pallas_call(
        _moe_kernel,
        grid_spec=grid_spec,
        out_shape=jax.ShapeDtypeStruct((rows, D_MODEL), F32),
        compiler_params=pltpu.CompilerParams(dimension_semantics=("arbitrary",), vmem_limit_bytes=VMEM_LIMIT),
        name="moe",
    )(block_exp, n_used, xg, p["w_gate_up"], p["b_gate_up"], p["w_down"], p["b_down"])


def _moe(hn, top_idx, gate, p, bm):
    n = hn.shape[0]
    nk = n * TOP_K
    e_flat = top_idx.reshape(-1)
    onehot = (e_flat[:, None] == jnp.arange(N_EXPERTS, dtype=jnp.int32)[None, :]).astype(jnp.int32)
    csum = jnp.cumsum(onehot, axis=0)
    counts = csum[-1]
    rank = jnp.take_along_axis(csum, e_flat[:, None], axis=1)[:, 0] - 1
    padded = (counts + bm - 1) // bm * bm
    pends = jnp.cumsum(padded)
    pstarts = pends - padded
    dest = pstarts[e_flat] + rank
    n_blocks = -(-nk // bm) + N_EXPERTS
    rows = n_blocks * bm
    tok_flat = jnp.arange(nk, dtype=jnp.int32) % n
    row_tok = jnp.zeros((rows,), jnp.int32).at[dest].set(tok_flat, unique_indices=True)
    block_start = jnp.arange(n_blocks, dtype=jnp.int32) * bm
    block_exp = jnp.minimum(jnp.sum((pends[None, :] <= block_start[:, None]).astype(jnp.int32), axis=1),
                            N_EXPERTS - 1)
    n_used = (pends[-1:] // bm).astype(jnp.int32)
    out = _moe_rows(hn[row_tok], block_exp, n_used, p, bm)
    return (out[dest.reshape(TOP_K, n)] * gate[:, :, None]).sum(axis=0)


def _prep_params(norm_mix, w_in, q_norm, k_norm, idx_k_norm, shift_mu, w0, w_lora_up, a0, a_lora_up, g_lora_up, k_k,
                 k_a, r_k, ln_x_w, ln_x_b, w_proj_a, w_proj_b, w_out, norm_ffn, router_w, router_b, w_gate_up,
                 b_gate_up, w_down, b_down):
    splits = (ATTN_W, KV_W, KV_W, IDX_W, D_IDX, N_IDX_HEADS, RWKV_PROJ_W, D_MODEL, D_MODEL)
    cuts = np.cumsum(splits)[:-1].tolist()
    wq, wk, wv, wiq, wik, wiw, wpr, wga, wgb = jnp.split(w_in.astype(BF16), cuts, axis=-1)
    wikw = jnp.concatenate([wik, wiw, jnp.zeros((D_MODEL, LANES - D_IDX - N_IDX_HEADS), BF16)], axis=-1)
    row = lambda z: z.reshape(1, -1).astype(F32)
    ge = _group_indicator(RWKV_W, RWKV_HEAD).astype(BF16)
    ge2 = _group_indicator(KV_W, HEAD_DIM).astype(BF16)
    return dict(
        norm_mix=row(norm_mix), wq=wq, wk=wk, wv=wv, wiq=wiq, wikw=wikw, wpr=wpr, wga=wga, wgb=wgb,
        q_norm_t=row(jnp.tile(q_norm, N_Q_HEADS)), k_norm_t=row(jnp.tile(k_norm, N_KV_HEADS)),
        ik_norm_t=row(idx_k_norm),
        ge=ge, get=ge.T, ge2=ge2, ge2t=ge2.T,
        shift_mu=row(shift_mu), w0=row(w0), w_lora_up=w_lora_up, a0=row(a0), a_lora_up=a_lora_up,
        g_lora_up=g_lora_up, k_k=row(k_k), k_a=row(k_a), r_k=row(r_k), ln_x_w=row(ln_x_w), ln_x_b=row(ln_x_b),
        w_proj_a=w_proj_a.astype(BF16), w_proj_b=w_proj_b.astype(BF16), w_out=w_out.astype(BF16),
        norm_ffn=row(norm_ffn), router_w=router_w.T, router_b=router_b.reshape(-1, 1),
        w_gate_up=w_gate_up, b_gate_up=b_gate_up[:, None, :], w_down=w_down, b_down=b_down[:, None, :],
    )


def _pad_axis(z, axis, size):
    if z.shape[axis] == size:
        return z
    pad = [(0, 0)] * z.ndim
    pad[axis] = (0, size - z.shape[axis])
    return jnp.pad(z, pad)


def _group(x, p, *, tm):
    b, t, _ = x.shape
    names = ("q", "k", "kb", "v", "vb", "iq", "ik", "ikb", "iw", "pr", "ga", "gb")
    g = dict(zip(names, _in_proj(x.reshape(b * t, D_MODEL), p, tm)), b=b, t=t)
    g["pr"] = g["pr"].reshape(b, t, RWKV_PROJ_W)
    return g


def _attend(g, ik_all, k_all, v_all, *, tq, tk, q_offset, top):
    b, t = g["b"], g["t"]
    tp = -(-t // tq) * tq
    n_keys = -(-ik_all.shape[1] // tk) * tk
    seq = lambda z: _pad_axis(z.reshape(b, t, -1), 1, tp)
    keys = lambda z: _pad_axis(z, 1, n_keys)
    oa = _dsa(seq(g["iq"]), seq(g["iw"]), seq(g["q"]), keys(ik_all), keys(k_all), keys(v_all),
              tq=tq, tk=tk, q_offset=q_offset, top=top)
    return oa[:, :t].reshape(b * t, ATTN_W)


def _mix(g, shift0, wkv0, p, *, tt, chunk):
    b, t = g["b"], g["t"]
    tp = -(-t // tt) * tt
    ob, z = _rwkv(_pad_axis(g["pr"], 1, tp), shift0[:, None, :], jnp.swapaxes(wkv0, -1, -2), p,
                  tt=tt, chunk=chunk, t_valid=min(t, tt) if tp != t else tt)
    return ob[:, :t].reshape(b * t, RWKV_W), jnp.swapaxes(z, -1, -2)


def kernel(x_prompt, x_sample, cache_k, cache_v, cache_idx_k, page_table, state_wkv, state_shift, norm_mix, w_in, q_norm, k_norm, idx_k_norm, shift_mu, w0, w_lora_up, a0, a_lora_up, g_lora_up, k_k, k_a, r_k, ln_x_w, ln_x_b, w_proj_a, w_proj_b, w_out, norm_ffn, router_w, router_b, w_gate_up, b_gate_up, w_down, b_down):
    depth = norm_mix.shape[0]
    assert depth == 1
    params = (norm_mix, w_in, q_norm, k_norm, idx_k_norm, shift_mu, w0, w_lora_up, a0, a_lora_up, g_lora_up, k_k, k_a,
              r_k, ln_x_w, ln_x_b, w_proj_a, w_proj_b, w_out, norm_ffn, router_w, router_b, w_gate_up, b_gate_up,
              w_down, b_down)
    p = _prep_params(*[z[0] for z in params])
    bp, sp, _ = x_prompt.shape
    bs, ts, _ = x_sample.shape
    n_p, n_s = bp * sp, bs * ts
    past = page_table.shape[1] * PAGE_SIZE

    gp = _group(x_prompt, p, tm=min(256, n_p))
    k_p = gp["k"].reshape(bp, sp, N_KV_HEADS, HEAD_DIM)
    v_p = gp["v"].reshape(bp, sp, N_KV_HEADS, HEAD_DIM)
    ik_p = gp["ik"].reshape(bp, sp, D_IDX)
    oa_p = _attend(gp, gp["ikb"].reshape(bp, sp, D_IDX), gp["kb"].reshape(bp, sp, KV_W),
                   gp["vb"].reshape(bp, sp, KV_W), tq=min(128, sp), tk=min(512, sp), q_offset=0,
                   top=min(TOPK_MAX, sp // 4))
    ob_p, wkv_p = _mix(gp, jnp.zeros((bp, RWKV_PROJ_W), F32),
                       jnp.zeros((bp, N_RWKV_HEADS, RWKV_HEAD, RWKV_HEAD), F32), p,
                       tt=min(256, sp), chunk=min(64, sp))

    gs = _group(x_sample, p, tm=min(256, n_s))
    k_s = gs["k"].reshape(bs, ts, N_KV_HEADS, HEAD_DIM)
    v_s = gs["v"].reshape(bs, ts, N_KV_HEADS, HEAD_DIM)
    ik_s = gs["ik"].reshape(bs, ts, D_IDX)
    tq_s = 16
    seq = lambda z: _pad_axis(z.reshape(bs, ts, -1), 1, tq_s)
    new_t = lambda z, *hd: _pad_axis(jnp.moveaxis(z.reshape((bs, ts) + hd), 1, -1), len(hd) + 1, PAGE_SIZE)
    oa_s = _dsa_paged(seq(gs["iq"]), seq(gs["iw"]), seq(gs["q"]),
                      new_t(gs["ikb"], D_IDX), new_t(gs["kb"], N_KV_HEADS, HEAD_DIM),
                      new_t(gs["vb"], N_KV_HEADS, HEAD_DIM),
                      jnp.transpose(cache_idx_k[0], (0, 2, 1)), jnp.transpose(cache_k[0], (0, 2, 3, 1)),
                      jnp.transpose(cache_v[0], (0, 2, 3, 1)), page_table,
                      tq=tq_s, top=min(TOPK_MAX, (past + ts) // 4))[:, :ts].reshape(n_s, ATTN_W)
    ob_s, wkv_s = _mix(gs, state_shift[0], state_wkv[0], p, tt=8, chunk=8)

    h_p, hn_p, idx_p, gate_p = _merge(x_prompt.reshape(n_p, D_MODEL), oa_p, ob_p, gp["ga"], gp["gb"], p,
                                      tm=min(256, n_p))
    h_s, hn_s, idx_s, gate_s = _merge(x_sample.reshape(n_s, D_MODEL), oa_s, ob_s, gs["ga"], gs["gb"], p,
                                      tm=min(256, n_s))
    hn = jnp.concatenate([hn_p, hn_s], axis=0)
    top_idx = jnp.concatenate([idx_p[:TOP_K], idx_s[:TOP_K]], axis=1)
    gate = jnp.concatenate([gate_p[:TOP_K], gate_s[:TOP_K]], axis=1)
    f = _moe(hn, top_idx, gate, p, bm=512)
    y_p = (h_p + f[:n_p]).reshape(bp, sp, D_MODEL)
    y_s = (h_s + f[n_p:]).reshape(bs, ts, D_MODEL)

    st = lambda z: z[None]
    return (y_p, y_s, st(k_p), st(v_p), st(ik_p), st(wkv_p), st(gp["pr"][:, -1]),
            st(k_s), st(v_s), st(ik_s), st(wkv_s), st(gs["pr"][:, -1]))
```

```python
import functools

import jax
import jax.numpy as jnp
import numpy as np
from jax import lax
from jax.experimental import pallas as pl
from jax.experimental.pallas import tpu as pltpu

D_MODEL = 1024
PAGE_SIZE = 128
HEAD_DIM = 64
N_Q_HEADS = 8
N_KV_HEADS = 2
Q_PER_KV = N_Q_HEADS // N_KV_HEADS
ATTN_W = N_Q_HEADS * HEAD_DIM
KV_W = N_KV_HEADS * HEAD_DIM
ATTN_SCALE = HEAD_DIM ** -0.5
N_IDX_HEADS = 8
D_IDX = 64
IDX_W = N_IDX_HEADS * D_IDX
IDX_SCALE = (N_IDX_HEADS * D_IDX) ** -0.5
TOPK_MAX = 256
RWKV_HEAD = 64
N_RWKV_HEADS = 8
RWKV_W = N_RWKV_HEADS * RWKV_HEAD
W_LORA = 64
A_LORA = 64
G_LORA = 128
RWKV_PROJ_W = 3 * RWKV_W + W_LORA + A_LORA + G_LORA
LNX_EPS = 64e-5
N_EXPERTS = 32
TOP_K = 4
D_FF = 1024
SWIGLU_LIMIT = 7.0
SWIGLU_ALPHA = 1.702
NORM_EPS = 1e-6

LANES = 128
VMEM_LIMIT = 56 * 1024 * 1024
INT_MIN = -(2 ** 31)
NEG_BIG = -1e30
F32 = jnp.float32
BF16 = jnp.bfloat16
HI = lax.Precision.HIGHEST


def _dot(a, b, precision=None):
    return jnp.dot(a, b, preferred_element_type=F32, precision=precision)


def _dot_nt(a, b, precision=None):
    return lax.dot_general(a, b, (((1,), (1,)), ((), ())), preferred_element_type=F32, precision=precision)


def _dot_tn(a, b, precision=None):
    return lax.dot_general(a, b, (((0,), (0,)), ((), ())), preferred_element_type=F32, precision=precision)


def _group_indicator(width, group):
    r = np.arange(width) // group
    return jnp.asarray((r[:, None] == np.arange(LANES)[None, :]).astype(np.float32))


def _split(x, terms=2):
    parts = []
    for _ in range(terms - 1):
        hi = x.astype(BF16)
        parts.append(hi)
        x = x - hi.astype(F32)
    parts.append(x.astype(BF16))
    return parts


def _mm(a, b, dot=None):
    dot = dot or _dot
    return dot(a[0], b[0]) + (dot(a[0], b[1]) + dot(a[1], b[0]))


def _group_sum(x, ge, get):
    s = sum(_dot(part, ge) for part in _split(x))
    return sum(_dot(part, get) for part in _split(s))


def _const_spec(shape):
    nd = len(shape)
    return pl.BlockSpec(shape, lambda *_: (0,) * nd)


def _inproj_kernel(x_ref, g_ref, wq_ref, wk_ref, wv_ref, wiq_ref, wikw_ref, wpr_ref, wga_ref, wgb_ref,
                   qn_ref, kn_ref, ikn_ref, ge_ref, get_ref, ge2_ref, ge2t_ref,
                   q_out, k_out, kb_out, v_out, vb_out, iq_out, ik_out, ikb_out, iw_out, pr_out, ga_out, gb_out):
    x = x_ref[...]
    ms = jnp.mean(x * x, axis=-1, keepdims=True)
    xn = (x * lax.rsqrt(ms + NORM_EPS) * g_ref[...]).astype(BF16)

    q = _dot(xn, wq_ref[...])
    qs = _group_sum(q * q, ge_ref[...], get_ref[...]) * (1.0 / HEAD_DIM)
    q_out[...] = (q * lax.rsqrt(qs + NORM_EPS) * qn_ref[...]).astype(q_out.dtype)

    k = _dot(xn, wk_ref[...])
    ks = _group_sum(k * k, ge2_ref[...], ge2t_ref[...]) * (1.0 / HEAD_DIM)
    kn = k * lax.rsqrt(ks + NORM_EPS) * kn_ref[...]
    for g in range(N_KV_HEADS):
        k_out[:, g, :] = kn[:, g * HEAD_DIM:(g + 1) * HEAD_DIM]
    kb_out[...] = kn.astype(kb_out.dtype)

    v = _dot(xn, wv_ref[...])
    for g in range(N_KV_HEADS):
        v_out[:, g, :] = v[:, g * HEAD_DIM:(g + 1) * HEAD_DIM]
    vb_out[...] = v.astype(vb_out.dtype)
    iq_out[...] = _dot(xn, wiq_ref[...]).astype(iq_out.dtype)

    ikw = _dot(xn, wikw_ref[...])
    lane = lax.broadcasted_iota(jnp.int32, ikw.shape, 1)
    iks = jnp.sum(jnp.where(lane < D_IDX, ikw * ikw, 0.0), axis=-1, keepdims=True) * (1.0 / D_IDX)
    ikn = ikw[:, :D_IDX] * lax.rsqrt(iks + NORM_EPS) * ikn_ref[...]
    ik_out[...] = ikn
    ikb_out[...] = ikn.astype(ikb_out.dtype)
    iw_out[...] = ikw[:, D_IDX:D_IDX + N_IDX_HEADS]

    pr_out[...] = _dot(xn, wpr_ref[...])
    ga_out[...] = jax.nn.sigmoid(_dot(xn, wga_ref[...])).astype(ga_out.dtype)
    gb_out[...] = jax.nn.sigmoid(_dot(xn, wgb_ref[...])).astype(gb_out.dtype)


def _in_proj(x2d, p, tm):
    n = x2d.shape[0]
    widths = (ATTN_W, KV_W, KV_W, KV_W, KV_W, IDX_W, D_IDX, D_IDX, N_IDX_HEADS, RWKV_PROJ_W, D_MODEL, D_MODEL)
    dtypes = (BF16, F32, BF16, F32, BF16, BF16, F32, BF16, F32, F32, BF16, BF16)
    row = lambda w: pl.BlockSpec((tm, w), lambda i: (i, 0))
    heads_spec = pl.BlockSpec((tm, N_KV_HEADS, HEAD_DIM), lambda i: (i, 0, 0))
    consts = (p["norm_mix"], p["wq"], p["wk"], p["wv"], p["wiq"], p["wikw"], p["wpr"], p["wga"], p["wgb"],
              p["q_norm_t"], p["k_norm_t"], p["ik_norm_t"], p["ge"], p["get"], p["ge2"], p["ge2t"])
    return pl.pallas_call(
        _inproj_kernel,
        grid=(n // tm,),
        in_specs=[row(D_MODEL)] + [_const_spec(c.shape) for c in consts],
        out_specs=[heads_spec if i in (1, 3) else row(w) for i, w in enumerate(widths)],
        out_shape=[jax.ShapeDtypeStruct((n, N_KV_HEADS, HEAD_DIM) if i in (1, 3) else (n, w), d)
                   for i, (w, d) in enumerate(zip(widths, dtypes))],
        compiler_params=pltpu.CompilerParams(dimension_semantics=("parallel",), vmem_limit_bytes=VMEM_LIMIT),
        name="in_proj",
    )(x2d, *consts)


def _stack_heads(x, heads, width):
    return jnp.concatenate([x[:, h * width:(h + 1) * width] for h in heads], axis=0)


def _score_keys(d, iw, tq):
    acc = jnp.zeros((tq, d.shape[1]), F32)
    for h in range(N_IDX_HEADS):
        acc = acc + jnp.maximum(d[h * tq:(h + 1) * tq], 0.0) * iw[:, h:h + 1]
    sc = acc * IDX_SCALE
    sc = jnp.where(sc == 0.0, 0.0, sc)
    bits = pltpu.bitcast(sc, jnp.int32)
    return bits ^ ((bits >> 31) & 0x7FFFFFFF)


def _select_threshold(keys_ref, n_kt, *, tq, tk, top):
    lane_pos = lax.broadcasted_iota(jnp.int32, (tq, tk), 1)

    def count(pred):
        def body(kt, c):
            start = pl.multiple_of(kt * tk, tk)
            m = pred(keys_ref[:, pl.ds(start, tk)], start).astype(F32)
            part = m[:, 0:LANES]
            for j in range(1, tk // LANES):
                part = part + m[:, j * LANES:(j + 1) * LANES]
            return c + part
        c = lax.fori_loop(0, n_kt, body, jnp.zeros((tq, LANES), F32))
        return jnp.sum(c, axis=-1, keepdims=True)

    c0 = count(lambda k, s: k >= 0)
    t = jnp.where(c0 >= top, 0, INT_MIN).astype(jnp.int32)
    c_t = jnp.where(c0 >= top, c0, 0)

    def bit_step(i, carry):
        t, c_t = carry
        cand = t | jnp.left_shift(jnp.int32(1), 30 - i)
        c = count(lambda k, s: k >= cand)
        return jnp.where(c >= top, cand, t), jnp.where(c >= top, c, c_t)

    t, c_t = lax.fori_loop(0, 31, bit_step, (t, c_t))
    t = jnp.maximum(t, INT_MIN + 1)

    excess = c_t > top

    @pl.when(jnp.max(excess.astype(jnp.int32)) > 0)
    def _():
        keep = top - count(lambda k, s: k > t)

        def idx_step(i, lim):
            cand = lim | jnp.left_shift(jnp.int32(1), 14 - i)
            c = count(lambda k, s: ((k == t) & (s + lane_pos < cand)))
            return jnp.where(c <= keep, cand, lim)

        lim = lax.fori_loop(0, 15, idx_step, jnp.zeros((tq, 1), jnp.int32))

        def demote(kt, carry):
            start = pl.multiple_of(kt * tk, tk)
            k = keys_ref[:, pl.ds(start, tk)]
            drop = (k == t) & (start + lane_pos >= lim) & excess
            keys_ref[:, pl.ds(start, tk)] = jnp.where(drop, t - 1, k)
            return carry

        lax.fori_loop(0, n_kt, demote, 0)

    return t


def _softmax_step(state, s, bias, pv, tq):
    m_i, l_i, acc = state
    gq, tk = s.shape
    s = (s.reshape(Q_PER_KV, tq, tk) + bias[None]).reshape(gq, tk)
    m_n = jnp.maximum(m_i, jnp.max(s, axis=-1, keepdims=True))
    alpha = jnp.exp(m_i - m_n)
    pm = jnp.exp(s - m_n)
    l_n = alpha * l_i + jnp.sum(pm, axis=-1, keepdims=True)
    return m_n, l_n, alpha * acc + pv(pm.astype(BF16))


def _softmax_init(tq):
    gq = Q_PER_KV * tq
    return tuple((jnp.full((gq, 1), NEG_BIG, F32), jnp.zeros((gq, 1), F32), jnp.zeros((gq, HEAD_DIM), F32))
                 for _ in range(N_KV_HEADS))


def _write_heads(o_ref, res, tq):
    for g in range(N_KV_HEADS):
        _, l_i, acc = res[g]
        og = acc / l_i
        for j in range(Q_PER_KV):
            h = g * Q_PER_KV + j
            o_ref[0, :, h * HEAD_DIM:(h + 1) * HEAD_DIM] = og[j * tq:(j + 1) * tq].astype(o_ref.dtype)


def _dsa_kernel(iq_ref, iw_ref, q_ref, ik_ref, k_ref, v_ref, o_ref, keys_ref, *, tq, tk, q_offset, n_keys, top):
    qi = pl.program_id(1)
    q_base = q_offset + qi * tq
    n_kt = jnp.minimum((q_base + tq + tk - 1) // tk, n_keys // tk)
    q_pos = q_base + lax.broadcasted_iota(jnp.int32, (tq, 1), 0)
    lane_pos = lax.broadcasted_iota(jnp.int32, (tq, tk), 1)

    iq = _stack_heads(iq_ref[0], range(N_IDX_HEADS), D_IDX)
    iw = iw_ref[0]

    def score_tile(kt, carry):
        start = pl.multiple_of(kt * tk, tk)
        key = _score_keys(_dot_nt(iq, ik_ref[0, pl.ds(start, tk), :]), iw, tq)
        keys_ref[:, pl.ds(start, tk)] = jnp.where(start + lane_pos <= q_pos, key, INT_MIN)
        return carry

    lax.fori_loop(0, n_kt, score_tile, 0)

    t = _select_threshold(keys_ref, n_kt, tq=tq, tk=tk, top=top)

    q_all = q_ref[0] * ATTN_SCALE
    qs = [_stack_heads(q_all, range(g * Q_PER_KV, (g + 1) * Q_PER_KV), HEAD_DIM) for g in range(N_KV_HEADS)]

    def attn_tile(kt, carry):
        start = pl.multiple_of(kt * tk, tk)
        bias = jnp.where(keys_ref[:, pl.ds(start, tk)] >= t, 0.0, NEG_BIG)
        k_t = k_ref[0, pl.ds(start, tk), :]
        v_t = v_ref[0, pl.ds(start, tk), :]
        new = []
        for g in range(N_KV_HEADS):
            hs = slice(g * HEAD_DIM, (g + 1) * HEAD_DIM)
            new.append(_softmax_step(carry[g], _dot_nt(qs[g], k_t[:, hs]), bias,
                                     lambda p, hs=hs: _dot(p, v_t[:, hs]), tq))
        return tuple(new)

    _write_heads(o_ref, lax.fori_loop(0, n_kt, attn_tile, _softmax_init(tq)), tq)


def _dsa(iq, iw, q, ik, k, v, *, tq, tk, q_offset, top):
    b, sq, _ = iq.shape
    n_keys = ik.shape[1]
    assert sq % tq == 0 and n_keys % tk == 0 and tk % LANES == 0
    kern = functools.partial(_dsa_kernel, tq=tq, tk=tk, q_offset=q_offset, n_keys=n_keys, top=top)
    return pl.pallas_call(
        kern,
        grid=(b, sq // tq),
        in_specs=[
            pl.BlockSpec((1, tq, IDX_W), lambda bi, qi: (bi, qi, 0)),
            pl.BlockSpec((1, tq, N_IDX_HEADS), lambda bi, qi: (bi, qi, 0)),
            pl.BlockSpec((1, tq, ATTN_W), lambda bi, qi: (bi, qi, 0)),
            pl.BlockSpec((1, n_keys, D_IDX), lambda bi, qi: (bi, 0, 0)),
            pl.BlockSpec((1, n_keys, KV_W), lambda bi, qi: (bi, 0, 0)),
            pl.BlockSpec((1, n_keys, KV_W), lambda bi, qi: (bi, 0, 0)),
        ],
        out_specs=pl.BlockSpec((1, tq, ATTN_W), lambda bi, qi: (bi, qi, 0)),
        out_shape=jax.ShapeDtypeStruct((b, sq, ATTN_W), BF16),
        scratch_shapes=[pltpu.VMEM((tq, n_keys), jnp.int32)],
        compiler_params=pltpu.CompilerParams(dimension_semantics=("parallel", "arbitrary"),
                                             vmem_limit_bytes=VMEM_LIMIT),
        name="dsa",
    )(iq, iw, q, ik, k, v)


def _dsa_paged_kernel(pt_ref, iq_ref, iw_ref, q_ref, ikn_ref, kn_ref, vn_ref, cik_ref, ck_ref, cv_ref, o_ref,
                      ik_buf, k_buf, v_buf, sems, keys_ref, *, tq, n_pages, ppt, tk_sel, top):
    b = pl.program_id(0)
    slot = b % 2
    past = n_pages * PAGE_SIZE
    tk = ppt * PAGE_SIZE

    def page_copies(bi, sl, p):
        page = pt_ref[bi * n_pages + p]
        return (pltpu.make_async_copy(cik_ref.at[page], ik_buf.at[sl, p], sems.at[sl, 0]),
                pltpu.make_async_copy(ck_ref.at[page], k_buf.at[sl, p], sems.at[sl, 1]),
                pltpu.make_async_copy(cv_ref.at[page], v_buf.at[sl, p], sems.at[sl, 2]))

    def fetch(bi, sl):
        def body(p, carry):
            for cp in page_copies(bi, sl, p):
                cp.start()
            return carry
        lax.fori_loop(0, n_pages, body, 0)

    @pl.when(b == 0)
    def _():
        fetch(0, 0)

    @pl.when(b + 1 < pl.num_programs(0))
    def _():
        fetch(b + 1, 1 - slot)

    def wait_page(p, carry):
        for cp in page_copies(b, slot, p):
            cp.wait()
        return carry

    lax.fori_loop(0, n_pages, wait_page, 0)

    def page_cols(buf, i, idx=()):
        return jnp.concatenate([buf[(slot, i * ppt + j) + idx] for j in range(ppt)], axis=1).astype(BF16)

    iq = _stack_heads(iq_ref[0], range(N_IDX_HEADS), D_IDX)
    iw = iw_ref[0]
    row = lax.broadcasted_iota(jnp.int32, (tq, PAGE_SIZE), 0)
    lane = lax.broadcasted_iota(jnp.int32, (tq, PAGE_SIZE), 1)

    def score_tile(i, carry):
        start = pl.multiple_of(i * tk, tk)
        keys_ref[:, pl.ds(start, tk)] = _score_keys(_dot(iq, page_cols(ik_buf, i)), iw, tq)
        return carry

    lax.fori_loop(0, n_pages // ppt, score_tile, 0, unroll=2)
    new_keys = _score_keys(_dot(iq, ikn_ref[0]), iw, tq)
    keys_ref[:, past:past + PAGE_SIZE] = jnp.where(lane <= row, new_keys, INT_MIN)

    t = _select_threshold(keys_ref, (past + PAGE_SIZE) // tk_sel, tq=tq, tk=tk_sel, top=top)

    q_all = q_ref[0] * ATTN_SCALE
    qs = [_stack_heads(q_all, range(g * Q_PER_KV, (g + 1) * Q_PER_KV), HEAD_DIM) for g in range(N_KV_HEADS)]

    def attn_tile(i, carry):
        start = pl.multiple_of(i * tk, tk)
        bias = jnp.where(keys_ref[:, pl.ds(start, tk)] >= t, 0.0, NEG_BIG)
        new = []
        for g in range(N_KV_HEADS):
            v_t = page_cols(v_buf, i, (g,))
            new.append(_softmax_step(carry[g], _dot(qs[g], page_cols(k_buf, i, (g,))), bias,
                                     lambda p, v_t=v_t: _dot_nt(p, v_t), tq))
        return tuple(new)

    res = lax.fori_loop(0, n_pages // ppt, attn_tile, _softmax_init(tq), unroll=2)
    bias = jnp.where(keys_ref[:, past:past + PAGE_SIZE] >= t, 0.0, NEG_BIG)
    res = tuple(_softmax_step(res[g], _dot(qs[g], kn_ref[0, g]), bias,
                              lambda p, g=g: _dot_nt(p, vn_ref[0, g]), tq) for g in range(N_KV_HEADS))
    _write_heads(o_ref, res, tq)


def _dsa_paged(iq, iw, q, ikn, kn, vn, cik_t, ck_t, cv_t, page_table, *, tq, top):
    b = iq.shape[0]
    n_pages = page_table.shape[1]
    ppt = next(d for d in (4, 2, 1) if n_pages % d == 0)
    n_lane_tiles = n_pages + 1
    tk_sel = LANES * next(d for d in range(8, 0, -1) if n_lane_tiles % d == 0)
    kern = functools.partial(_dsa_paged_kernel, tq=tq, n_pages=n_pages, ppt=ppt, tk_sel=tk_sel, top=top)
    per_seq = lambda *blk: pl.BlockSpec((1,) + blk, lambda bi, pt: (bi,) + (0,) * len(blk))
    grid_spec = pltpu.PrefetchScalarGridSpec(
        num_scalar_prefetch=1,
        grid=(b,),
        in_specs=[per_seq(tq, IDX_W), per_seq(tq, N_IDX_HEADS), per_seq(tq, ATTN_W),
                  per_seq(D_IDX, PAGE_SIZE), per_seq(N_KV_HEADS, HEAD_DIM, PAGE_SIZE),
                  per_seq(N_KV_HEADS, HEAD_DIM, PAGE_SIZE),
                  pl.BlockSpec(memory_space=pl.ANY), pl.BlockSpec(memory_space=pl.ANY),
                  pl.BlockSpec(memory_space=pl.ANY)],
        out_specs=per_seq(tq, ATTN_W),
        scratch_shapes=[pltpu.VMEM((2, n_pages, D_IDX, PAGE_SIZE), F32),
                        pltpu.VMEM((2, n_pages, N_KV_HEADS, HEAD_DIM, PAGE_SIZE), F32),
                        pltpu.VMEM((2, n_pages, N_KV_HEADS, HEAD_DIM, PAGE_SIZE), F32),
                        pltpu.SemaphoreType.DMA((2, 3)),
                        pltpu.VMEM((tq, (n_pages + 1) * PAGE_SIZE), jnp.int32)],
    )
    return pl.pallas_call(
        kern,
        grid_spec=grid_spec,
        out_shape=jax.ShapeDtypeStruct((b, tq, ATTN_W), BF16),
        compiler_params=pltpu.CompilerParams(dimension_semantics=("arbitrary",), vmem_limit_bytes=VMEM_LIMIT),
        name="dsa_paged",
    )(page_table.reshape(-1), iq, iw, q, ikn, kn, vn, cik_t, ck_t, cv_t)


def _rwkv_kernel(pr_ref, sh0_ref, z0_ref, mu_ref, w0_ref, wup_ref, a0_ref, aup_ref, gup_ref, kk_ref, ka_ref,
                 rk_ref, lnw_ref, lnb_ref, ge_ref, get_ref,
                 ob_ref, zout_ref,
                 z_scr, prev_scr, r_s, k_s, v_s, a_s, b_s, lw_s, y_s, *, tt, chunk, t_valid):
    ti = pl.program_id(1)
    n_t = pl.num_programs(1)
    nh, hd = N_RWKV_HEADS, RWKV_HEAD

    @pl.when(ti == 0)
    def _():
        z_scr[...] = z0_ref[0]
        prev_scr[...] = sh0_ref[0]

    gsum = lambda z: _group_sum(z, ge_ref[...], get_ref[...])

    pr = pr_ref[0]
    row = lax.broadcasted_iota(jnp.int32, (tt, 1), 0)
    prev = jnp.where(row == 0, prev_scr[...], pltpu.roll(pr, 1, 0))
    prev_scr[...] = pr[tt - 1:tt]
    m = pr + mu_ref[...] * (prev - pr)
    r = m[:, 0:RWKV_W]
    k = m[:, RWKV_W:2 * RWKV_W]
    v = m[:, 2 * RWKV_W:3 * RWKV_W]
    o = 3 * RWKV_W
    wd = m[:, o:o + W_LORA]
    ad = m[:, o + W_LORA:o + W_LORA + A_LORA]
    gd = m[:, o + W_LORA + A_LORA:]

    lora = lambda z, w_ref: _mm(_split(z), _split(w_ref[...]))
    u = -(w0_ref[...] + lora(jnp.tanh(wd), wup_ref))
    softplus = jnp.maximum(u, 0.0) + jnp.log(1.0 + jnp.exp(-jnp.abs(u)))
    lw = -jnp.exp(-softplus - 0.5)
    a = jax.nn.sigmoid(a0_ref[...] + lora(ad, aup_ref))
    gate = lora(jax.nn.sigmoid(gd), gup_ref)
    kk = k * kk_ref[...]
    kk = kk / jnp.maximum(jnp.sqrt(gsum(kk * kk)), 1e-12)
    k2 = k * (1.0 + (a - 1.0) * ka_ref[...])
    bonus = gsum(r * k2 * rk_ref[...])
    av = -kk
    bv = kk * a
    if t_valid < tt:
        ok = (row < t_valid).astype(F32)
        k2, v, av, bv, lw = k2 * ok, v * ok, av * ok, bv * ok, lw * ok
    r_s[...] = r
    k_s[...] = k2
    v_s[...] = v
    a_s[...] = av
    b_s[...] = bv
    lw_s[...] = lw

    c = chunk
    ri = lax.broadcasted_iota(jnp.int32, (c, c), 0)
    ci = lax.broadcasted_iota(jnp.int32, (c, c), 1)
    tri_incl = (ci <= ri)
    tri_strict = (ci < ri)
    ltri = tri_incl.astype(BF16)
    eye_h = (lax.broadcasted_iota(jnp.int32, (hd, hd), 0) == lax.broadcasted_iota(jnp.int32, (hd, hd), 1))
    n_double = max(int(np.ceil(np.log2(c))), 1)
    heads = range(nh)
    hsl = [slice(h * hd, (h + 1) * hd) for h in heads]

    def chunk_body(ci_, carry):
        s0 = pl.multiple_of(ci_ * c, c)
        sl = pl.ds(s0, c)
        lwc = lw_s[sl, :]
        cum = sum(_dot(ltri, part) for part in _split(lwc, 3))
        cum_end = cum[c - 1:c, :]
        g_end = jnp.exp(cum_end)
        at = a_s[sl, :] * jnp.exp(cum - lwc)
        rt = r_s[sl, :] * jnp.exp(cum)
        g_inv = jnp.exp(-cum)
        g_tail = jnp.exp(cum_end - cum)
        bt = b_s[sl, :] * g_inv
        kt = k_s[sl, :] * g_inv
        bc = b_s[sl, :] * g_tail
        kc = k_s[sl, :] * g_tail
        vc = v_s[sl, :]

        left = [_split(jnp.concatenate([at[:, s], rt[:, s]], axis=0)) for s in hsl]
        right = [_split(jnp.concatenate([bt[:, s], kt[:, s]], axis=0)) for s in hsl]
        amat = [_mm(left[h], right[h], _dot_nt) for h in heads]
        a_ab = [jnp.where(tri_strict, amat[h][:c, :c], 0.0) for h in heads]
        a_ak = [_split(jnp.where(tri_strict, amat[h][:c, c:], 0.0)) for h in heads]
        a_rb = [_split(jnp.where(tri_incl, amat[h][c:, :c], 0.0)) for h in heads]
        a_rk = [_split(jnp.where(tri_incl, amat[h][c:, c:], 0.0)) for h in heads]
        vh = [_split(vc[:, s]) for s in hsl]
        akv = [_mm(a_ak[h], vh[h]) for h in heads]
        uu = [jnp.concatenate([at[:, hsl[h]], akv[h]], axis=1) for h in heads]
        pw = a_ab
        for step in range(n_double):
            pws = [_split(z) for z in pw]
            uus = [_split(z) for z in uu]
            uu = [uu[h] + _mm(pws[h], uus[h]) for h in heads]
            if step + 1 < n_double:
                pw = [_mm(pws[h], pws[h]) for h in heads]
        uus = [_split(z) for z in uu]
        x1 = [_mm(a_rb[h], uus[h]) for h in heads]
        x2 = [_mm(a_rk[h], vh[h]) for h in heads]
        mn = [_mm(_split(bc[:, hsl[h]]), uus[h], _dot_tn) for h in heads]
        nk = [_mm(_split(kc[:, hsl[h]]), vh[h], _dot_tn) for h in heads]
        pm = []
        for h in heads:
            p2 = rt[:, hsl[h]] + x1[h][:, :hd]
            mh = jnp.where(eye_h, g_end[:, hsl[h]], 0.0) + mn[h][:, :hd]
            pm.append(_split(jnp.concatenate([p2, mh], axis=0)))
        res = [_mm(pm[h], _split(z_scr[h])) for h in heads]
        for h in heads:
            y_s[sl, hsl[h]] = res[h][:c] + x1[h][:, hd:] + x2[h]
            z_scr[h] = res[h][c:] + mn[h][:, hd:] + nk[h]
        return carry

    lax.fori_loop(0, tt // c, chunk_body, 0)

    y = y_s[...]
    mean = gsum(y) * (1.0 / hd)
    dlt = y - mean
    var = gsum(dlt * dlt) * (1.0 / hd)
    yn = dlt * lax.rsqrt(var + LNX_EPS) * lnw_ref[...] + lnb_ref[...]
    ob_ref[0] = ((yn + bonus * v_s[...]) * gate).astype(ob_ref.dtype)

    @pl.when(ti == n_t - 1)
    def _():
        zout_ref[0] = z_scr[...]


def _rwkv(pr, shift0, z0, p, *, tt, chunk, t_valid):
    b, t, _ = pr.shape
    assert t % tt == 0 and tt % chunk == 0
    consts = (p["shift_mu"], p["w0"], p["w_lora_up"], p["a0"], p["a_lora_up"], p["g_lora_up"], p["k_k"], p["k_a"],
              p["r_k"], p["ln_x_w"], p["ln_x_b"], p["ge"], p["get"])
    kern = functools.partial(_rwkv_kernel, tt=tt, chunk=chunk, t_valid=t_valid)
    wide = lambda: pltpu.VMEM((tt, RWKV_W), F32)
    return pl.pallas_call(
        kern,
        grid=(b, t // tt),
        in_specs=[
            pl.BlockSpec((1, tt, RWKV_PROJ_W), lambda bi, ti: (bi, ti, 0)),
            pl.BlockSpec((1, 1, RWKV_PROJ_W), lambda bi, ti: (bi, 0, 0)),
            pl.BlockSpec((1, N_RWKV_HEADS, RWKV_HEAD, RWKV_HEAD), lambda bi, ti: (bi, 0, 0, 0)),
        ] + [_const_spec(c.shape) for c in consts],
        out_specs=[
            pl.BlockSpec((1, tt, RWKV_W), lambda bi, ti: (bi, ti, 0)),
            pl.BlockSpec((1, N_RWKV_HEADS, RWKV_HEAD, RWKV_HEAD), lambda bi, ti: (bi, 0, 0, 0)),
        ],
        out_shape=[jax.ShapeDtypeStruct((b, t, RWKV_W), BF16),
                   jax.ShapeDtypeStruct((b, N_RWKV_HEADS, RWKV_HEAD, RWKV_HEAD), F32)],
        scratch_shapes=[pltpu.VMEM((N_RWKV_HEADS, RWKV_HEAD, RWKV_HEAD), F32),
                        pltpu.VMEM((1, RWKV_PROJ_W), F32),
                        wide(), wide(), wide(), wide(), wide(), wide(), wide()],
        compiler_params=pltpu.CompilerParams(dimension_semantics=("parallel", "arbitrary"),
                                             vmem_limit_bytes=VMEM_LIMIT),
        name="rwkv",
    )(pr, shift0, z0, *consts)


def _merge_kernel(x_ref, oa_ref, ob_ref, ga_ref, gb_ref, wa_ref, wb_ref, wo_ref, nf_ref, rwt_ref, rb_ref,
                  h_out, hn_out, idx_out, gate_out):
    ma = _dot(oa_ref[...], wa_ref[...])
    mb = _dot(ob_ref[...], wb_ref[...])
    mm = ga_ref[...].astype(F32) * ma + gb_ref[...].astype(F32) * mb
    h = x_ref[...] + _dot(mm.astype(BF16), wo_ref[...])
    h_out[...] = h
    ms = jnp.mean(h * h, axis=-1, keepdims=True)
    hn = h * lax.rsqrt(ms + NORM_EPS) * nf_ref[...]
    hn_out[...] = hn.astype(hn_out.dtype)
    logits = _mm(_split(rwt_ref[...]), _split(hn), _dot_nt) + rb_ref[...]
    tm = logits.shape[1]
    expert = lax.broadcasted_iota(jnp.int32, logits.shape, 0)
    vals, idxs = [], []
    for _ in range(TOP_K):
        mx = jnp.max(logits, axis=0, keepdims=True)
        ix = jnp.min(jnp.where(logits == mx, expert, N_EXPERTS), axis=0, keepdims=True)
        vals.append(mx)
        idxs.append(ix)
        logits = jnp.where(expert == ix, -jnp.inf, logits)
    es = [jnp.exp(v - vals[0]) for v in vals]
    den = es[0] + es[1] + es[2] + es[3]
    pad = 8 - TOP_K
    idx_out[...] = jnp.concatenate(idxs + [jnp.zeros((pad, tm), jnp.int32)], axis=0)
    gate_out[...] = jnp.concatenate([e / den for e in es] + [jnp.zeros((pad, tm), F32)], axis=0)


def _merge(x2d, oa, ob, ga, gb, p, tm):
    n = x2d.shape[0]
    row = lambda w: pl.BlockSpec((tm, w), lambda i: (i, 0))
    consts = (p["w_proj_a"], p["w_proj_b"], p["w_out"], p["norm_ffn"], p["router_w"], p["router_b"])
    return pl.pallas_call(
        _merge_kernel,
        grid=(n // tm,),
        in_specs=[row(D_MODEL), row(ATTN_W), row(RWKV_W), row(D_MODEL), row(D_MODEL)]
        + [_const_spec(c.shape) for c in consts],
        out_specs=[row(D_MODEL), row(D_MODEL)] + [pl.BlockSpec((8, tm), lambda i: (0, i))] * 2,
        out_shape=[jax.ShapeDtypeStruct((n, D_MODEL), F32), jax.ShapeDtypeStruct((n, D_MODEL), BF16),
                   jax.ShapeDtypeStruct((8, n), jnp.int32), jax.ShapeDtypeStruct((8, n), F32)],
        compiler_params=pltpu.CompilerParams(dimension_semantics=("parallel",), vmem_limit_bytes=VMEM_LIMIT),
        name="merge",
    )(x2d, oa, ob, ga, gb, *consts)


def _moe_kernel(be_ref, nb_ref, x_ref, wgu_ref, bgu_ref, wd_ref, bd_ref, o_ref, wgu_s, wd_s):
    i = pl.program_id(0)
    used = i < nb_ref[0]

    @pl.when(used & ((i == 0) | (be_ref[i] != be_ref[jnp.maximum(i - 1, 0)])))
    def _():
        wgu_s[...] = wgu_ref[0].astype(wgu_s.dtype)
        wd_s[...] = wd_ref[0].astype(wd_s.dtype)

    @pl.when(used)
    def _():
        hcat = _dot(x_ref[...], wgu_s[...]) + bgu_ref[0]
        glu = jnp.minimum(hcat[:, :D_FF], SWIGLU_LIMIT)
        lin = jnp.clip(hcat[:, D_FF:], -SWIGLU_LIMIT, SWIGLU_LIMIT)
        act = glu * jax.nn.sigmoid(SWIGLU_ALPHA * glu) * (lin + 1.0)
        o_ref[...] = _dot(act.astype(BF16), wd_s[...]) + bd_ref[0]

    @pl.when(jnp.logical_not(used))
    def _():
        o_ref[...] = jnp.zeros(o_ref.shape, o_ref.dtype)


def _moe_rows(xg, block_exp, n_used, p, bm):
    rows = xg.shape[0]
    grid_spec = pltpu.PrefetchScalarGridSpec(
        num_scalar_prefetch=2,
        grid=(rows // bm,),
        in_specs=[
            pl.BlockSpec((bm, D_MODEL), lambda i, be, nb: (i, 0)),
            pl.BlockSpec((1, D_MODEL, 2 * D_FF), lambda i, be, nb: (be[i], 0, 0)),
            pl.BlockSpec((1, 1, 2 * D_FF), lambda i, be, nb: (be[i], 0, 0)),
            pl.BlockSpec((1, D_FF, D_MODEL), lambda i, be, nb: (be[i], 0, 0)),
            pl.BlockSpec((1, 1, D_MODEL), lambda i, be, nb: (be[i], 0, 0)),
        ],
        out_specs=pl.BlockSpec((bm, D_MODEL), lambda i, be, nb: (i, 0)),
        scratch_shapes=[pltpu.VMEM((D_MODEL, 2 * D_FF), BF16), pltpu.VMEM((D_FF, D_MODEL), BF16)],
    )
    return pl.pallas_call(
        _moe_kernel,
        grid_spec=grid_spec,
        out_shape=jax.ShapeDtypeStruct((rows, D_MODEL), F32),
        compiler_params=pltpu.CompilerParams(dimension_semantics=("arbitrary",), vmem_limit_bytes=VMEM_LIMIT),
        name="moe",
    )(block_exp, n_used, xg, p["w_gate_up"], p["b_gate_up"], p["w_down"], p["b_down"])


def _moe(hn, top_idx, gate, p, bm):
    n = hn.shape[0]
    nk = n * TOP_K
    e_flat = top_idx.reshape(-1)
    onehot = (e_flat[:, None] == jnp.arange(N_EXPERTS, dtype=jnp.int32)[None, :]).astype(jnp.int32)
    csum = jnp.cumsum(onehot, axis=0)
    counts = csum[-1]
    rank = jnp.take_along_axis(csum, e_flat[:, None], axis=1)[:, 0] - 1
    padded = (counts + bm - 1) // bm * bm
    pends = jnp.cumsum(padded)
    pstarts = pends - padded
    dest = pstarts[e_flat] + rank
    n_blocks = -(-nk // bm) + N_EXPERTS
    rows = n_blocks * bm
    tok_flat = jnp.arange(nk, dtype=jnp.int32) % n
    row_tok = jnp.zeros((rows,), jnp.int32).at[dest].set(tok_flat, unique_indices=True)
    block_start = jnp.arange(n_blocks, dtype=jnp.int32) * bm
    block_exp = jnp.minimum(jnp.sum((pends[None, :] <= block_start[:, None]).astype(jnp.int32), axis=1),
                            N_EXPERTS - 1)
    n_used = (pends[-1:] // bm).astype(jnp.int32)
    out = _moe_rows(hn[row_tok], block_exp, n_used, p, bm)
    return (out[dest.reshape(TOP_K, n)] * gate[:, :, None]).sum(axis=0)


def _prep_params(norm_mix, w_in, q_norm, k_norm, idx_k_norm, shift_mu, w0, w_lora_up, a0, a_lora_up, g_lora_up, k_k,
                 k_a, r_k, ln_x_w, ln_x_b, w_proj_a, w_proj_b, w_out, norm_ffn, router_w, router_b, w_gate_up,
                 b_gate_up, w_down, b_down):
    splits = (ATTN_W, KV_W, KV_W, IDX_W, D_IDX, N_IDX_HEADS, RWKV_PROJ_W, D_MODEL, D_MODEL)
    cuts = np.cumsum(splits)[:-1].tolist()
    wq, wk, wv, wiq, wik, wiw, wpr, wga, wgb = jnp.split(w_in.astype(BF16), cuts, axis=-1)
    wikw = jnp.concatenate([wik, wiw, jnp.zeros((D_MODEL, LANES - D_IDX - N_IDX_HEADS), BF16)], axis=-1)
    row = lambda z: z.reshape(1, -1).astype(F32)
    ge = _group_indicator(RWKV_W, RWKV_HEAD).astype(BF16)
    ge2 = _group_indicator(KV_W, HEAD_DIM).astype(BF16)
    return dict(
        norm_mix=row(norm_mix), wq=wq, wk=wk, wv=wv, wiq=wiq, wikw=wikw, wpr=wpr, wga=wga, wgb=wgb,
        q_norm_t=row(jnp.tile(q_norm, N_Q_HEADS)), k_norm_t=row(jnp.tile(k_norm, N_KV_HEADS)),
        ik_norm_t=row(idx_k_norm),
        ge=ge, get=ge.T, ge2=ge2, ge2t=ge2.T,
        shift_mu=row(shift_mu), w0=row(w0), w_lora_up=w_lora_up, a0=row(a0), a_lora_up=a_lora_up,
        g_lora_up=g_lora_up, k_k=row(k_k), k_a=row(k_a), r_k=row(r_k), ln_x_w=row(ln_x_w), ln_x_b=row(ln_x_b),
        w_proj_a=w_proj_a.astype(BF16), w_proj_b=w_proj_b.astype(BF16), w_out=w_out.astype(BF16),
        norm_ffn=row(norm_ffn), router_w=router_w.T, router_b=router_b.reshape(-1, 1),
        w_gate_up=w_gate_up, b_gate_up=b_gate_up[:, None, :], w_down=w_down, b_down=b_down[:, None, :],
    )


def _pad_axis(z, axis, size):
    if z.shape[axis] == size:
        return z
    pad = [(0, 0)] * z.ndim
    pad[axis] = (0, size - z.shape[axis])
    return jnp.pad(z, pad)


def _group(x, p, *, tm):
    b, t, _ = x.shape
    names = ("q", "k", "kb", "v", "vb", "iq", "ik", "ikb", "iw", "pr", "ga", "gb")
    g = dict(zip(names, _in_proj(x.reshape(b * t, D_MODEL), p, tm)), b=b, t=t)
    g["pr"] = g["pr"].reshape(b, t, RWKV_PROJ_W)
    return g


def _attend(g, ik_all, k_all, v_all, *, tq, tk, q_offset, top):
    b, t = g["b"], g["t"]
    tp = -(-t // tq) * tq
    n_keys = -(-ik_all.shape[1] // tk) * tk
    seq = lambda z: _pad_axis(z.reshape(b, t, -1), 1, tp)
    keys = lambda z: _pad_axis(z, 1, n_keys)
    oa = _dsa(seq(g["iq"]), seq(g["iw"]), seq(g["q"]), keys(ik_all), keys(k_all), keys(v_all),
              tq=tq, tk=tk, q_offset=q_offset, top=top)
    return oa[:, :t].reshape(b * t, ATTN_W)


def _mix(g, shift0, wkv0, p, *, tt, chunk):
    b, t = g["b"], g["t"]
    tp = -(-t // tt) * tt
    ob, z = _rwkv(_pad_axis(g["pr"], 1, tp), shift0[:, None, :], jnp.swapaxes(wkv0, -1, -2), p,
                  tt=tt, chunk=chunk, t_valid=min(t, tt) if tp != t else tt)
    return ob[:, :t].reshape(b * t, RWKV_W), jnp.swapaxes(z, -1, -2)


def kernel(x_prompt, x_sample, cache_k, cache_v, cache_idx_k, page_table, state_wkv, state_shift, norm_mix, w_in, q_norm, k_norm, idx_k_norm, shift_mu, w0, w_lora_up, a0, a_lora_up, g_lora_up, k_k, k_a, r_k, ln_x_w, ln_x_b, w_proj_a, w_proj_b, w_out, norm_ffn, router_w, router_b, w_gate_up, b_gate_up, w_down, b_down):
    depth = norm_mix.shape[0]
    assert depth == 1
    params = (norm_mix, w_in, q_norm, k_norm, idx_k_norm, shift_mu, w0, w_lora_up, a0, a_lora_up, g_lora_up, k_k, k_a,
              r_k, ln_x_w, ln_x_b, w_proj_a, w_proj_b, w_out, norm_ffn, router_w, router_b, w_gate_up, b_gate_up,
              w_down, b_down)
    p = _prep_params(*[z[0] for z in params])
    bp, sp, _ = x_prompt.shape
    bs, ts, _ = x_sample.shape
    n_p, n_s = bp * sp, bs * ts
    past = page_table.shape[1] * PAGE_SIZE

    gp = _group(x_prompt, p, tm=min(256, n_p))
    k_p = gp["k"].reshape(bp, sp, N_KV_HEADS, HEAD_DIM)
    v_p = gp["v"].reshape(bp, sp, N_KV_HEADS, HEAD_DIM)
    ik_p = gp["ik"].reshape(bp, sp, D_IDX)
    oa_p = _attend(gp, gp["ikb"].reshape(bp, sp, D_IDX), gp["kb"].reshape(bp, sp, KV_W),
                   gp["vb"].reshape(bp, sp, KV_W), tq=min(128, sp), tk=min(512, sp), q_offset=0,
                   top=min(TOPK_MAX, sp // 4))
    ob_p, wkv_p = _mix(gp, jnp.zeros((bp, RWKV_PROJ_W), F32),
                       jnp.zeros((bp, N_RWKV_HEADS, RWKV_HEAD, RWKV_HEAD), F32), p,
                       tt=min(256, sp), chunk=min(64, sp))

    gs = _group(x_sample, p, tm=min(256, n_s))
    k_s = gs["k"].reshape(bs, ts, N_KV_HEADS, HEAD_DIM)
    v_s = gs["v"].reshape(bs, ts, N_KV_HEADS, HEAD_DIM)
    ik_s = gs["ik"].reshape(bs, ts, D_IDX)
    tq_s = 16
    seq = lambda z: _pad_axis(z.reshape(bs, ts, -1), 1, tq_s)
    new_t = lambda z, *hd: _pad_axis(jnp.moveaxis(z.reshape((bs, ts) + hd), 1, -1), len(hd) + 1, PAGE_SIZE)
    oa_s = _dsa_paged(seq(gs["iq"]), seq(gs["iw"]), seq(gs["q"]),
                      new_t(gs["ikb"], D_IDX), new_t(gs["kb"], N_KV_HEADS, HEAD_DIM),
                      new_t(gs["vb"], N_KV_HEADS, HEAD_DIM),
                      jnp.transpose(cache_idx_k[0], (0, 2, 1)), jnp.transpose(cache_k[0], (0, 2, 3, 1)),
                      jnp.transpose(cache_v[0], (0, 2, 3, 1)), page_table,
                      tq=tq_s, top=min(TOPK_MAX, (past + ts) // 4))[:, :ts].reshape(n_s, ATTN_W)
    ob_s, wkv_s = _mix(gs, state_shift[0], state_wkv[0], p, tt=8, chunk=8)

    h_p, hn_p, idx_p, gate_p = _merge(x_prompt.reshape(n_p, D_MODEL), oa_p, ob_p, gp["ga"], gp["gb"], p,
                                      tm=min(256, n_p))
    h_s, hn_s, idx_s, gate_s = _merge(x_sample.reshape(n_s, D_MODEL), oa_s, ob_s, gs["ga"], gs["gb"], p,
                                      tm=min(256, n_s))
    hn = jnp.concatenate([hn_p, hn_s], axis=0)
    top_idx = jnp.concatenate([idx_p[:TOP_K], idx_s[:TOP_K]], axis=1)
    gate = jnp.concatenate([gate_p[:TOP_K], gate_s[:TOP_K]], axis=1)
    f = _moe(hn, top_idx, gate, p, bm=512)
    y_p = (h_p + f[:n_p]).reshape(bp, sp, D_MODEL)
    y_s = (h_s + f[n_p:]).reshape(bs, ts, D_MODEL)

    st = lambda z: z[None]
    return (y_p, y_s, st(k_p), st(v_p), st(ik_p), st(wkv_p), st(gp["pr"][:, -1]),
            st(k_s), st(v_s), st(ik_s), st(wkv_s), st(gs["pr"][:, -1]))
```

```python
import functools

import jax
import jax.numpy as jnp
import numpy as np
from jax import lax
from jax.experimental import pallas as pl
from jax.experimental.pallas import tpu as pltpu

D_MODEL = 1024
PAGE_SIZE = 128
HEAD_DIM = 64
N_Q_HEADS = 8
N_KV_HEADS = 2
Q_PER_KV = N_Q_HEADS // N_KV_HEADS
ATTN_W = N_Q_HEADS * HEAD_DIM
KV_W = N_KV_HEADS * HEAD_DIM
ATTN_SCALE = HEAD_DIM ** -0.5
N_IDX_HEADS = 8
D_IDX = 64
IDX_W = N_IDX_HEADS * D_IDX
IDX_SCALE = (N_IDX_HEADS * D_IDX) ** -0.5
TOPK_MAX = 256
RWKV_HEAD = 64
N_RWKV_HEADS = 8
RWKV_W = N_RWKV_HEADS * RWKV_HEAD
W_LORA = 64
A_LORA = 64
G_LORA = 128
RWKV_PROJ_W = 3 * RWKV_W + W_LORA + A_LORA + G_LORA
LNX_EPS = 64e-5
N_EXPERTS = 32
TOP_K = 4
D_FF = 1024
SWIGLU_LIMIT = 7.0
SWIGLU_ALPHA = 1.702
NORM_EPS = 1e-6

LANES = 128
VMEM_LIMIT = 56 * 1024 * 1024
INT_MIN = -(2 ** 31)
NEG_BIG = -1e30
F32 = jnp.float32
BF16 = jnp.bfloat16
HI = lax.Precision.HIGHEST


def _dot(a, b, precision=None):
    return jnp.dot(a, b, preferred_element_type=F32, precision=precision)


def _dot_nt(a, b, precision=None):
    return lax.dot_general(a, b, (((1,), (1,)), ((), ())), preferred_element_type=F32, precision=precision)


def _dot_tn(a, b, precision=None):
    return lax.dot_general(a, b, (((0,), (0,)), ((), ())), preferred_element_type=F32, precision=precision)


def _group_indicator(width, group):
    r = np.arange(width) // group
    return jnp.asarray((r[:, None] == np.arange(LANES)[None, :]).astype(np.float32))


def _split(x, terms=2):
    parts = []
    for _ in range(terms - 1):
        hi = x.astype(BF16)
        parts.append(hi)
        x = x - hi.astype(F32)
    parts.append(x.astype(BF16))
    return parts


def _mm(a, b, dot=None):
    dot = dot or _dot
    return dot(a[0], b[0]) + (dot(a[0], b[1]) + dot(a[1], b[0]))


def _group_sum(x, ge, get):
    s = sum(_dot(part, ge) for part in _split(x))
    return sum(_dot(part, get) for part in _split(s))


def _const_spec(shape):
    nd = len(shape)
    return pl.BlockSpec(shape, lambda *_: (0,) * nd)


def _inproj_kernel(x_ref, g_ref, wq_ref, wk_ref, wv_ref, wiq_ref, wikw_ref, wpr_ref, wga_ref, wgb_ref,
                   qn_ref, kn_ref, ikn_ref, ge_ref, get_ref, ge2_ref, ge2t_ref,
                   q_out, k_out, kb_out, v_out, vb_out, iq_out, ik_out, ikb_out, iw_out, pr_out, ga_out, gb_out):
    x = x_ref[...]
    ms = jnp.mean(x * x, axis=-1, keepdims=True)
    xn = (x * lax.rsqrt(ms + NORM_EPS) * g_ref[...]).astype(BF16)

    q = _dot(xn, wq_ref[...])
    qs = _group_sum(q * q, ge_ref[...], get_ref[...]) * (1.0 / HEAD_DIM)
    q_out[...] = (q * lax.rsqrt(qs + NORM_EPS) * qn_ref[...]).astype(q_out.dtype)

    k = _dot(xn, wk_ref[...])
    ks = _group_sum(k * k, ge2_ref[...], ge2t_ref[...]) * (1.0 / HEAD_DIM)
    kn = k * lax.rsqrt(ks + NORM_EPS) * kn_ref[...]
    for g in range(N_KV_HEADS):
        k_out[:, g, :] = kn[:, g * HEAD_DIM:(g + 1) * HEAD_DIM]
    kb_out[...] = kn.astype(kb_out.dtype)

    v = _dot(xn, wv_ref[...])
    for g in range(N_KV_HEADS):
        v_out[:, g, :] = v[:, g * HEAD_DIM:(g + 1) * HEAD_DIM]
    vb_out[...] = v.astype(vb_out.dtype)
    iq_out[...] = _dot(xn, wiq_ref[...]).astype(iq_out.dtype)

    ikw = _dot(xn, wikw_ref[...])
    lane = lax.broadcasted_iota(jnp.int32, ikw.shape, 1)
    iks = jnp.sum(jnp.where(lane < D_IDX, ikw * ikw, 0.0), axis=-1, keepdims=True) * (1.0 / D_IDX)
    ikn = ikw[:, :D_IDX] * lax.rsqrt(iks + NORM_EPS) * ikn_ref[...]
    ik_out[...] = ikn
    ikb_out[...] = ikn.astype(ikb_out.dtype)
    iw_out[...] = ikw[:, D_IDX:D_IDX + N_IDX_HEADS]

    pr_out[...] = _dot(xn, wpr_ref[...])
    ga_out[...] = jax.nn.sigmoid(_dot(xn, wga_ref[...])).astype(ga_out.dtype)
    gb_out[...] = jax.nn.sigmoid(_dot(xn, wgb_ref[...])).astype(gb_out.dtype)


def _in_proj(x2d, p, tm):
    n = x2d.shape[0]
    widths = (ATTN_W, KV_W, KV_W, KV_W, KV_W, IDX_W, D_IDX, D_IDX, N_IDX_HEADS, RWKV_PROJ_W, D_MODEL, D_MODEL)
    dtypes = (BF16, F32, BF16, F32, BF16, BF16, F32, BF16, F32, F32, BF16, BF16)
    row = lambda w: pl.BlockSpec((tm, w), lambda i: (i, 0))
    heads_spec = pl.BlockSpec((tm, N_KV_HEADS, HEAD_DIM), lambda i: (i, 0, 0))
    consts = (p["norm_mix"], p["wq"], p["wk"], p["wv"], p["wiq"], p["wikw"], p["wpr"], p["wga"], p["wgb"],
              p["q_norm_t"], p["k_norm_t"], p["ik_norm_t"], p["ge"], p["get"], p["ge2"], p["ge2t"])
    return pl.pallas_call(
        _inproj_kernel,
        grid=(n // tm,),
        in_specs=[row(D_MODEL)] + [_const_spec(c.shape) for c in consts],
        out_specs=[heads_spec if i in (1, 3) else row(w) for i, w in enumerate(widths)],
        out_shape=[jax.ShapeDtypeStruct((n, N_KV_HEADS, HEAD_DIM) if i in (1, 3) else (n, w), d)
                   for i, (w, d) in enumerate(zip(widths, dtypes))],
        compiler_params=pltpu.CompilerParams(dimension_semantics=("parallel",), vmem_limit_bytes=VMEM_LIMIT),
        name="in_proj",
    )(x2d, *consts)


def _stack_heads(x, heads, width):
    return jnp.concatenate([x[:, h * width:(h + 1) * width] for h in heads], axis=0)


def _score_keys(d, iw, tq):
    acc = jnp.zeros((tq, d.shape[1]), F32)
    for h in range(N_IDX_HEADS):
        acc = acc + jnp.maximum(d[h * tq:(h + 1) * tq], 0.0) * iw[:, h:h + 1]
    sc = acc * IDX_SCALE
    sc = jnp.where(sc == 0.0, 0.0, sc)
    bits = pltpu.bitcast(sc, jnp.int32)
    return bits ^ ((bits >> 31) & 0x7FFFFFFF)


def _select_threshold(keys_ref, n_kt, *, tq, tk, top, bounds=None):
    lane_pos = lax.broadcasted_iota(jnp.int32, (tq, tk), 1)

    def count(pred):
        def body(kt, c):
            start = pl.multiple_of(kt * tk, tk)
            m = pred(keys_ref[:, pl.ds(start, tk)], start).astype(F32)
            part = m[:, 0:LANES]
            for j in range(1, tk // LANES):
                part = part + m[:, j * LANES:(j + 1) * LANES]
            return c + part
        c = lax.fori_loop(0, n_kt, body, jnp.zeros((tq, LANES), F32))
        return jnp.sum(c, axis=-1, keepdims=True)

    if bounds is None:
        first_bit = 0
        t = jnp.full((tq, 1), INT_MIN, jnp.int32)
        c_t = jnp.zeros((tq, 1), F32)
    else:
        lo_b, hi_b = bounds
        first_bit = jnp.min(lax.clz(lo_b ^ hi_b))
        shift = jnp.minimum(32 - first_bit, 31)
        mask = jnp.where(first_bit == 0, 0, jnp.left_shift(jnp.int32(-1), shift))
        t = ((hi_b ^ INT_MIN) & mask) ^ INT_MIN
        c_t = jnp.where(t == INT_MIN, 0.0, count(lambda k, s: k >= t))

    def bit_step(i, carry):
        t, c_t = carry
        cand = t + jnp.left_shift(jnp.int32(1), 31 - i)
        c = count(lambda k, s: k >= cand)
        return jnp.where(c >= top, cand, t), jnp.where(c >= top, c, c_t)

    t, c_t = lax.fori_loop(first_bit, 32, bit_step, (t, c_t))
    t = jnp.maximum(t, INT_MIN + 1)

    excess = c_t > top

    @pl.when(jnp.max(excess.astype(jnp.int32)) > 0)
    def _():
        keep = top - count(lambda k, s: k > t)

        def idx_step(i, lim):
            cand = lim | jnp.left_shift(jnp.int32(1), 14 - i)
            c = count(lambda k, s: ((k == t) & (s + lane_pos < cand)))
            return jnp.where(c <= keep, cand, lim)

        lim = lax.fori_loop(0, 15, idx_step, jnp.zeros((tq, 1), jnp.int32))

        def demote(kt, carry):
            start = pl.multiple_of(kt * tk, tk)
            k = keys_ref[:, pl.ds(start, tk)]
            drop = (k == t) & (start + lane_pos >= lim) & excess
            keys_ref[:, pl.ds(start, tk)] = jnp.where(drop, t - 1, k)
            return carry

        lax.fori_loop(0, n_kt, demote, 0)

    return t


def _softmax_step(state, s, bias, pv, tq):
    m_i, l_i, acc = state
    gq, tk = s.shape
    s = (s.reshape(Q_PER_KV, tq, tk) + bias[None]).reshape(gq, tk)
    m_n = jnp.maximum(m_i, jnp.max(s, axis=-1, keepdims=True))
    alpha = jnp.exp(m_i - m_n)
    pm = jnp.exp(s - m_n)
    l_n = alpha * l_i + jnp.sum(pm, axis=-1, keepdims=True)
    return m_n, l_n, alpha * acc + pv(pm.astype(BF16))


def _softmax_init(tq):
    gq = Q_PER_KV * tq
    return tuple((jnp.full((gq, 1), NEG_BIG, F32), jnp.zeros((gq, 1), F32), jnp.zeros((gq, HEAD_DIM), F32))
                 for _ in range(N_KV_HEADS))


def _write_heads(o_ref, res, tq):
    for g in range(N_KV_HEADS):
        _, l_i, acc = res[g]
        og = acc / l_i
        for j in range(Q_PER_KV):
            h = g * Q_PER_KV + j
            o_ref[0, :, h * HEAD_DIM:(h + 1) * HEAD_DIM] = og[j * tq:(j + 1) * tq].astype(o_ref.dtype)


def _dsa_kernel(iq_ref, iw_ref, q_ref, ik_ref, k_ref, v_ref, o_ref, keys_ref, *, tq, tk, q_offset, n_keys, top):
    qi = pl.program_id(1)
    q_base = q_offset + qi * tq
    n_kt = jnp.minimum((q_base + tq + tk - 1) // tk, n_keys // tk)
    q_pos = q_base + lax.broadcasted_iota(jnp.int32, (tq, 1), 0)
    lane_pos = lax.broadcasted_iota(jnp.int32, (tq, tk), 1)

    iq = _stack_heads(iq_ref[0], range(N_IDX_HEADS), D_IDX)
    iw = iw_ref[0]

    n_lt = tk // LANES
    halves = ([range(n_lt // 2), range(n_lt // 2, n_lt)] if n_lt >= 2 and top <= 2 * LANES else [])

    def score_tile(kt, carry):
        start = pl.multiple_of(kt * tk, tk)
        key = _score_keys(_dot_nt(iq, ik_ref[0, pl.ds(start, tk), :]), iw, tq)
        key = jnp.where(start + lane_pos <= q_pos, key, INT_MIN)
        keys_ref[:, pl.ds(start, tk)] = key
        new = []
        for slot_max, lane_tiles in zip(carry, halves):
            for j in lane_tiles:
                slot_max = jnp.maximum(slot_max, key[:, j * LANES:(j + 1) * LANES])
            new.append(slot_max)
        return tuple(new)

    slots = lax.fori_loop(0, n_kt, score_tile, tuple(jnp.full((tq, LANES), INT_MIN, jnp.int32) for _ in halves))
    bounds = None
    if halves:
        bounds = (jnp.min(jnp.minimum(slots[0], slots[1]), axis=-1, keepdims=True),
                  jnp.max(jnp.maximum(slots[0], slots[1]), axis=-1, keepdims=True))

    t = _select_threshold(keys_ref, n_kt, tq=tq, tk=tk, top=top, bounds=bounds)

    q_all = q_ref[0] * ATTN_SCALE
    qs = [_stack_heads(q_all, range(g * Q_PER_KV, (g + 1) * Q_PER_KV), HEAD_DIM) for g in range(N_KV_HEADS)]

    def attn_tile(kt, carry):
        start = pl.multiple_of(kt * tk, tk)
        bias = jnp.where(keys_ref[:, pl.ds(start, tk)] >= t, 0.0, NEG_BIG)
        k_t = k_ref[0, pl.ds(start, tk), :]
        v_t = v_ref[0, pl.ds(start, tk), :]
        new = []
        for g in range(N_KV_HEADS):
            hs = slice(g * HEAD_DIM, (g + 1) * HEAD_DIM)
            new.append(_softmax_step(carry[g], _dot_nt(qs[g], k_t[:, hs]), bias,
                                     lambda p, hs=hs: _dot(p, v_t[:, hs]), tq))
        return tuple(new)

    _write_heads(o_ref, lax.fori_loop(0, n_kt, attn_tile, _softmax_init(tq)), tq)


def _dsa(iq, iw, q, ik, k, v, *, tq, tk, q_offset, top):
    b, sq, _ = iq.shape
    n_keys = ik.shape[1]
    assert sq % tq == 0 and n_keys % tk == 0 and tk % LANES == 0
    kern = functools.partial(_dsa_kernel, tq=tq, tk=tk, q_offset=q_offset, n_keys=n_keys, top=top)
    return pl.pallas_call(
        kern,
        grid=(b, sq // tq),
        in_specs=[
            pl.BlockSpec((1, tq, IDX_W), lambda bi, qi: (bi, qi, 0)),
            pl.BlockSpec((1, tq, N_IDX_HEADS), lambda bi, qi: (bi, qi, 0)),
            pl.BlockSpec((1, tq, ATTN_W), lambda bi, qi: (bi, qi, 0)),
            pl.BlockSpec((1, n_keys, D_IDX), lambda bi, qi: (bi, 0, 0)),
            pl.BlockSpec((1, n_keys, KV_W), lambda bi, qi: (bi, 0, 0)),
            pl.BlockSpec((1, n_keys, KV_W), lambda bi, qi: (bi, 0, 0)),
        ],
        out_specs=pl.BlockSpec((1, tq, ATTN_W), lambda bi, qi: (bi, qi, 0)),
        out_shape=jax.ShapeDtypeStruct((b, sq, ATTN_W), BF16),
        scratch_shapes=[pltpu.VMEM((tq, n_keys), jnp.int32)],
        compiler_params=pltpu.CompilerParams(dimension_semantics=("parallel", "arbitrary"),
                                             vmem_limit_bytes=VMEM_LIMIT),
        name="dsa",
    )(iq, iw, q, ik, k, v)


def _dsa_paged_kernel(pt_ref, iq_ref, iw_ref, q_ref, ikn_ref, kn_ref, vn_ref, cik_ref, ck_ref, cv_ref, o_ref,
                      ik_buf, k_buf, v_buf, sems, keys_ref, *, tq, n_pages, ppt, tk_sel, top):
    b = pl.program_id(0)
    slot = b % 2
    past = n_pages * PAGE_SIZE
    tk = ppt * PAGE_SIZE

    def page_copies(bi, sl, p):
        page = pt_ref[bi * n_pages + p]
        return (pltpu.make_async_copy(cik_ref.at[page], ik_buf.at[sl, p], sems.at[sl, 0]),
                pltpu.make_async_copy(ck_ref.at[page], k_buf.at[sl, p], sems.at[sl, 1]),
                pltpu.make_async_copy(cv_ref.at[page], v_buf.at[sl, p], sems.at[sl, 2]))

    def fetch(bi, sl):
        def body(p, carry):
            for cp in page_copies(bi, sl, p):
                cp.start()
            return carry
        lax.fori_loop(0, n_pages, body, 0)

    @pl.when(b == 0)
    def _():
        fetch(0, 0)

    @pl.when(b + 1 < pl.num_programs(0))
    def _():
        fetch(b + 1, 1 - slot)

    def wait_page(p, carry):
        for cp in page_copies(b, slot, p):
            cp.wait()
        return carry

    lax.fori_loop(0, n_pages, wait_page, 0)

    def page_cols(buf, i, idx=()):
        return jnp.concatenate([buf[(slot, i * ppt + j) + idx] for j in range(ppt)], axis=1).astype(BF16)

    iq = _stack_heads(iq_ref[0], range(N_IDX_HEADS), D_IDX)
    iw = iw_ref[0]
    row = lax.broadcasted_iota(jnp.int32, (tq, PAGE_SIZE), 0)
    lane = lax.broadcasted_iota(jnp.int32, (tq, PAGE_SIZE), 1)

    def score_tile(i, carry):
        start = pl.multiple_of(i * tk, tk)
        keys_ref[:, pl.ds(start, tk)] = _score_keys(_dot(iq, page_cols(ik_buf, i)), iw, tq)
        return carry

    lax.fori_loop(0, n_pages // ppt, score_tile, 0, unroll=2)
    new_keys = _score_keys(_dot(iq, ikn_ref[0]), iw, tq)
    keys_ref[:, past:past + PAGE_SIZE] = jnp.where(lane <= row, new_keys, INT_MIN)

    t = _select_threshold(keys_ref, (past + PAGE_SIZE) // tk_sel, tq=tq, tk=tk_sel, top=top)

    q_all = q_ref[0] * ATTN_SCALE
    qs = [_stack_heads(q_all, range(g * Q_PER_KV, (g + 1) * Q_PER_KV), HEAD_DIM) for g in range(N_KV_HEADS)]

    def attn_tile(i, carry):
        start = pl.multiple_of(i * tk, tk)
        bias = jnp.where(keys_ref[:, pl.ds(start, tk)] >= t, 0.0, NEG_BIG)
        new = []
        for g in range(N_KV_HEADS):
            v_t = page_cols(v_buf, i, (g,))
            new.append(_softmax_step(carry[g], _dot(qs[g], page_cols(k_buf, i, (g,))), bias,
                                     lambda p, v_t=v_t: _dot_nt(p, v_t), tq))
        return tuple(new)

    res = lax.fori_loop(0, n_pages // ppt, attn_tile, _softmax_init(tq), unroll=2)
    bias = jnp.where(keys_ref[:, past:past + PAGE_SIZE] >= t, 0.0, NEG_BIG)
    res = tuple(_softmax_step(res[g], _dot(qs[g], kn_ref[0, g]), bias,
                              lambda p, g=g: _dot_nt(p, vn_ref[0, g]), tq) for g in range(N_KV_HEADS))
    _write_heads(o_ref, res, tq)


def _dsa_paged(iq, iw, q, ikn, kn, vn, cik_t, ck_t, cv_t, page_table, *, tq, top):
    b = iq.shape[0]
    n_pages = page_table.shape[1]
    ppt = next(d for d in (4, 2, 1) if n_pages % d == 0)
    n_lane_tiles = n_pages + 1
    tk_sel = LANES * next(d for d in range(8, 0, -1) if n_lane_tiles % d == 0)
    kern = functools.partial(_dsa_paged_kernel, tq=tq, n_pages=n_pages, ppt=ppt, tk_sel=tk_sel, top=top)
    per_seq = lambda *blk: pl.BlockSpec((1,) + blk, lambda bi, pt: (bi,) + (0,) * len(blk))
    grid_spec = pltpu.PrefetchScalarGridSpec(
        num_scalar_prefetch=1,
        grid=(b,),
        in_specs=[per_seq(tq, IDX_W), per_seq(tq, N_IDX_HEADS), per_seq(tq, ATTN_W),
                  per_seq(D_IDX, PAGE_SIZE), per_seq(N_KV_HEADS, HEAD_DIM, PAGE_SIZE),
                  per_seq(N_KV_HEADS, HEAD_DIM, PAGE_SIZE),
                  pl.BlockSpec(memory_space=pl.ANY), pl.BlockSpec(memory_space=pl.ANY),
                  pl.BlockSpec(memory_space=pl.ANY)],
        out_specs=per_seq(tq, ATTN_W),
        scratch_shapes=[pltpu.VMEM((2, n_pages, D_IDX, PAGE_SIZE), F32),
                        pltpu.VMEM((2, n_pages, N_KV_HEADS, HEAD_DIM, PAGE_SIZE), F32),
                        pltpu.VMEM((2, n_pages, N_KV_HEADS, HEAD_DIM, PAGE_SIZE), F32),
                        pltpu.SemaphoreType.DMA((2, 3)),
                        pltpu.VMEM((tq, (n_pages + 1) * PAGE_SIZE), jnp.int32)],
    )
    return pl.pallas_call(
        kern,
        grid_spec=grid_spec,
        out_shape=jax.ShapeDtypeStruct((b, tq, ATTN_W), BF16),
        compiler_params=pltpu.CompilerParams(dimension_semantics=("arbitrary",), vmem_limit_bytes=VMEM_LIMIT),
        name="dsa_paged",
    )(page_table.reshape(-1), iq, iw, q, ikn, kn, vn, cik_t, ck_t, cv_t)


def _rwkv_kernel(pr_ref, sh0_ref, z0_ref, mu_ref, w0_ref, wup_ref, a0_ref, aup_ref, gup_ref, kk_ref, ka_ref,
                 rk_ref, lnw_ref, lnb_ref, ge_ref, get_ref,
                 ob_ref, zout_ref,
                 z_scr, prev_scr, r_s, k_s, v_s, a_s, b_s, lw_s, y_s, *, tt, chunk, t_valid):
    ti = pl.program_id(1)
    n_t = pl.num_programs(1)
    nh, hd = N_RWKV_HEADS, RWKV_HEAD

    @pl.when(ti == 0)
    def _():
        z_scr[...] = z0_ref[0]
        prev_scr[...] = sh0_ref[0]

    gsum = lambda z: _group_sum(z, ge_ref[...], get_ref[...])

    pr = pr_ref[0]
    row = lax.broadcasted_iota(jnp.int32, (tt, 1), 0)
    prev = jnp.where(row == 0, prev_scr[...], pltpu.roll(pr, 1, 0))
    prev_scr[...] = pr[tt - 1:tt]
    m = pr + mu_ref[...] * (prev - pr)
    r = m[:, 0:RWKV_W]
    k = m[:, RWKV_W:2 * RWKV_W]
    v = m[:, 2 * RWKV_W:3 * RWKV_W]
    o = 3 * RWKV_W
    wd = m[:, o:o + W_LORA]
    ad = m[:, o + W_LORA:o + W_LORA + A_LORA]
    gd = m[:, o + W_LORA + A_LORA:]

    lora = lambda z, w_ref: _mm(_split(z), _split(w_ref[...]))
    u = -(w0_ref[...] + lora(jnp.tanh(wd), wup_ref))
    softplus = jnp.maximum(u, 0.0) + jnp.log(1.0 + jnp.exp(-jnp.abs(u)))
    lw = -jnp.exp(-softplus - 0.5)
    a = jax.nn.sigmoid(a0_ref[...] + lora(ad, aup_ref))
    gate = lora(jax.nn.sigmoid(gd), gup_ref)
    kk = k * kk_ref[...]
    kk = kk / jnp.maximum(jnp.sqrt(gsum(kk * kk)), 1e-12)
    k2 = k * (1.0 + (a - 1.0) * ka_ref[...])
    bonus = gsum(r * k2 * rk_ref[...])
    av = -kk
    bv = kk * a
    if t_valid < tt:
        ok = (row < t_valid).astype(F32)
        k2, v, av, bv, lw = k2 * ok, v * ok, av * ok, bv * ok, lw * ok
    r_s[...] = r
    k_s[...] = k2
    v_s[...] = v
    a_s[...] = av
    b_s[...] = bv
    lw_s[...] = lw

    c = chunk
    ri = lax.broadcasted_iota(jnp.int32, (c, c), 0)
    ci = lax.broadcasted_iota(jnp.int32, (c, c), 1)
    tri_incl = (ci <= ri)
    tri_strict = (ci < ri)
    ltri = tri_incl.astype(BF16)
    eye_h = (lax.broadcasted_iota(jnp.int32, (hd, hd), 0) == lax.broadcasted_iota(jnp.int32, (hd, hd), 1))
    n_double = max(int(np.ceil(np.log2(c))), 1)
    heads = range(nh)
    hsl = [slice(h * hd, (h + 1) * hd) for h in heads]

    def chunk_body(ci_, carry):
        s0 = pl.multiple_of(ci_ * c, c)
        sl = pl.ds(s0, c)
        lwc = lw_s[sl, :]
        cum = sum(_dot(ltri, part) for part in _split(lwc, 3))
        cum_end = cum[c - 1:c, :]
        g_end = jnp.exp(cum_end)
        at = a_s[sl, :] * jnp.exp(cum - lwc)
        rt = r_s[sl, :] * jnp.exp(cum)
        g_inv = jnp.exp(-cum)
        g_tail = jnp.exp(cum_end - cum)
        bt = b_s[sl, :] * g_inv
        kt = k_s[sl, :] * g_inv
        bc = b_s[sl, :] * g_tail
        kc = k_s[sl, :] * g_tail
        vc = v_s[sl, :]

        left = [_split(jnp.concatenate([at[:, s], rt[:, s]], axis=0)) for s in hsl]
        right = [_split(jnp.concatenate([bt[:, s], kt[:, s]], axis=0)) for s in hsl]
        amat = [_mm(left[h], right[h], _dot_nt) for h in heads]
        a_ab = [jnp.where(tri_strict, amat[h][:c, :c], 0.0) for h in heads]
        a_ak = [_split(jnp.where(tri_strict, amat[h][:c, c:], 0.0)) for h in heads]
        a_rb = [_split(jnp.where(tri_incl, amat[h][c:, :c], 0.0)) for h in heads]
        a_rk = [_split(jnp.where(tri_incl, amat[h][c:, c:], 0.0)) for h in heads]
        vh = [_split(vc[:, s]) for s in hsl]
        akv = [_mm(a_ak[h], vh[h]) for h in heads]
        uu = [jnp.concatenate([at[:, hsl[h]], akv[h]], axis=1) for h in heads]
        pw = a_ab
        for step in range(n_double):
            pws = [_split(z) for z in pw]
            uus = [_split(z) for z in uu]
            uu = [uu[h] + _mm(pws[h], uus[h]) for h in heads]
            if step + 1 < n_double:
                pw = [_mm(pws[h], pws[h]) for h in heads]
        uus = [_split(z) for z in uu]
        x1 = [_mm(a_rb[h], uus[h]) for h in heads]
        x2 = [_mm(a_rk[h], vh[h]) for h in heads]
        mn = [_mm(_split(bc[:, hsl[h]]), uus[h], _dot_tn) for h in heads]
        nk = [_mm(_split(kc[:, hsl[h]]), vh[h], _dot_tn) for h in heads]
        pm = []
        for h in heads:
            p2 = rt[:, hsl[h]] + x1[h][:, :hd]
            mh = jnp.where(eye_h, g_end[:, hsl[h]], 0.0) + mn[h][:, :hd]
            pm.append(_split(jnp.concatenate([p2, mh], axis=0)))
        res = [_mm(pm[h], _split(z_scr[h])) for h in heads]
        for h in heads:
            y_s[sl, hsl[h]] = res[h][:c] + x1[h][:, hd:] + x2[h]
            z_scr[h] = res[h][c:] + mn[h][:, hd:] + nk[h]
        return carry

    lax.fori_loop(0, tt // c, chunk_body, 0)

    y = y_s[...]
    mean = gsum(y) * (1.0 / hd)
    dlt = y - mean
    var = gsum(dlt * dlt) * (1.0 / hd)
    yn = dlt * lax.rsqrt(var + LNX_EPS) * lnw_ref[...] + lnb_ref[...]
    ob_ref[0] = ((yn + bonus * v_s[...]) * gate).astype(ob_ref.dtype)

    @pl.when(ti == n_t - 1)
    def _():
        zout_ref[0] = z_scr[...]


def _rwkv(pr, shift0, z0, p, *, tt, chunk, t_valid):
    b, t, _ = pr.shape
    assert t % tt == 0 and tt % chunk == 0
    consts = (p["shift_mu"], p["w0"], p["w_lora_up"], p["a0"], p["a_lora_up"], p["g_lora_up"], p["k_k"], p["k_a"],
              p["r_k"], p["ln_x_w"], p["ln_x_b"], p["ge"], p["get"])
    kern = functools.partial(_rwkv_kernel, tt=tt, chunk=chunk, t_valid=t_valid)
    wide = lambda: pltpu.VMEM((tt, RWKV_W), F32)
    return pl.pallas_call(
        kern,
        grid=(b, t // tt),
        in_specs=[
            pl.BlockSpec((1, tt, RWKV_PROJ_W), lambda bi, ti: (bi, ti, 0)),
            pl.BlockSpec((1, 1, RWKV_PROJ_W), lambda bi, ti: (bi, 0, 0)),
            pl.BlockSpec((1, N_RWKV_HEADS, RWKV_HEAD, RWKV_HEAD), lambda bi, ti: (bi, 0, 0, 0)),
        ] + [_const_spec(c.shape) for c in consts],
        out_specs=[
            pl.BlockSpec((1, tt, RWKV_W), lambda bi, ti: (bi, ti, 0)),
            pl.BlockSpec((1, N_RWKV_HEADS, RWKV_HEAD, RWKV_HEAD), lambda bi, ti: (bi, 0, 0, 0)),
        ],
        out_shape=[jax.ShapeDtypeStruct((b, t, RWKV_W), BF16),
                   jax.ShapeDtypeStruct((b, N_RWKV_HEADS, RWKV_HEAD, RWKV_HEAD), F32)],
        scratch_shapes=[pltpu.VMEM((N_RWKV_HEADS, RWKV_HEAD, RWKV_HEAD), F32),
                        pltpu.VMEM((1, RWKV_PROJ_W), F32),
                        wide(), wide(), wide(), wide(), wide(), wide(), wide()],
        compiler_params=pltpu.CompilerParams(dimension_semantics=("parallel", "arbitrary"),
                                             vmem_limit_bytes=VMEM_LIMIT),
        name="rwkv",
    )(pr, shift0, z0, *consts)


def _merge_kernel(x_ref, oa_ref, ob_ref, ga_ref, gb_ref, wa_ref, wb_ref, wo_ref, nf_ref, rwt_ref, rb_ref,
                  h_out, hn_out, idx_out, gate_out):
    ma = _dot(oa_ref[...], wa_ref[...])
    mb = _dot(ob_ref[...], wb_ref[...])
    mm = ga_ref[...].astype(F32) * ma + gb_ref[...].astype(F32) * mb
    h = x_ref[...] + _dot(mm.astype(BF16), wo_ref[...])
    h_out[...] = h
    ms = jnp.mean(h * h, axis=-1, keepdims=True)
    hn = h * lax.rsqrt(ms + NORM_EPS) * nf_ref[...]
    hn_out[...] = hn.astype(hn_out.dtype)
    logits = _mm(_split(rwt_ref[...]), _split(hn), _dot_nt) + rb_ref[...]
    tm = logits.shape[1]
    expert = lax.broadcasted_iota(jnp.int32, logits.shape, 0)
    vals, idxs = [], []
    for _ in range(TOP_K):
        mx = jnp.max(logits, axis=0, keepdims=True)
        ix = jnp.min(jnp.where(logits == mx, expert, N_EXPERTS), axis=0, keepdims=True)
        vals.append(mx)
        idxs.append(ix)
        logits = jnp.where(expert == ix, -jnp.inf, logits)
    es = [jnp.exp(v - vals[0]) for v in vals]
    den = es[0] + es[1] + es[2] + es[3]
    pad = 8 - TOP_K
    idx_out[...] = jnp.concatenate(idxs + [jnp.zeros((pad, tm), jnp.int32)], axis=0)
    gate_out[...] = jnp.concatenate([e / den for e in es] + [jnp.zeros((pad, tm), F32)], axis=0)


def _merge(x2d, oa, ob, ga, gb, p, tm):
    n = x2d.shape[0]
    row = lambda w: pl.BlockSpec((tm, w), lambda i: (i, 0))
    consts = (p["w_proj_a"], p["w_proj_b"], p["w_out"], p["norm_ffn"], p["router_w"], p["router_b"])
    return pl.pallas_call(
        _merge_kernel,
        grid=(n // tm,),
        in_specs=[row(D_MODEL), row(ATTN_W), row(RWKV_W), row(D_MODEL), row(D_MODEL)]
        + [_const_spec(c.shape) for c in consts],
        out_specs=[row(D_MODEL), row(D_MODEL)] + [pl.BlockSpec((8, tm), lambda i: (0, i))] * 2,
        out_shape=[jax.ShapeDtypeStruct((n, D_MODEL), F32), jax.ShapeDtypeStruct((n, D_MODEL), BF16),
                   jax.ShapeDtypeStruct((8, n), jnp.int32), jax.ShapeDtypeStruct((8, n), F32)],
        compiler_params=pltpu.CompilerParams(dimension_semantics=("parallel",), vmem_limit_bytes=VMEM_LIMIT),
        name="merge",
    )(x2d, oa, ob, ga, gb, *consts)


def _moe_kernel(be_ref, nb_ref, x_ref, wgu_ref, bgu_ref, wd_ref, bd_ref, o_ref, wgu_s, wd_s):
    i = pl.program_id(0)
    used = i < nb_ref[0]

    @pl.when(used & ((i == 0) | (be_ref[i] != be_ref[jnp.maximum(i - 1, 0)])))
    def _():
        wgu_s[...] = wgu_ref[0].astype(wgu_s.dtype)
        wd_s[...] = wd_ref[0].astype(wd_s.dtype)

    @pl.when(used)
    def _():
        hcat = _dot(x_ref[...], wgu_s[...]) + bgu_ref[0]
        glu = jnp.minimum(hcat[:, :D_FF], SWIGLU_LIMIT)
        lin = jnp.clip(hcat[:, D_FF:], -SWIGLU_LIMIT, SWIGLU_LIMIT)
        act = glu * jax.nn.sigmoid(SWIGLU_ALPHA * glu) * (lin + 1.0)
        o_ref[...] = _dot(act.astype(BF16), wd_s[...]) + bd_ref[0]

    @pl.when(jnp.logical_not(used))
    def _():
        o_ref[...] = jnp.zeros(o_ref.shape, o_ref.dtype)


def _moe_rows(xg, block_exp, n_used, p, bm):
    rows = xg.shape[0]
    grid_spec = pltpu.PrefetchScalarGridSpec(
        num_scalar_prefetch=2,
        grid=(rows // bm,),
        in_specs=[
            pl.BlockSpec((bm, D_MODEL), lambda i, be, nb: (i, 0)),
            pl.BlockSpec((1, D_MODEL, 2 * D_FF), lambda i, be, nb: (be[i], 0, 0)),
            pl.BlockSpec((1, 1, 2 * D_FF), lambda i, be, nb: (be[i], 0, 0)),
            pl.BlockSpec((1, D_FF, D_MODEL), lambda i, be, nb: (be[i], 0, 0)),
            pl.BlockSpec((1, 1, D_MODEL), lambda i, be, nb: (be[i], 0, 0)),
        ],
        out_specs=pl.BlockSpec((bm, D_MODEL), lambda i, be, nb: (i, 0)),
        scratch_shapes=[pltpu.VMEM((D_MODEL, 2 * D_FF), BF16), pltpu.VMEM((D_FF, D_MODEL), BF16)],
    )
    return pl.pallas_call(
        _moe_kernel,
        grid_spec=grid_spec,
        out_shape=jax.ShapeDtypeStruct((rows, D_MODEL), F32),
        compiler_params=pltpu.CompilerParams(dimension_semantics=("arbitrary",), vmem_limit_bytes=VMEM_LIMIT),
        name="moe",
    )(block_exp, n_used, xg, p["w_gate_up"], p["b_gate_up"], p["w_down"], p["b_down"])


def _moe(hn, top_idx, gate, p, bm):
    n = hn.shape[0]
    nk = n * TOP_K
    e_flat = top_idx.reshape(-1)
    onehot = (e_flat[:, None] == jnp.arange(N_EXPERTS, dtype=jnp.int32)[None, :]).astype(jnp.int32)
    csum = jnp.cumsum(onehot, axis=0)
    counts = csum[-1]
    rank = jnp.take_along_axis(csum, e_flat[:, None], axis=1)[:, 0] - 1
    padded = (counts + bm - 1) // bm * bm
    pends = jnp.cumsum(padded)
    pstarts = pends - padded
    dest = pstarts[e_flat] + rank
    n_blocks = -(-nk // bm) + N_EXPERTS
    rows = n_blocks * bm
    tok_flat = jnp.arange(nk, dtype=jnp.int32) % n
    row_tok = jnp.zeros((rows,), jnp.int32).at[dest].set(tok_flat, unique_indices=True)
    block_start = jnp.arange(n_blocks, dtype=jnp.int32) * bm
    block_exp = jnp.minimum(jnp.sum((pends[None, :] <= block_start[:, None]).astype(jnp.int32), axis=1),
                            N_EXPERTS - 1)
    n_used = (pends[-1:] // bm).astype(jnp.int32)
    out = _moe_rows(hn[row_tok], block_exp, n_used, p, bm)
    return (out[dest.reshape(TOP_K, n)] * gate[:, :, None]).sum(axis=0)


def _prep_params(norm_mix, w_in, q_norm, k_norm, idx_k_norm, shift_mu, w0, w_lora_up, a0, a_lora_up, g_lora_up, k_k,
                 k_a, r_k, ln_x_w, ln_x_b, w_proj_a, w_proj_b, w_out, norm_ffn, router_w, router_b, w_gate_up,
                 b_gate_up, w_down, b_down):
    splits = (ATTN_W, KV_W, KV_W, IDX_W, D_IDX, N_IDX_HEADS, RWKV_PROJ_W, D_MODEL, D_MODEL)
    cuts = np.cumsum(splits)[:-1].tolist()
    wq, wk, wv, wiq, wik, wiw, wpr, wga, wgb = jnp.split(w_in.astype(BF16), cuts, axis=-1)
    wikw = jnp.concatenate([wik, wiw, jnp.zeros((D_MODEL, LANES - D_IDX - N_IDX_HEADS), BF16)], axis=-1)
    row = lambda z: z.reshape(1, -1).astype(F32)
    ge = _group_indicator(RWKV_W, RWKV_HEAD).astype(BF16)
    ge2 = _group_indicator(KV_W, HEAD_DIM).astype(BF16)
    return dict(
        norm_mix=row(norm_mix), wq=wq, wk=wk, wv=wv, wiq=wiq, wikw=wikw, wpr=wpr, wga=wga, wgb=wgb,
        q_norm_t=row(jnp.tile(q_norm, N_Q_HEADS)), k_norm_t=row(jnp.tile(k_norm, N_KV_HEADS)),
        ik_norm_t=row(idx_k_norm),
        ge=ge, get=ge.T, ge2=ge2, ge2t=ge2.T,
        shift_mu=row(shift_mu), w0=row(w0), w_lora_up=w_lora_up, a0=row(a0), a_lora_up=a_lora_up,
        g_lora_up=g_lora_up, k_k=row(k_k), k_a=row(k_a), r_k=row(r_k), ln_x_w=row(ln_x_w), ln_x_b=row(ln_x_b),
        w_proj_a=w_proj_a.astype(BF16), w_proj_b=w_proj_b.astype(BF16), w_out=w_out.astype(BF16),
        norm_ffn=row(norm_ffn), router_w=router_w.T, router_b=router_b.reshape(-1, 1),
        w_gate_up=w_gate_up, b_gate_up=b_gate_up[:, None, :], w_down=w_down, b_down=b_down[:, None, :],
    )


def _pad_axis(z, axis, size):
    if z.shape[axis] == size:
        return z
    pad = [(0, 0)] * z.ndim
    pad[axis] = (0, size - z.shape[axis])
    return jnp.pad(z, pad)


def _group(x, p, *, tm):
    b, t, _ = x.shape
    names = ("q", "k", "kb", "v", "vb", "iq", "ik", "ikb", "iw", "pr", "ga", "gb")
    g = dict(zip(names, _in_proj(x.reshape(b * t, D_MODEL), p, tm)), b=b, t=t)
    g["pr"] = g["pr"].reshape(b, t, RWKV_PROJ_W)
    return g


def _attend(g, ik_all, k_all, v_all, *, tq, tk, q_offset, top):
    b, t = g["b"], g["t"]
    tp = -(-t // tq) * tq
    n_keys = -(-ik_all.shape[1] // tk) * tk
    seq = lambda z: _pad_axis(z.reshape(b, t, -1), 1, tp)
    keys = lambda z: _pad_axis(z, 1, n_keys)
    oa = _dsa(seq(g["iq"]), seq(g["iw"]), seq(g["q"]), keys(ik_all), keys(k_all), keys(v_all),
              tq=tq, tk=tk, q_offset=q_offset, top=top)
    return oa[:, :t].reshape(b * t, ATTN_W)


def _mix(g, shift0, wkv0, p, *, tt, chunk):
    b, t = g["b"], g["t"]
    tp = -(-t // tt) * tt
    ob, z = _rwkv(_pad_axis(g["pr"], 1, tp), shift0[:, None, :], jnp.swapaxes(wkv0, -1, -2), p,
                  tt=tt, chunk=chunk, t_valid=min(t, tt) if tp != t else tt)
    return ob[:, :t].reshape(b * t, RWKV_W), jnp.swapaxes(z, -1, -2)


def kernel(x_prompt, x_sample, cache_k, cache_v, cache_idx_k, page_table, state_wkv, state_shift, norm_mix, w_in, q_norm, k_norm, idx_k_norm, shift_mu, w0, w_lora_up, a0, a_lora_up, g_lora_up, k_k, k_a, r_k, ln_x_w, ln_x_b, w_proj_a, w_proj_b, w_out, norm_ffn, router_w, router_b, w_gate_up, b_gate_up, w_down, b_down):
    depth = norm_mix.shape[0]
    assert depth == 1
    params = (norm_mix, w_in, q_norm, k_norm, idx_k_norm, shift_mu, w0, w_lora_up, a0, a_lora_up, g_lora_up, k_k, k_a,
              r_k, ln_x_w, ln_x_b, w_proj_a, w_proj_b, w_out, norm_ffn, router_w, router_b, w_gate_up, b_gate_up,
              w_down, b_down)
    p = _prep_params(*[z[0] for z in params])
    bp, sp, _ = x_prompt.shape
    bs, ts, _ = x_sample.shape
    n_p, n_s = bp * sp, bs * ts
    past = page_table.shape[1] * PAGE_SIZE

    gp = _group(x_prompt, p, tm=min(256, n_p))
    k_p = gp["k"].reshape(bp, sp, N_KV_HEADS, HEAD_DIM)
    v_p = gp["v"].reshape(bp, sp, N_KV_HEADS, HEAD_DIM)
    ik_p = gp["ik"].reshape(bp, sp, D_IDX)
    oa_p = _attend(gp, gp["ikb"].reshape(bp, sp, D_IDX), gp["kb"].reshape(bp, sp, KV_W),
                   gp["vb"].reshape(bp, sp, KV_W), tq=min(128, sp), tk=min(512, sp), q_offset=0,
                   top=min(TOPK_MAX, sp // 4))
    ob_p, wkv_p = _mix(gp, jnp.zeros((bp, RWKV_PROJ_W), F32),
                       jnp.zeros((bp, N_RWKV_HEADS, RWKV_HEAD, RWKV_HEAD), F32), p,
                       tt=min(256, sp), chunk=min(64, sp))

    gs = _group(x_sample, p, tm=min(256, n_s))
    k_s = gs["k"].reshape(bs, ts, N_KV_HEADS, HEAD_DIM)
    v_s = gs["v"].reshape(bs, ts, N_KV_HEADS, HEAD_DIM)
    ik_s = gs["ik"].reshape(bs, ts, D_IDX)
    tq_s = 16
    seq = lambda z: _pad_axis(z.reshape(bs, ts, -1), 1, tq_s)
    new_t = lambda z, *hd: _pad_axis(jnp.moveaxis(z.reshape((bs, ts) + hd), 1, -1), len(hd) + 1, PAGE_SIZE)
    oa_s = _dsa_paged(seq(gs["iq"]), seq(gs["iw"]), seq(gs["q"]),
                      new_t(gs["ikb"], D_IDX), new_t(gs["kb"], N_KV_HEADS, HEAD_DIM),
                      new_t(gs["vb"], N_KV_HEADS, HEAD_DIM),
                      jnp.transpose(cache_idx_k[0], (0, 2, 1)), jnp.transpose(cache_k[0], (0, 2, 3, 1)),
                      jnp.transpose(cache_v[0], (0, 2, 3, 1)), page_table,
                      tq=tq_s, top=min(TOPK_MAX, (past + ts) // 4))[:, :ts].reshape(n_s, ATTN_W)
    ob_s, wkv_s = _mix(gs, state_shift[0], state_wkv[0], p, tt=8, chunk=8)

    h_p, hn_p, idx_p, gate_p = _merge(x_prompt.reshape(n_p, D_MODEL), oa_p, ob_p, gp["ga"], gp["gb"], p,
                                      tm=min(256, n_p))
    h_s, hn_s, idx_s, gate_s = _merge(x_sample.reshape(n_s, D_MODEL), oa_s, ob_s, gs["ga"], gs["gb"], p,
                                      tm=min(256, n_s))
    hn = jnp.concatenate([hn_p, hn_s], axis=0)
    top_idx = jnp.concatenate([idx_p[:TOP_K], idx_s[:TOP_K]], axis=1)
    gate = jnp.concatenate([gate_p[:TOP_K], gate_s[:TOP_K]], axis=1)
    f = _moe(hn, top_idx, gate, p, bm=512)
    y_p = (h_p + f[:n_p]).reshape(bp, sp, D_MODEL)
    y_s = (h_s + f[n_p:]).reshape(bs, ts, D_MODEL)

    st = lambda z: z[None]
    return (y_p, y_s, st(k_p), st(v_p), st(ik_p), st(wkv_p), st(gp["pr"][:, -1]),
            st(k_s), st(v_s), st(ik_s), st(wkv_s), st(gs["pr"][:, -1]))
```

```python
import functools

import jax
import jax.numpy as jnp
import numpy as np
from jax import lax
from jax.experimental import pallas as pl
from jax.experimental.pallas import tpu as pltpu

D_MODEL = 1024
PAGE_SIZE = 128
HEAD_DIM = 64
N_Q_HEADS = 8
N_KV_HEADS = 2
Q_PER_KV = N_Q_HEADS // N_KV_HEADS
ATTN_W = N_Q_HEADS * HEAD_DIM
KV_W = N_KV_HEADS * HEAD_DIM
ATTN_SCALE = HEAD_DIM ** -0.5
N_IDX_HEADS = 8
D_IDX = 64
IDX_W = N_IDX_HEADS * D_IDX
IDX_SCALE = (N_IDX_HEADS * D_IDX) ** -0.5
TOPK_MAX = 256
RWKV_HEAD = 64
N_RWKV_HEADS = 8
RWKV_W = N_RWKV_HEADS * RWKV_HEAD
W_LORA = 64
A_LORA = 64
G_LORA = 128
RWKV_PROJ_W = 3 * RWKV_W + W_LORA + A_LORA + G_LORA
LNX_EPS = 64e-5
N_EXPERTS = 32
TOP_K = 4
D_FF = 1024
SWIGLU_LIMIT = 7.0
SWIGLU_ALPHA = 1.702
NORM_EPS = 1e-6

LANES = 128
VMEM_LIMIT = 56 * 1024 * 1024
INT_MIN = -(2 ** 31)
NEG_BIG = -1e30
F32 = jnp.float32
BF16 = jnp.bfloat16
HI = lax.Precision.HIGHEST


def _dot(a, b, precision=None):
    return jnp.dot(a, b, preferred_element_type=F32, precision=precision)


def _dot_nt(a, b, precision=None):
    return lax.dot_general(a, b, (((1,), (1,)), ((), ())), preferred_element_type=F32, precision=precision)


def _dot_tn(a, b, precision=None):
    return lax.dot_general(a, b, (((0,), (0,)), ((), ())), preferred_element_type=F32, precision=precision)


def _group_indicator(width, group):
    r = np.arange(width) // group
    return jnp.asarray((r[:, None] == np.arange(LANES)[None, :]).astype(np.float32))


def _split(x, terms=2):
    parts = []
    for _ in range(terms - 1):
        hi = x.astype(BF16)
        parts.append(hi)
        x = x - hi.astype(F32)
    parts.append(x.astype(BF16))
    return parts


def _mm(a, b, dot=None):
    dot = dot or _dot
    return dot(a[0], b[0]) + (dot(a[0], b[1]) + dot(a[1], b[0]))


def _group_sum(x, ge, get):
    s = sum(_dot(part, ge) for part in _split(x))
    return sum(_dot(part, get) for part in _split(s))


def _const_spec(shape):
    nd = len(shape)
    return pl.BlockSpec(shape, lambda *_: (0,) * nd)


def _inproj_kernel(x_ref, g_ref, wq_ref, wk_ref, wv_ref, wiq_ref, wikw_ref, wpr_ref, wga_ref, wgb_ref,
                   qn_ref, kn_ref, ikn_ref, ge_ref, get_ref, ge2_ref, ge2t_ref,
                   q_out, k_out, kb_out, v_out, vb_out, iq_out, ik_out, ikb_out, iw_out, pr_out, ga_out, gb_out):
    x = x_ref[...]
    ms = jnp.mean(x * x, axis=-1, keepdims=True)
    xn = (x * lax.rsqrt(ms + NORM_EPS) * g_ref[...]).astype(BF16)

    q = _dot(xn, wq_ref[...])
    qs = _group_sum(q * q, ge_ref[...], get_ref[...]) * (1.0 / HEAD_DIM)
    q_out[...] = (q * lax.rsqrt(qs + NORM_EPS) * qn_ref[...]).astype(q_out.dtype)

    k = _dot(xn, wk_ref[...])
    ks = _group_sum(k * k, ge2_ref[...], ge2t_ref[...]) * (1.0 / HEAD_DIM)
    kn = k * lax.rsqrt(ks + NORM_EPS) * kn_ref[...]
    for g in range(N_KV_HEADS):
        k_out[:, g, :] = kn[:, g * HEAD_DIM:(g + 1) * HEAD_DIM]
    kb_out[...] = kn.astype(kb_out.dtype)

    v = _dot(xn, wv_ref[...])
    for g in range(N_KV_HEADS):
        v_out[:, g, :] = v[:, g * HEAD_DIM:(g + 1) * HEAD_DIM]
    vb_out[...] = v.astype(vb_out.dtype)
    iq_out[...] = _dot(xn, wiq_ref[...]).astype(iq_out.dtype)

    ikw = _dot(xn, wikw_ref[...])
    lane = lax.broadcasted_iota(jnp.int32, ikw.shape, 1)
    iks = jnp.sum(jnp.where(lane < D_IDX, ikw * ikw, 0.0), axis=-1, keepdims=True) * (1.0 / D_IDX)
    ikn = ikw[:, :D_IDX] * lax.rsqrt(iks + NORM_EPS) * ikn_ref[...]
    ik_out[...] = ikn
    ikb_out[...] = ikn.astype(ikb_out.dtype)
    iw_out[...] = ikw[:, D_IDX:D_IDX + N_IDX_HEADS]

    pr_out[...] = _dot(xn, wpr_ref[...])
    ga_out[...] = jax.nn.sigmoid(_dot(xn, wga_ref[...])).astype(ga_out.dtype)
    gb_out[...] = jax.nn.sigmoid(_dot(xn, wgb_ref[...])).astype(gb_out.dtype)


def _in_proj(x2d, p, tm):
    n = x2d.shape[0]
    widths = (ATTN_W, KV_W, KV_W, KV_W, KV_W, IDX_W, D_IDX, D_IDX, N_IDX_HEADS, RWKV_PROJ_W, D_MODEL, D_MODEL)
    dtypes = (BF16, F32, BF16, F32, BF16, BF16, F32, BF16, F32, F32, BF16, BF16)
    row = lambda w: pl.BlockSpec((tm, w), lambda i: (i, 0))
    heads_spec = pl.BlockSpec((tm, N_KV_HEADS, HEAD_DIM), lambda i: (i, 0, 0))
    consts = (p["norm_mix"], p["wq"], p["wk"], p["wv"], p["wiq"], p["wikw"], p["wpr"], p["wga"], p["wgb"],
              p["q_norm_t"], p["k_norm_t"], p["ik_norm_t"], p["ge"], p["get"], p["ge2"], p["ge2t"])
    return pl.pallas_call(
        _inproj_kernel,
        grid=(n // tm,),
        in_specs=[row(D_MODEL)] + [_const_spec(c.shape) for c in consts],
        out_specs=[heads_spec if i in (1, 3) else row(w) for i, w in enumerate(widths)],
        out_shape=[jax.ShapeDtypeStruct((n, N_KV_HEADS, HEAD_DIM) if i in (1, 3) else (n, w), d)
                   for i, (w, d) in enumerate(zip(widths, dtypes))],
        compiler_params=pltpu.CompilerParams(dimension_semantics=("parallel",), vmem_limit_bytes=VMEM_LIMIT),
        name="in_proj",
    )(x2d, *consts)


def _stack_heads(x, heads, width):
    return jnp.concatenate([x[:, h * width:(h + 1) * width] for h in heads], axis=0)


def _score_keys(d, iw, tq):
    acc = jnp.zeros((tq, d.shape[1]), F32)
    for h in range(N_IDX_HEADS):
        acc = acc + jnp.maximum(d[h * tq:(h + 1) * tq], 0.0) * iw[:, h:h + 1]
    sc = acc * IDX_SCALE
    sc = jnp.where(sc == 0.0, 0.0, sc)
    bits = pltpu.bitcast(sc, jnp.int32)
    return bits ^ ((bits >> 31) & 0x7FFFFFFF)


def _select_threshold(keys_ref, n_kt, *, tq, tk, top):
    lane_pos = lax.broadcasted_iota(jnp.int32, (tq, tk), 1)

    def count(pred):
        def body(kt, c):
            start = pl.multiple_of(kt * tk, tk)
            m = pred(keys_ref[:, pl.ds(start, tk)], start).astype(F32)
            part = m[:, 0:LANES]
            for j in range(1, tk // LANES):
                part = part + m[:, j * LANES:(j + 1) * LANES]
            return c + part
        c = lax.fori_loop(0, n_kt, body, jnp.zeros((tq, LANES), F32))
        return jnp.sum(c, axis=-1, keepdims=True)

    c0 = count(lambda k, s: k >= 0)
    t = jnp.where(c0 >= top, 0, INT_MIN).astype(jnp.int32)
    c_t = jnp.where(c0 >= top, c0, 0)

    def bit_step(i, carry):
        t, c_t = carry
        cand = t | jnp.left_shift(jnp.int32(1), 30 - i)
        c = count(lambda k, s: k >= cand)
        return jnp.where(c >= top, cand, t), jnp.where(c >= top, c, c_t)

    t, c_t = lax.fori_loop(0, 31, bit_step, (t, c_t))
    t = jnp.maximum(t, INT_MIN + 1)

    excess = c_t > top

    @pl.when(jnp.max(excess.astype(jnp.int32)) > 0)
    def _():
        keep = top - count(lambda k, s: k > t)

        def idx_step(i, lim):
            cand = lim | jnp.left_shift(jnp.int32(1), 14 - i)
            c = count(lambda k, s: ((k == t) & (s + lane_pos < cand)))
            return jnp.where(c <= keep, cand, lim)

        lim = lax.fori_loop(0, 15, idx_step, jnp.zeros((tq, 1), jnp.int32))

        def demote(kt, carry):
            start = pl.multiple_of(kt * tk, tk)
            k = keys_ref[:, pl.ds(start, tk)]
            drop = (k == t) & (start + lane_pos >= lim) & excess
            keys_ref[:, pl.ds(start, tk)] = jnp.where(drop, t - 1, k)
            return carry

        lax.fori_loop(0, n_kt, demote, 0)

    return t


def _softmax_step(state, s, bias, pv, tq):
    m_i, l_i, acc = state
    gq, tk = s.shape
    s = (s.reshape(Q_PER_KV, tq, tk) + bias[None]).reshape(gq, tk)
    m_n = jnp.maximum(m_i, jnp.max(s, axis=-1, keepdims=True))
    alpha = jnp.exp(m_i - m_n)
    pm = jnp.exp(s - m_n)
    l_n = alpha * l_i + jnp.sum(pm, axis=-1, keepdims=True)
    return m_n, l_n, alpha * acc + pv(pm.astype(BF16))


def _softmax_init(tq):
    gq = Q_PER_KV * tq
    return tuple((jnp.full((gq, 1), NEG_BIG, F32), jnp.zeros((gq, 1), F32), jnp.zeros((gq, HEAD_DIM), F32))
                 for _ in range(N_KV_HEADS))


def _write_heads(o_ref, res, tq):
    for g in range(N_KV_HEADS):
        _, l_i, acc = res[g]
        og = acc / l_i
        for j in range(Q_PER_KV):
            h = g * Q_PER_KV + j
            o_ref[0, :, h * HEAD_DIM:(h + 1) * HEAD_DIM] = og[j * tq:(j + 1) * tq].astype(o_ref.dtype)


def _dsa_kernel(iq_ref, iw_ref, q_ref, ik_ref, k_ref, v_ref, o_ref, keys_ref, *, tq, tk, q_offset, n_keys, top):
    qi = pl.program_id(1)
    q_base = q_offset + qi * tq
    n_kt = jnp.minimum((q_base + tq + tk - 1) // tk, n_keys // tk)
    q_pos = q_base + lax.broadcasted_iota(jnp.int32, (tq, 1), 0)
    lane_pos = lax.broadcasted_iota(jnp.int32, (tq, tk), 1)

    iq = _stack_heads(iq_ref[0], range(N_IDX_HEADS), D_IDX)
    iw = iw_ref[0]

    def score_tile(kt, carry):
        start = pl.multiple_of(kt * tk, tk)
        key = _score_keys(_dot_nt(iq, ik_ref[0, pl.ds(start, tk), :]), iw, tq)
        keys_ref[:, pl.ds(start, tk)] = jnp.where(start + lane_pos <= q_pos, key, INT_MIN)
        return carry

    lax.fori_loop(0, n_kt, score_tile, 0)

    t = _select_threshold(keys_ref, n_kt, tq=tq, tk=tk, top=top)

    q_all = q_ref[0] * ATTN_SCALE
    qs = [_stack_heads(q_all, range(g * Q_PER_KV, (g + 1) * Q_PER_KV), HEAD_DIM) for g in range(N_KV_HEADS)]

    def attn_tile(kt, carry):
        start = pl.multiple_of(kt * tk, tk)
        bias = jnp.where(keys_ref[:, pl.ds(start, tk)] >= t, 0.0, NEG_BIG)
        k_t = k_ref[0, pl.ds(start, tk), :]
        v_t = v_ref[0, pl.ds(start, tk), :]
        new = []
        for g in range(N_KV_HEADS):
            hs = slice(g * HEAD_DIM, (g + 1) * HEAD_DIM)
            new.append(_softmax_step(carry[g], _dot_nt(qs[g], k_t[:, hs]), bias,
                                     lambda p, hs=hs: _dot(p, v_t[:, hs]), tq))
        return tuple(new)

    _write_heads(o_ref, lax.fori_loop(0, n_kt, attn_tile, _softmax_init(tq)), tq)


def _dsa(iq, iw, q, ik, k, v, *, tq, tk, q_offset, top):
    b, sq, _ = iq.shape
    n_keys = ik.shape[1]
    assert sq % tq == 0 and n_keys % tk == 0 and tk % LANES == 0
    kern = functools.partial(_dsa_kernel, tq=tq, tk=tk, q_offset=q_offset, n_keys=n_keys, top=top)
    return pl.pallas_call(
        kern,
        grid=(b, sq // tq),
        in_specs=[
            pl.BlockSpec((1, tq, IDX_W), lambda bi, qi: (bi, qi, 0)),
            pl.BlockSpec((1, tq, N_IDX_HEADS), lambda bi, qi: (bi, qi, 0)),
            pl.BlockSpec((1, tq, ATTN_W), lambda bi, qi: (bi, qi, 0)),
            pl.BlockSpec((1, n_keys, D_IDX), lambda bi, qi: (bi, 0, 0)),
            pl.BlockSpec((1, n_keys, KV_W), lambda bi, qi: (bi, 0, 0)),
            pl.BlockSpec((1, n_keys, KV_W), lambda bi, qi: (bi, 0, 0)),
        ],
        out_specs=pl.BlockSpec((1, tq, ATTN_W), lambda bi, qi: (bi, qi, 0)),
        out_shape=jax.ShapeDtypeStruct((b, sq, ATTN_W), BF16),
        scratch_shapes=[pltpu.VMEM((tq, n_keys), jnp.int32)],
        compiler_params=pltpu.CompilerParams(dimension_semantics=("parallel", "arbitrary"),
                                             vmem_limit_bytes=VMEM_LIMIT),
        name="dsa",
    )(iq, iw, q, ik, k, v)


def _dsa_paged_kernel(pt_ref, iq_ref, iw_ref, q_ref, ikn_ref, kn_ref, vn_ref, cik_ref, ck_ref, cv_ref, o_ref,
                      ik_buf, k_buf, v_buf, sems, keys_ref, *, tq, n_pages, ppt, tk_sel, top):
    b = pl.program_id(0)
    slot = b % 2
    past = n_pages * PAGE_SIZE
    tk = ppt * PAGE_SIZE

    def page_copies(bi, sl, p):
        page = pt_ref[bi * n_pages + p]
        return (pltpu.make_async_copy(cik_ref.at[page], ik_buf.at[sl, p], sems.at[sl, 0]),
                pltpu.make_async_copy(ck_ref.at[page], k_buf.at[sl, p], sems.at[sl, 1]),
                pltpu.make_async_copy(cv_ref.at[page], v_buf.at[sl, p], sems.at[sl, 2]))

    def fetch(bi, sl):
        def body(p, carry):
            for cp in page_copies(bi, sl, p):
                cp.start()
            return carry
        lax.fori_loop(0, n_pages, body, 0)

    @pl.when(b == 0)
    def _():
        fetch(0, 0)

    @pl.when(b + 1 < pl.num_programs(0))
    def _():
        fetch(b + 1, 1 - slot)

    def wait_page(p, carry):
        for cp in page_copies(b, slot, p):
            cp.wait()
        return carry

    lax.fori_loop(0, n_pages, wait_page, 0)

    def page_cols(buf, i, idx=()):
        return jnp.concatenate([buf[(slot, i * ppt + j) + idx] for j in range(ppt)], axis=1).astype(BF16)

    iq = _stack_heads(iq_ref[0], range(N_IDX_HEADS), D_IDX)
    iw = iw_ref[0]
    row = lax.broadcasted_iota(jnp.int32, (tq, PAGE_SIZE), 0)
    lane = lax.broadcasted_iota(jnp.int32, (tq, PAGE_SIZE), 1)

    def score_tile(i, carry):
        start = pl.multiple_of(i * tk, tk)
        keys_ref[:, pl.ds(start, tk)] = _score_keys(_dot(iq, page_cols(ik_buf, i)), iw, tq)
        return carry

    lax.fori_loop(0, n_pages // ppt, score_tile, 0, unroll=2)
    new_keys = _score_keys(_dot(iq, ikn_ref[0]), iw, tq)
    keys_ref[:, past:past + PAGE_SIZE] = jnp.where(lane <= row, new_keys, INT_MIN)

    t = _select_threshold(keys_ref, (past + PAGE_SIZE) // tk_sel, tq=tq, tk=tk_sel, top=top)

    q_all = q_ref[0] * ATTN_SCALE
    qs = [_stack_heads(q_all, range(g * Q_PER_KV, (g + 1) * Q_PER_KV), HEAD_DIM) for g in range(N_KV_HEADS)]

    def attn_tile(i, carry):
        start = pl.multiple_of(i * tk, tk)
        bias = jnp.where(keys_ref[:, pl.ds(start, tk)] >= t, 0.0, NEG_BIG)
        new = []
        for g in range(N_KV_HEADS):
            v_t = page_cols(v_buf, i, (g,))
            new.append(_softmax_step(carry[g], _dot(qs[g], page_cols(k_buf, i, (g,))), bias,
                                     lambda p, v_t=v_t: _dot_nt(p, v_t), tq))
        return tuple(new)

    res = lax.fori_loop(0, n_pages // ppt, attn_tile, _softmax_init(tq), unroll=2)
    bias = jnp.where(keys_ref[:, past:past + PAGE_SIZE] >= t, 0.0, NEG_BIG)
    res = tuple(_softmax_step(res[g], _dot(qs[g], kn_ref[0, g]), bias,
                              lambda p, g=g: _dot_nt(p, vn_ref[0, g]), tq) for g in range(N_KV_HEADS))
    _write_heads(o_ref, res, tq)


def _dsa_paged(iq, iw, q, ikn, kn, vn, cik_t, ck_t, cv_t, page_table, *, tq, top):
    b = iq.shape[0]
    n_pages = page_table.shape[1]
    ppt = next(d for d in (4, 2, 1) if n_pages % d == 0)
    n_lane_tiles = n_pages + 1
    tk_sel = LANES * next(d for d in range(8, 0, -1) if n_lane_tiles % d == 0)
    kern = functools.partial(_dsa_paged_kernel, tq=tq, n_pages=n_pages, ppt=ppt, tk_sel=tk_sel, top=top)
    per_seq = lambda *blk: pl.BlockSpec((1,) + blk, lambda bi, pt: (bi,) + (0,) * len(blk))
    grid_spec = pltpu.PrefetchScalarGridSpec(
        num_scalar_prefetch=1,
        grid=(b,),
        in_specs=[per_seq(tq, IDX_W), per_seq(tq, N_IDX_HEADS), per_seq(tq, ATTN_W),
                  per_seq(D_IDX, PAGE_SIZE), per_seq(N_KV_HEADS, HEAD_DIM, PAGE_SIZE),
                  per_seq(N_KV_HEADS, HEAD_DIM, PAGE_SIZE),
                  pl.BlockSpec(memory_space=pl.ANY), pl.BlockSpec(memory_space=pl.ANY),
                  pl.BlockSpec(memory_space=pl.ANY)],
        out_specs=per_seq(tq, ATTN_W),
        scratch_shapes=[pltpu.VMEM((2, n_pages, D_IDX, PAGE_SIZE), F32),
                        pltpu.VMEM((2, n_pages, N_KV_HEADS, HEAD_DIM, PAGE_SIZE), F32),
                        pltpu.VMEM((2, n_pages, N_KV_HEADS, HEAD_DIM, PAGE_SIZE), F32),
                        pltpu.SemaphoreType.DMA((2, 3)),
                        pltpu.VMEM((tq, (n_pages + 1) * PAGE_SIZE), jnp.int32)],
    )
    return pl.pallas_call(
        kern,
        grid_spec=grid_spec,
        out_shape=jax.ShapeDtypeStruct((b, tq, ATTN_W), BF16),
        compiler_params=pltpu.CompilerParams(dimension_semantics=("arbitrary",), vmem_limit_bytes=VMEM_LIMIT),
        name="dsa_paged",
    )(page_table.reshape(-1), iq, iw, q, ikn, kn, vn, cik_t, ck_t, cv_t)


def _rwkv_kernel(pr_ref, sh0_ref, z0_ref, mu_ref, w0_ref, wup_ref, a0_ref, aup_ref, gup_ref, kk_ref, ka_ref,
                 rk_ref, lnw_ref, lnb_ref, ge_ref, get_ref,
                 ob_ref, zout_ref,
                 z_scr, prev_scr, r_s, k_s, v_s, a_s, b_s, lw_s, y_s, *, tt, chunk, t_valid):
    ti = pl.program_id(1)
    n_t = pl.num_programs(1)
    nh, hd = N_RWKV_HEADS, RWKV_HEAD

    @pl.when(ti == 0)
    def _():
        z_scr[...] = z0_ref[0]
        prev_scr[...] = sh0_ref[0]

    gsum = lambda z: _group_sum(z, ge_ref[...], get_ref[...])

    pr = pr_ref[0]
    row = lax.broadcasted_iota(jnp.int32, (tt, 1), 0)
    prev = jnp.where(row == 0, prev_scr[...], pltpu.roll(pr, 1, 0))
    prev_scr[...] = pr[tt - 1:tt]
    m = pr + mu_ref[...] * (prev - pr)
    r = m[:, 0:RWKV_W]
    k = m[:, RWKV_W:2 * RWKV_W]
    v = m[:, 2 * RWKV_W:3 * RWKV_W]
    o = 3 * RWKV_W
    wd = m[:, o:o + W_LORA]
    ad = m[:, o + W_LORA:o + W_LORA + A_LORA]
    gd = m[:, o + W_LORA + A_LORA:]

    lora = lambda z, w_ref: _mm(_split(z), _split(w_ref[...]))
    u = -(w0_ref[...] + lora(jnp.tanh(wd), wup_ref))
    softplus = jnp.maximum(u, 0.0) + jnp.log(1.0 + jnp.exp(-jnp.abs(u)))
    lw = -jnp.exp(-softplus - 0.5)
    a = jax.nn.sigmoid(a0_ref[...] + lora(ad, aup_ref))
    gate = lora(jax.nn.sigmoid(gd), gup_ref)
    kk = k * kk_ref[...]
    kk = kk / jnp.maximum(jnp.sqrt(gsum(kk * kk)), 1e-12)
    k2 = k * (1.0 + (a - 1.0) * ka_ref[...])
    bonus = gsum(r * k2 * rk_ref[...])
    av = -kk
    bv = kk * a
    if t_valid < tt:
        ok = (row < t_valid).astype(F32)
        k2, v, av, bv, lw = k2 * ok, v * ok, av * ok, bv * ok, lw * ok
    r_s[...] = r
    k_s[...] = k2
    v_s[...] = v
    a_s[...] = av
    b_s[...] = bv
    lw_s[...] = lw

    c = chunk
    ri = lax.broadcasted_iota(jnp.int32, (c, c), 0)
    ci = lax.broadcasted_iota(jnp.int32, (c, c), 1)
    tri_incl = (ci <= ri)
    tri_strict = (ci < ri)
    ltri = tri_incl.astype(BF16)
    eye_h = (lax.broadcasted_iota(jnp.int32, (hd, hd), 0) == lax.broadcasted_iota(jnp.int32, (hd, hd), 1))
    n_double = max(int(np.ceil(np.log2(c))), 1)
    heads = range(nh)
    hsl = [slice(h * hd, (h + 1) * hd) for h in heads]

    def chunk_body(ci_, carry):
        s0 = pl.multiple_of(ci_ * c, c)
        sl = pl.ds(s0, c)
        lwc = lw_s[sl, :]
        cum = sum(_dot(ltri, part) for part in _split(lwc, 3))
        cum_end = cum[c - 1:c, :]
        g_end = jnp.exp(cum_end)
        at = a_s[sl, :] * jnp.exp(cum - lwc)
        rt = r_s[sl, :] * jnp.exp(cum)
        g_inv = jnp.exp(-cum)
        g_tail = jnp.exp(cum_end - cum)
        bt = b_s[sl, :] * g_inv
        kt = k_s[sl, :] * g_inv
        bc = b_s[sl, :] * g_tail
        kc = k_s[sl, :] * g_tail
        vc = v_s[sl, :]

        left = [_split(jnp.concatenate([at[:, s], rt[:, s]], axis=0)) for s in hsl]
        right = [_split(jnp.concatenate([bt[:, s], kt[:, s]], axis=0)) for s in hsl]
        amat = [_mm(left[h], right[h], _dot_nt) for h in heads]
        a_ab = [jnp.where(tri_strict, amat[h][:c, :c], 0.0) for h in heads]
        a_ak = [_split(jnp.where(tri_strict, amat[h][:c, c:], 0.0)) for h in heads]
        a_rb = [_split(jnp.where(tri_incl, amat[h][c:, :c], 0.0)) for h in heads]
        a_rk = [_split(jnp.where(tri_incl, amat[h][c:, c:], 0.0)) for h in heads]
        vh = [_split(vc[:, s]) for s in hsl]
        akv = [_mm(a_ak[h], vh[h]) for h in heads]
        uu = [jnp.concatenate([at[:, hsl[h]], akv[h]], axis=1) for h in heads]
        pw = a_ab
        for step in range(n_double):
            pws = [_split(z) for z in pw]
            uus = [_split(z) for z in uu]
            uu = [uu[h] + _mm(pws[h], uus[h]) for h in heads]
            if step + 1 < n_double:
                pw = [_mm(pws[h], pws[h]) for h in heads]
        uus = [_split(z) for z in uu]
        x1 = [_mm(a_rb[h], uus[h]) for h in heads]
        x2 = [_mm(a_rk[h], vh[h]) for h in heads]
        mn = [_mm(_split(bc[:, hsl[h]]), uus[h], _dot_tn) for h in heads]
        nk = [_mm(_split(kc[:, hsl[h]]), vh[h], _dot_tn) for h in heads]
        pm = []
        for h in heads:
            p2 = rt[:, hsl[h]] + x1[h][:, :hd]
            mh = jnp.where(eye_h, g_end[:, hsl[h]], 0.0) + mn[h][:, :hd]
            pm.append(_split(jnp.concatenate([p2, mh], axis=0)))
        res = [_mm(pm[h], _split(z_scr[h])) for h in heads]
        for h in heads:
            y_s[sl, hsl[h]] = res[h][:c] + x1[h][:, hd:] + x2[h]
            z_scr[h] = res[h][c:] + mn[h][:, hd:] + nk[h]
        return carry

    lax.fori_loop(0, tt // c, chunk_body, 0)

    y = y_s[...]
    mean = gsum(y) * (1.0 / hd)
    dlt = y - mean
    var = gsum(dlt * dlt) * (1.0 / hd)
    yn = dlt * lax.rsqrt(var + LNX_EPS) * lnw_ref[...] + lnb_ref[...]
    ob_ref[0] = ((yn + bonus * v_s[...]) * gate).astype(ob_ref.dtype)

    @pl.when(ti == n_t - 1)
    def _():
        zout_ref[0] = z_scr[...]


def _rwkv(pr, shift0, z0, p, *, tt, chunk, t_valid):
    b, t, _ = pr.shape
    assert t % tt == 0 and tt % chunk == 0
    consts = (p["shift_mu"], p["w0"], p["w_lora_up"], p["a0"], p["a_lora_up"], p["g_lora_up"], p["k_k"], p["k_a"],
              p["r_k"], p["ln_x_w"], p["ln_x_b"], p["ge"], p["get"])
    kern = functools.partial(_rwkv_kernel, tt=tt, chunk=chunk, t_valid=t_valid)
    wide = lambda: pltpu.VMEM((tt, RWKV_W), F32)
    return pl.pallas_call(
        kern,
        grid=(b, t // tt),
        in_specs=[
            pl.BlockSpec((1, tt, RWKV_PROJ_W), lambda bi, ti: (bi, ti, 0)),
            pl.BlockSpec((1, 1, RWKV_PROJ_W), lambda bi, ti: (bi, 0, 0)),
            pl.BlockSpec((1, N_RWKV_HEADS, RWKV_HEAD, RWKV_HEAD), lambda bi, ti: (bi, 0, 0, 0)),
        ] + [_const_spec(c.shape) for c in consts],
        out_specs=[
            pl.BlockSpec((1, tt, RWKV_W), lambda bi, ti: (bi, ti, 0)),
            pl.BlockSpec((1, N_RWKV_HEADS, RWKV_HEAD, RWKV_HEAD), lambda bi, ti: (bi, 0, 0, 0)),
        ],
        out_shape=[jax.ShapeDtypeStruct((b, t, RWKV_W), BF16),
                   jax.ShapeDtypeStruct((b, N_RWKV_HEADS, RWKV_HEAD, RWKV_HEAD), F32)],
        scratch_shapes=[pltpu.VMEM((N_RWKV_HEADS, RWKV_HEAD, RWKV_HEAD), F32),
                        pltpu.VMEM((1, RWKV_PROJ_W), F32),
                        wide(), wide(), wide(), wide(), wide(), wide(), wide()],
        compiler_params=pltpu.CompilerParams(dimension_semantics=("parallel", "arbitrary"),
                                             vmem_limit_bytes=VMEM_LIMIT),
        name="rwkv",
    )(pr, shift0, z0, *consts)


def _merge_kernel(x_ref, oa_ref, ob_ref, ga_ref, gb_ref, wa_ref, wb_ref, wo_ref, nf_ref, rwt_ref, rb_ref,
                  h_out, hn_out, idx_out, gate_out):
    ma = _dot(oa_ref[...], wa_ref[...])
    mb = _dot(ob_ref[...], wb_ref[...])
    mm = ga_ref[...].astype(F32) * ma + gb_ref[...].astype(F32) * mb
    h = x_ref[...] + _dot(mm.astype(BF16), wo_ref[...])
    h_out[...] = h
    ms = jnp.mean(h * h, axis=-1, keepdims=True)
    hn = h * lax.rsqrt(ms + NORM_EPS) * nf_ref[...]
    hn_out[...] = hn.astype(hn_out.dtype)
    logits = _mm(_split(rwt_ref[...]), _split(hn), _dot_nt) + rb_ref[...]
    tm = logits.shape[1]
    expert = lax.broadcasted_iota(jnp.int32, logits.shape, 0)
    vals, idxs = [], []
    for _ in range(TOP_K):
        mx = jnp.max(logits, axis=0, keepdims=True)
        ix = jnp.min(jnp.where(logits == mx, expert, N_EXPERTS), axis=0, keepdims=True)
        vals.append(mx)
        idxs.append(ix)
        logits = jnp.where(expert == ix, -jnp.inf, logits)
    es = [jnp.exp(v - vals[0]) for v in vals]
    den = es[0] + es[1] + es[2] + es[3]
    pad = 8 - TOP_K
    idx_out[...] = jnp.concatenate(idxs + [jnp.zeros((pad, tm), jnp.int32)], axis=0)
    gate_out[...] = jnp.concatenate([e / den for e in es] + [jnp.zeros((pad, tm), F32)], axis=0)


def _merge(x2d, oa, ob, ga, gb, p, tm):
    n = x2d.shape[0]
    row = lambda w: pl.BlockSpec((tm, w), lambda i: (i, 0))
    consts = (p["w_proj_a"], p["w_proj_b"], p["w_out"], p["norm_ffn"], p["router_w"], p["router_b"])
    return pl.pallas_call(
        _merge_kernel,
        grid=(n // tm,),
        in_specs=[row(D_MODEL), row(ATTN_W), row(RWKV_W), row(D_MODEL), row(D_MODEL)]
        + [_const_spec(c.shape) for c in consts],
        out_specs=[row(D_MODEL), row(D_MODEL)] + [pl.BlockSpec((8, tm), lambda i: (0, i))] * 2,
        out_shape=[jax.ShapeDtypeStruct((n, D_MODEL), F32), jax.ShapeDtypeStruct((n, D_MODEL), BF16),
                   jax.ShapeDtypeStruct((8, n), jnp.int32), jax.ShapeDtypeStruct((8, n), F32)],
        compiler_params=pltpu.CompilerParams(dimension_semantics=("parallel",), vmem_limit_bytes=VMEM_LIMIT),
        name="merge",
    )(x2d, oa, ob, ga, gb, *consts)


def _moe_kernel(be_ref, nb_ref, x_ref, wgu_ref, bgu_ref, wd_ref, bd_ref, o_ref, wgu_s, wd_s):
    i = pl.program_id(0)
    used = i < nb_ref[0]

    @pl.when(used & ((i == 0) | (be_ref[i] != be_ref[jnp.maximum(i - 1, 0)])))
    def _():
        wgu_s[...] = wgu_ref[0].astype(wgu_s.dtype)
        wd_s[...] = wd_ref[0].astype(wd_s.dtype)

    @pl.when(used)
    def _():
        hcat = _dot(x_ref[...], wgu_s[...]) + bgu_ref[0]
        glu = jnp.minimum(hcat[:, :D_FF], SWIGLU_LIMIT)
        lin = jnp.clip(hcat[:, D_FF:], -SWIGLU_LIMIT, SWIGLU_LIMIT)
        act = glu * jax.nn.sigmoid(SWIGLU_ALPHA * glu) * (lin + 1.0)
        o_ref[...] = _dot(act.astype(BF16), wd_s[...]) + bd_ref[0]

    @pl.when(jnp.logical_not(used))
    def _():
        o_ref[...] = jnp.zeros(o_ref.shape, o_ref.dtype)


def _moe_rows(xg, block_exp, n_used, p, bm):
    rows = xg.shape[0]
    grid_spec = pltpu.PrefetchScalarGridSpec(
        num_scalar_prefetch=2,
        grid=(rows // bm,),
        in_specs=[
            pl.BlockSpec((bm, D_MODEL), lambda i, be, nb: (i, 0)),
            pl.BlockSpec((1, D_MODEL, 2 * D_FF), lambda i, be, nb: (be[i], 0, 0)),
            pl.BlockSpec((1, 1, 2 * D_FF), lambda i, be, nb: (be[i], 0, 0)),
            pl.BlockSpec((1, D_FF, D_MODEL), lambda i, be, nb: (be[i], 0, 0)),
            pl.BlockSpec((1, 1, D_MODEL), lambda i, be, nb: (be[i], 0, 0)),
        ],
        out_specs=pl.BlockSpec((bm, D_MODEL), lambda i, be, nb: (i, 0)),
        scratch_shapes=[pltpu.VMEM((D_MODEL, 2 * D_FF), BF16), pltpu.VMEM((D_FF, D_MODEL), BF16)],
    )
    return pl.pallas_call(
        _moe_kernel,
        grid_spec=grid_spec,
        out_shape=jax.ShapeDtypeStruct((rows, D_MODEL), F32),
        compiler_params=pltpu.CompilerParams(dimension_semantics=("arbitrary",), vmem_limit_bytes=VMEM_LIMIT),
        name="moe",
    )(block_exp, n_used, xg, p["w_gate_up"], p["b_gate_up"], p["w_down"], p["b_down"])


def _moe(hn, top_idx, gate, p, bm):
    n = hn.shape[0]
    nk = n * TOP_K
    e_flat = top_idx.reshape(-1)
    onehot = (e_flat[:, None] == jnp.arange(N_EXPERTS, dtype=jnp.int32)[None, :]).astype(jnp.int32)
    csum = jnp.cumsum(onehot, axis=0)
    counts = csum[-1]
    rank = jnp.take_along_axis(csum, e_flat[:, None], axis=1)[:, 0] - 1
    padded = (counts + bm - 1) // bm * bm
    pends = jnp.cumsum(padded)
    pstarts = pends - padded
    dest = pstarts[e_flat] + rank
    n_blocks = -(-nk // bm) + N_EXPERTS
    rows = n_blocks * bm
    block_start = jnp.arange(n_blocks, dtype=jnp.int32) * bm
    block_exp = jnp.minimum(jnp.sum((pends[None, :] <= block_start[:, None]).astype(jnp.int32), axis=1),
                            N_EXPERTS - 1)
    order = jnp.argsort(e_flat, stable=True).astype(jnp.int32)
    e_row = jnp.repeat(block_exp, bm)
    j_row = jnp.arange(rows, dtype=jnp.int32) - pstarts[e_row]
    src = jnp.clip((jnp.cumsum(counts) - counts)[e_row] + j_row, 0, nk - 1)
    row_tok = jnp.where(j_row < counts[e_row], order[src] % n, 0)
    n_used = (pends[-1:] // bm).astype(jnp.int32)
    out = _moe_rows(hn[row_tok], block_exp, n_used, p, bm)
    return (out[dest.reshape(TOP_K, n)] * gate[:, :, None]).sum(axis=0)


def _prep_params(norm_mix, w_in, q_norm, k_norm, idx_k_norm, shift_mu, w0, w_lora_up, a0, a_lora_up, g_lora_up, k_k,
                 k_a, r_k, ln_x_w, ln_x_b, w_proj_a, w_proj_b, w_out, norm_ffn, router_w, router_b, w_gate_up,
                 b_gate_up, w_down, b_down):
    splits = (ATTN_W, KV_W, KV_W, IDX_W, D_IDX, N_IDX_HEADS, RWKV_PROJ_W, D_MODEL, D_MODEL)
    cuts = np.cumsum(splits)[:-1].tolist()
    wq, wk, wv, wiq, wik, wiw, wpr, wga, wgb = jnp.split(w_in.astype(BF16), cuts, axis=-1)
    wikw = jnp.concatenate([wik, wiw, jnp.zeros((D_MODEL, LANES - D_IDX - N_IDX_HEADS), BF16)], axis=-1)
    row = lambda z: z.reshape(1, -1).astype(F32)
    ge = _group_indicator(RWKV_W, RWKV_HEAD).astype(BF16)
    ge2 = _group_indicator(KV_W, HEAD_DIM).astype(BF16)
    return dict(
        norm_mix=row(norm_mix), wq=wq, wk=wk, wv=wv, wiq=wiq, wikw=wikw, wpr=wpr, wga=wga, wgb=wgb,
        q_norm_t=row(jnp.tile(q_norm, N_Q_HEADS)), k_norm_t=row(jnp.tile(k_norm, N_KV_HEADS)),
        ik_norm_t=row(idx_k_norm),
        ge=ge, get=ge.T, ge2=ge2, ge2t=ge2.T,
        shift_mu=row(shift_mu), w0=row(w0), w_lora_up=w_lora_up, a0=row(a0), a_lora_up=a_lora_up,
        g_lora_up=g_lora_up, k_k=row(k_k), k_a=row(k_a), r_k=row(r_k), ln_x_w=row(ln_x_w), ln_x_b=row(ln_x_b),
        w_proj_a=w_proj_a.astype(BF16), w_proj_b=w_proj_b.astype(BF16), w_out=w_out.astype(BF16),
        norm_ffn=row(norm_ffn), router_w=router_w.T, router_b=router_b.reshape(-1, 1),
        w_gate_up=w_gate_up, b_gate_up=b_gate_up[:, None, :], w_down=w_down, b_down=b_down[:, None, :],
    )


def _pad_axis(z, axis, size):
    if z.shape[axis] == size:
        return z
    pad = [(0, 0)] * z.ndim
    pad[axis] = (0, size - z.shape[axis])
    return jnp.pad(z, pad)


def _group(x, p, *, tm):
    b, t, _ = x.shape
    names = ("q", "k", "kb", "v", "vb", "iq", "ik", "ikb", "iw", "pr", "ga", "gb")
    g = dict(zip(names, _in_proj(x.reshape(b * t, D_MODEL), p, tm)), b=b, t=t)
    g["pr"] = g["pr"].reshape(b, t, RWKV_PROJ_W)
    return g


def _attend(g, ik_all, k_all, v_all, *, tq, tk, q_offset, top):
    b, t = g["b"], g["t"]
    tp = -(-t // tq) * tq
    n_keys = -(-ik_all.shape[1] // tk) * tk
    seq = lambda z: _pad_axis(z.reshape(b, t, -1), 1, tp)
    keys = lambda z: _pad_axis(z, 1, n_keys)
    oa = _dsa(seq(g["iq"]), seq(g["iw"]), seq(g["q"]), keys(ik_all), keys(k_all), keys(v_all),
              tq=tq, tk=tk, q_offset=q_offset, top=top)
    return oa[:, :t].reshape(b * t, ATTN_W)


def _mix(g, shift0, wkv0, p, *, tt, chunk):
    b, t = g["b"], g["t"]
    tp = -(-t // tt) * tt
    ob, z = _rwkv(_pad_axis(g["pr"], 1, tp), shift0[:, None, :], jnp.swapaxes(wkv0, -1, -2), p,
                  tt=tt, chunk=chunk, t_valid=min(t, tt) if tp != t else tt)
    return ob[:, :t].reshape(b * t, RWKV_W), jnp.swapaxes(z, -1, -2)


def kernel(x_prompt, x_sample, cache_k, cache_v, cache_idx_k, page_table, state_wkv, state_shift, norm_mix, w_in, q_norm, k_norm, idx_k_norm, shift_mu, w0, w_lora_up, a0, a_lora_up, g_lora_up, k_k, k_a, r_k, ln_x_w, ln_x_b, w_proj_a, w_proj_b, w_out, norm_ffn, router_w, router_b, w_gate_up, b_gate_up, w_down, b_down):
    depth = norm_mix.shape[0]
    assert depth == 1
    params = (norm_mix, w_in, q_norm, k_norm, idx_k_norm, shift_mu, w0, w_lora_up, a0, a_lora_up, g_lora_up, k_k, k_a,
              r_k, ln_x_w, ln_x_b, w_proj_a, w_proj_b, w_out, norm_ffn, router_w, router_b, w_gate_up, b_gate_up,
              w_down, b_down)
    p = _prep_params(*[z[0] for z in params])
    bp, sp, _ = x_prompt.shape
    bs, ts, _ = x_sample.shape
    n_p, n_s = bp * sp, bs * ts
    past = page_table.shape[1] * PAGE_SIZE

    gp = _group(x_prompt, p, tm=min(256, n_p))
    k_p = gp["k"].reshape(bp, sp, N_KV_HEADS, HEAD_DIM)
    v_p = gp["v"].reshape(bp, sp, N_KV_HEADS, HEAD_DIM)
    ik_p = gp["ik"].reshape(bp, sp, D_IDX)
    oa_p = _attend(gp, gp["ikb"].reshape(bp, sp, D_IDX), gp["kb"].reshape(bp, sp, KV_W),
                   gp["vb"].reshape(bp, sp, KV_W), tq=min(128, sp), tk=min(512, sp), q_offset=0,
                   top=min(TOPK_MAX, sp // 4))
    ob_p, wkv_p = _mix(gp, jnp.zeros((bp, RWKV_PROJ_W), F32),
                       jnp.zeros((bp, N_RWKV_HEADS, RWKV_HEAD, RWKV_HEAD), F32), p,
                       tt=min(256, sp), chunk=min(64, sp))

    gs = _group(x_sample, p, tm=min(256, n_s))
    k_s = gs["k"].reshape(bs, ts, N_KV_HEADS, HEAD_DIM)
    v_s = gs["v"].reshape(bs, ts, N_KV_HEADS, HEAD_DIM)
    ik_s = gs["ik"].reshape(bs, ts, D_IDX)
    tq_s = 16
    seq = lambda z: _pad_axis(z.reshape(bs, ts, -1), 1, tq_s)
    new_t = lambda z, *hd: _pad_axis(jnp.moveaxis(z.reshape((bs, ts) + hd), 1, -1), len(hd) + 1, PAGE_SIZE)
    oa_s = _dsa_paged(seq(gs["iq"]), seq(gs["iw"]), seq(gs["q"]),
                      new_t(gs["ikb"], D_IDX), new_t(gs["kb"], N_KV_HEADS, HEAD_DIM),
                      new_t(gs["vb"], N_KV_HEADS, HEAD_DIM),
                      jnp.transpose(cache_idx_k[0], (0, 2, 1)), jnp.transpose(cache_k[0], (0, 2, 3, 1)),
                      jnp.transpose(cache_v[0], (0, 2, 3, 1)), page_table,
                      tq=tq_s, top=min(TOPK_MAX, (past + ts) // 4))[:, :ts].reshape(n_s, ATTN_W)
    ob_s, wkv_s = _mix(gs, state_shift[0], state_wkv[0], p, tt=8, chunk=8)

    h_p, hn_p, idx_p, gate_p = _merge(x_prompt.reshape(n_p, D_MODEL), oa_p, ob_p, gp["ga"], gp["gb"], p,
                                      tm=min(256, n_p))
    h_s, hn_s, idx_s, gate_s = _merge(x_sample.reshape(n_s, D_MODEL), oa_s, ob_s, gs["ga"], gs["gb"], p,
                                      tm=min(256, n_s))
    hn = jnp.concatenate([hn_p, hn_s], axis=0)
    top_idx = jnp.concatenate([idx_p[:TOP_K], idx_s[:TOP_K]], axis=1)
    gate = jnp.concatenate([gate_p[:TOP_K], gate_s[:TOP_K]], axis=1)
    f = _moe(hn, top_idx, gate, p, bm=512)
    y_p = (h_p + f[:n_p]).reshape(bp, sp, D_MODEL)
    y_s = (h_s + f[n_p:]).reshape(bs, ts, D_MODEL)

    st = lambda z: z[None]
    return (y_p, y_s, st(k_p), st(v_p), st(ik_p), st(wkv_p), st(gp["pr"][:, -1]),
            st(k_s), st(v_s), st(ik_s), st(wkv_s), st(gs["pr"][:, -1]))
```

```python
import functools

import jax
import jax.numpy as jnp
import numpy as np
from jax import lax
from jax.experimental import pallas as pl
from jax.experimental.pallas import tpu as pltpu

D_MODEL = 1024
PAGE_SIZE = 128
HEAD_DIM = 64
N_Q_HEADS = 8
N_KV_HEADS = 2
Q_PER_KV = N_Q_HEADS // N_KV_HEADS
ATTN_W = N_Q_HEADS * HEAD_DIM
KV_W = N_KV_HEADS * HEAD_DIM
ATTN_SCALE = HEAD_DIM ** -0.5
N_IDX_HEADS = 8
D_IDX = 64
IDX_W = N_IDX_HEADS * D_IDX
IDX_SCALE = (N_IDX_HEADS * D_IDX) ** -0.5
TOPK_MAX = 256
RWKV_HEAD = 64
N_RWKV_HEADS = 8
RWKV_W = N_RWKV_HEADS * RWKV_HEAD
W_LORA = 64
A_LORA = 64
G_LORA = 128
RWKV_PROJ_W = 3 * RWKV_W + W_LORA + A_LORA + G_LORA
LNX_EPS = 64e-5
N_EXPERTS = 32
TOP_K = 4
D_FF = 1024
SWIGLU_LIMIT = 7.0
SWIGLU_ALPHA = 1.702
NORM_EPS = 1e-6

LANES = 128
VMEM_LIMIT = 56 * 1024 * 1024
INT_MIN = -(2 ** 31)
NEG_BIG = -1e30
F32 = jnp.float32
BF16 = jnp.bfloat16
HI = lax.Precision.HIGHEST


def _dot(a, b, precision=None):
    return jnp.dot(a, b, preferred_element_type=F32, precision=precision)


def _dot_nt(a, b, precision=None):
    return lax.dot_general(a, b, (((1,), (1,)), ((), ())), preferred_element_type=F32, precision=precision)


def _dot_tn(a, b, precision=None):
    return lax.dot_general(a, b, (((0,), (0,)), ((), ())), preferred_element_type=F32, precision=precision)


def _group_indicator(width, group):
    r = np.arange(width) // group
    return jnp.asarray((r[:, None] == np.arange(LANES)[None, :]).astype(np.float32))


def _split(x, terms=2):
    parts = []
    for _ in range(terms - 1):
        hi = x.astype(BF16)
        parts.append(hi)
        x = x - hi.astype(F32)
    parts.append(x.astype(BF16))
    return parts


def _mm(a, b, dot=None):
    dot = dot or _dot
    return dot(a[0], b[0]) + (dot(a[0], b[1]) + dot(a[1], b[0]))


def _group_sum(x, ge, get):
    s = sum(_dot(part, ge) for part in _split(x))
    return sum(_dot(part, get) for part in _split(s))


def _const_spec(shape):
    nd = len(shape)
    return pl.BlockSpec(shape, lambda *_: (0,) * nd)


def _inproj_kernel(x_ref, g_ref, wq_ref, wk_ref, wv_ref, wiq_ref, wikw_ref, wpr_ref, wga_ref, wgb_ref,
                   qn_ref, kn_ref, ikn_ref, ge_ref, get_ref, ge2_ref, ge2t_ref,
                   q_out, k_out, kb_out, v_out, vb_out, iq_out, ik_out, ikb_out, iw_out, pr_out, ga_out, gb_out):
    x = x_ref[...]
    ms = jnp.mean(x * x, axis=-1, keepdims=True)
    xn = (x * lax.rsqrt(ms + NORM_EPS) * g_ref[...]).astype(BF16)

    q = _dot(xn, wq_ref[...])
    qs = _group_sum(q * q, ge_ref[...], get_ref[...]) * (1.0 / HEAD_DIM)
    q_out[...] = (q * lax.rsqrt(qs + NORM_EPS) * qn_ref[...]).astype(q_out.dtype)

    k = _dot(xn, wk_ref[...])
    ks = _group_sum(k * k, ge2_ref[...], ge2t_ref[...]) * (1.0 / HEAD_DIM)
    kn = k * lax.rsqrt(ks + NORM_EPS) * kn_ref[...]
    for g in range(N_KV_HEADS):
        k_out[:, g, :] = kn[:, g * HEAD_DIM:(g + 1) * HEAD_DIM]
    kb_out[...] = kn.astype(kb_out.dtype)

    v = _dot(xn, wv_ref[...])
    for g in range(N_KV_HEADS):
        v_out[:, g, :] = v[:, g * HEAD_DIM:(g + 1) * HEAD_DIM]
    vb_out[...] = v.astype(vb_out.dtype)
    iq_out[...] = _dot(xn, wiq_ref[...]).astype(iq_out.dtype)

    ikw = _dot(xn, wikw_ref[...])
    lane = lax.broadcasted_iota(jnp.int32, ikw.shape, 1)
    iks = jnp.sum(jnp.where(lane < D_IDX, ikw * ikw, 0.0), axis=-1, keepdims=True) * (1.0 / D_IDX)
    ikn = ikw[:, :D_IDX] * lax.rsqrt(iks + NORM_EPS) * ikn_ref[...]
    ik_out[...] = ikn
    ikb_out[...] = ikn.astype(ikb_out.dtype)
    iw_out[...] = ikw[:, D_IDX:D_IDX + N_IDX_HEADS]

    pr_out[...] = _dot(xn, wpr_ref[...])
    ga_out[...] = jax.nn.sigmoid(_dot(xn, wga_ref[...])).astype(ga_out.dtype)
    gb_out[...] = jax.nn.sigmoid(_dot(xn, wgb_ref[...])).astype(gb_out.dtype)


def _in_proj(x2d, p, tm):
    n = x2d.shape[0]
    widths = (ATTN_W, KV_W, KV_W, KV_W, KV_W, IDX_W, D_IDX, D_IDX, N_IDX_HEADS, RWKV_PROJ_W, D_MODEL, D_MODEL)
    dtypes = (BF16, F32, BF16, F32, BF16, BF16, F32, BF16, F32, F32, BF16, BF16)
    row = lambda w: pl.BlockSpec((tm, w), lambda i: (i, 0))
    heads_spec = pl.BlockSpec((tm, N_KV_HEADS, HEAD_DIM), lambda i: (i, 0, 0))
    consts = (p["norm_mix"], p["wq"], p["wk"], p["wv"], p["wiq"], p["wikw"], p["wpr"], p["wga"], p["wgb"],
              p["q_norm_t"], p["k_norm_t"], p["ik_norm_t"], p["ge"], p["get"], p["ge2"], p["ge2t"])
    return pl.pallas_call(
        _inproj_kernel,
        grid=(n // tm,),
        in_specs=[row(D_MODEL)] + [_const_spec(c.shape) for c in consts],
        out_specs=[heads_spec if i in (1, 3) else row(w) for i, w in enumerate(widths)],
        out_shape=[jax.ShapeDtypeStruct((n, N_KV_HEADS, HEAD_DIM) if i in (1, 3) else (n, w), d)
                   for i, (w, d) in enumerate(zip(widths, dtypes))],
        compiler_params=pltpu.CompilerParams(dimension_semantics=("parallel",), vmem_limit_bytes=VMEM_LIMIT),
        name="in_proj",
    )(x2d, *consts)


def _stack_heads(x, heads, width):
    return jnp.concatenate([x[:, h * width:(h + 1) * width] for h in heads], axis=0)


def _score_keys(d, iw, tq):
    acc = jnp.zeros((tq, d.shape[1]), F32)
    for h in range(N_IDX_HEADS):
        acc = acc + jnp.maximum(d[h * tq:(h + 1) * tq], 0.0) * iw[:, h:h + 1]
    sc = acc * IDX_SCALE
    sc = jnp.where(sc == 0.0, 0.0, sc)
    bits = pltpu.bitcast(sc, jnp.int32)
    return bits ^ ((bits >> 31) & 0x7FFFFFFF)


def _select_threshold(keys_ref, n_kt, *, tq, tk, top):
    lane_pos = lax.broadcasted_iota(jnp.int32, (tq, tk), 1)

    def count(pred):
        def body(kt, c):
            start = pl.multiple_of(kt * tk, tk)
            m = pred(keys_ref[:, pl.ds(start, tk)], start).astype(F32)
            part = m[:, 0:LANES]
            for j in range(1, tk // LANES):
                part = part + m[:, j * LANES:(j + 1) * LANES]
            return c + part
        c = lax.fori_loop(0, n_kt, body, jnp.zeros((tq, LANES), F32))
        return jnp.sum(c, axis=-1, keepdims=True)

    c0 = count(lambda k, s: k >= 0)
    t = jnp.where(c0 >= top, 0, INT_MIN).astype(jnp.int32)
    c_t = jnp.where(c0 >= top, c0, 0)

    def bit_step(i, carry):
        t, c_t = carry
        cand = t | jnp.left_shift(jnp.int32(1), 30 - i)
        c = count(lambda k, s: k >= cand)
        return jnp.where(c >= top, cand, t), jnp.where(c >= top, c, c_t)

    t, c_t = lax.fori_loop(0, 31, bit_step, (t, c_t))
    t = jnp.maximum(t, INT_MIN + 1)

    excess = c_t > top

    @pl.when(jnp.max(excess.astype(jnp.int32)) > 0)
    def _():
        keep = top - count(lambda k, s: k > t)

        def idx_step(i, lim):
            cand = lim | jnp.left_shift(jnp.int32(1), 14 - i)
            c = count(lambda k, s: ((k == t) & (s + lane_pos < cand)))
            return jnp.where(c <= keep, cand, lim)

        lim = lax.fori_loop(0, 15, idx_step, jnp.zeros((tq, 1), jnp.int32))

        def demote(kt, carry):
            start = pl.multiple_of(kt * tk, tk)
            k = keys_ref[:, pl.ds(start, tk)]
            drop = (k == t) & (start + lane_pos >= lim) & excess
            keys_ref[:, pl.ds(start, tk)] = jnp.where(drop, t - 1, k)
            return carry

        lax.fori_loop(0, n_kt, demote, 0)

    return t


def _softmax_step(state, s, bias, pv, tq):
    m_i, l_i, acc = state
    gq, tk = s.shape
    s = (s.reshape(Q_PER_KV, tq, tk) + bias[None]).reshape(gq, tk)
    m_n = jnp.maximum(m_i, jnp.max(s, axis=-1, keepdims=True))
    alpha = jnp.exp(m_i - m_n)
    pm = jnp.exp(s - m_n)
    l_n = alpha * l_i + jnp.sum(pm, axis=-1, keepdims=True)
    return m_n, l_n, alpha * acc + pv(pm.astype(BF16))


def _softmax_init(tq):
    gq = Q_PER_KV * tq
    return tuple((jnp.full((gq, 1), NEG_BIG, F32), jnp.zeros((gq, 1), F32), jnp.zeros((gq, HEAD_DIM), F32))
                 for _ in range(N_KV_HEADS))


def _write_heads(o_ref, res, tq):
    for g in range(N_KV_HEADS):
        _, l_i, acc = res[g]
        og = acc / l_i
        for j in range(Q_PER_KV):
            h = g * Q_PER_KV + j
            o_ref[0, :, h * HEAD_DIM:(h + 1) * HEAD_DIM] = og[j * tq:(j + 1) * tq].astype(o_ref.dtype)


def _dsa_kernel(iq_ref, iw_ref, q_ref, ik_ref, k_ref, v_ref, o_ref, keys_ref, *, tq, tk, q_offset, n_keys, top):
    qi = pl.program_id(1)
    q_base = q_offset + qi * tq
    n_kt = jnp.minimum((q_base + tq + tk - 1) // tk, n_keys // tk)
    q_pos = q_base + lax.broadcasted_iota(jnp.int32, (tq, 1), 0)
    lane_pos = lax.broadcasted_iota(jnp.int32, (tq, tk), 1)

    iq = _stack_heads(iq_ref[0], range(N_IDX_HEADS), D_IDX)
    iw = iw_ref[0]

    def score_tile(kt, carry):
        start = pl.multiple_of(kt * tk, tk)
        key = _score_keys(_dot_nt(iq, ik_ref[0, pl.ds(start, tk), :]), iw, tq)
        keys_ref[:, pl.ds(start, tk)] = jnp.where(start + lane_pos <= q_pos, key, INT_MIN)
        return carry

    lax.fori_loop(0, n_kt, score_tile, 0)

    t = _select_threshold(keys_ref, n_kt, tq=tq, tk=tk, top=top)

    q_all = q_ref[0] * ATTN_SCALE
    qs = [_stack_heads(q_all, range(g * Q_PER_KV, (g + 1) * Q_PER_KV), HEAD_DIM) for g in range(N_KV_HEADS)]

    def attn_tile(kt, carry):
        start = pl.multiple_of(kt * tk, tk)
        bias = jnp.where(keys_ref[:, pl.ds(start, tk)] >= t, 0.0, NEG_BIG)
        k_t = k_ref[0, pl.ds(start, tk), :]
        v_t = v_ref[0, pl.ds(start, tk), :]
        new = []
        for g in range(N_KV_HEADS):
            hs = slice(g * HEAD_DIM, (g + 1) * HEAD_DIM)
            new.append(_softmax_step(carry[g], _dot_nt(qs[g], k_t[:, hs]), bias,
                                     lambda p, hs=hs: _dot(p, v_t[:, hs]), tq))
        return tuple(new)

    _write_heads(o_ref, lax.fori_loop(0, n_kt, attn_tile, _softmax_init(tq)), tq)


def _dsa(iq, iw, q, ik, k, v, *, tq, tk, q_offset, top):
    b, sq, _ = iq.shape
    n_keys = ik.shape[1]
    assert sq % tq == 0 and n_keys % tk == 0 and tk % LANES == 0
    kern = functools.partial(_dsa_kernel, tq=tq, tk=tk, q_offset=q_offset, n_keys=n_keys, top=top)
    return pl.pallas_call(
        kern,
        grid=(b, sq // tq),
        in_specs=[
            pl.BlockSpec((1, tq, IDX_W), lambda bi, qi: (bi, qi, 0)),
            pl.BlockSpec((1, tq, N_IDX_HEADS), lambda bi, qi: (bi, qi, 0)),
            pl.BlockSpec((1, tq, ATTN_W), lambda bi, qi: (bi, qi, 0)),
            pl.BlockSpec((1, n_keys, D_IDX), lambda bi, qi: (bi, 0, 0)),
            pl.BlockSpec((1, n_keys, KV_W), lambda bi, qi: (bi, 0, 0)),
            pl.BlockSpec((1, n_keys, KV_W), lambda bi, qi: (bi, 0, 0)),
        ],
        out_specs=pl.BlockSpec((1, tq, ATTN_W), lambda bi, qi: (bi, qi, 0)),
        out_shape=jax.ShapeDtypeStruct((b, sq, ATTN_W), BF16),
        scratch_shapes=[pltpu.VMEM((tq, n_keys), jnp.int32)],
        compiler_params=pltpu.CompilerParams(dimension_semantics=("parallel", "arbitrary"),
                                             vmem_limit_bytes=VMEM_LIMIT),
        name="dsa",
    )(iq, iw, q, ik, k, v)


def _dsa_paged_kernel(pt_ref, iq_ref, iw_ref, q_ref, ikn_ref, kn_ref, vn_ref, cik_ref, ck_ref, cv_ref, o_ref,
                      ik_buf, k_buf, v_buf, sems, keys_ref, *, tq, n_pages, ppt, tk_sel, top):
    b = pl.program_id(0)
    slot = b % 2
    past = n_pages * PAGE_SIZE
    tk = ppt * PAGE_SIZE

    def page_copies(bi, sl, p):
        page = pt_ref[bi * n_pages + p]
        return (pltpu.make_async_copy(cik_ref.at[page], ik_buf.at[sl, p], sems.at[sl, 0]),
                pltpu.make_async_copy(ck_ref.at[page], k_buf.at[sl, p], sems.at[sl, 1]),
                pltpu.make_async_copy(cv_ref.at[page], v_buf.at[sl, p], sems.at[sl, 2]))

    def fetch(bi, sl):
        def body(p, carry):
            for cp in page_copies(bi, sl, p):
                cp.start()
            return carry
        lax.fori_loop(0, n_pages, body, 0)

    @pl.when(b == 0)
    def _():
        fetch(0, 0)

    @pl.when(b + 1 < pl.num_programs(0))
    def _():
        fetch(b + 1, 1 - slot)

    def wait_page(p, carry):
        for cp in page_copies(b, slot, p):
            cp.wait()
        return carry

    lax.fori_loop(0, n_pages, wait_page, 0)

    def page_cols(buf, i, idx=()):
        return jnp.concatenate([buf[(slot, i * ppt + j) + idx] for j in range(ppt)], axis=1).astype(BF16)

    iq = _stack_heads(iq_ref[0], range(N_IDX_HEADS), D_IDX)
    iw = iw_ref[0]
    row = lax.broadcasted_iota(jnp.int32, (tq, PAGE_SIZE), 0)
    lane = lax.broadcasted_iota(jnp.int32, (tq, PAGE_SIZE), 1)

    def score_tile(i, carry):
        start = pl.multiple_of(i * tk, tk)
        keys_ref[:, pl.ds(start, tk)] = _score_keys(_dot(iq, page_cols(ik_buf, i)), iw, tq)
        return carry

    lax.fori_loop(0, n_pages // ppt, score_tile, 0, unroll=2)
    new_keys = _score_keys(_dot(iq, ikn_ref[0]), iw, tq)
    keys_ref[:, past:past + PAGE_SIZE] = jnp.where(lane <= row, new_keys, INT_MIN)

    t = _select_threshold(keys_ref, (past + PAGE_SIZE) // tk_sel, tq=tq, tk=tk_sel, top=top)

    q_all = q_ref[0] * ATTN_SCALE
    qs = [_stack_heads(q_all, range(g * Q_PER_KV, (g + 1) * Q_PER_KV), HEAD_DIM) for g in range(N_KV_HEADS)]

    def attn_tile(i, carry):
        start = pl.multiple_of(i * tk, tk)
        bias = jnp.where(keys_ref[:, pl.ds(start, tk)] >= t, 0.0, NEG_BIG)
        new = []
        for g in range(N_KV_HEADS):
            v_t = page_cols(v_buf, i, (g,))
            new.append(_softmax_step(carry[g], _dot(qs[g], page_cols(k_buf, i, (g,))), bias,
                                     lambda p, v_t=v_t: _dot_nt(p, v_t), tq))
        return tuple(new)

    res = lax.fori_loop(0, n_pages // ppt, attn_tile, _softmax_init(tq), unroll=4)
    bias = jnp.where(keys_ref[:, past:past + PAGE_SIZE] >= t, 0.0, NEG_BIG)
    res = tuple(_softmax_step(res[g], _dot(qs[g], kn_ref[0, g]), bias,
                              lambda p, g=g: _dot_nt(p, vn_ref[0, g]), tq) for g in range(N_KV_HEADS))
    _write_heads(o_ref, res, tq)


def _dsa_paged(iq, iw, q, ikn, kn, vn, cik_t, ck_t, cv_t, page_table, *, tq, top):
    b = iq.shape[0]
    n_pages = page_table.shape[1]
    ppt = next(d for d in (4, 2, 1) if n_pages % d == 0)
    n_lane_tiles = n_pages + 1
    tk_sel = LANES * next(d for d in range(8, 0, -1) if n_lane_tiles % d == 0)
    kern = functools.partial(_dsa_paged_kernel, tq=tq, n_pages=n_pages, ppt=ppt, tk_sel=tk_sel, top=top)
    per_seq = lambda *blk: pl.BlockSpec((1,) + blk, lambda bi, pt: (bi,) + (0,) * len(blk))
    grid_spec = pltpu.PrefetchScalarGridSpec(
        num_scalar_prefetch=1,
        grid=(b,),
        in_specs=[per_seq(tq, IDX_W), per_seq(tq, N_IDX_HEADS), per_seq(tq, ATTN_W),
                  per_seq(D_IDX, PAGE_SIZE), per_seq(N_KV_HEADS, HEAD_DIM, PAGE_SIZE),
                  per_seq(N_KV_HEADS, HEAD_DIM, PAGE_SIZE),
                  pl.BlockSpec(memory_space=pl.ANY), pl.BlockSpec(memory_space=pl.ANY),
                  pl.BlockSpec(memory_space=pl.ANY)],
        out_specs=per_seq(tq, ATTN_W),
        scratch_shapes=[pltpu.VMEM((2, n_pages, D_IDX, PAGE_SIZE), F32),
                        pltpu.VMEM((2, n_pages, N_KV_HEADS, HEAD_DIM, PAGE_SIZE), F32),
                        pltpu.VMEM((2, n_pages, N_KV_HEADS, HEAD_DIM, PAGE_SIZE), F32),
                        pltpu.SemaphoreType.DMA((2, 3)),
                        pltpu.VMEM((tq, (n_pages + 1) * PAGE_SIZE), jnp.int32)],
    )
    return pl.pallas_call(
        kern,
        grid_spec=grid_spec,
        out_shape=jax.ShapeDtypeStruct((b, tq, ATTN_W), BF16),
        compiler_params=pltpu.CompilerParams(dimension_semantics=("arbitrary",), vmem_limit_bytes=VMEM_LIMIT),
        name="dsa_paged",
    )(page_table.reshape(-1), iq, iw, q, ikn, kn, vn, cik_t, ck_t, cv_t)


def _rwkv_kernel(pr_ref, sh0_ref, z0_ref, mu_ref, w0_ref, wup_ref, a0_ref, aup_ref, gup_ref, kk_ref, ka_ref,
                 rk_ref, lnw_ref, lnb_ref, ge_ref, get_ref,
                 ob_ref, zout_ref,
                 z_scr, prev_scr, r_s, k_s, v_s, a_s, b_s, lw_s, y_s, *, tt, chunk, t_valid):
    ti = pl.program_id(1)
    n_t = pl.num_programs(1)
    nh, hd = N_RWKV_HEADS, RWKV_HEAD

    @pl.when(ti == 0)
    def _():
        z_scr[...] = z0_ref[0]
        prev_scr[...] = sh0_ref[0]

    gsum = lambda z: _group_sum(z, ge_ref[...], get_ref[...])

    pr = pr_ref[0]
    row = lax.broadcasted_iota(jnp.int32, (tt, 1), 0)
    prev = jnp.where(row == 0, prev_scr[...], pltpu.roll(pr, 1, 0))
    prev_scr[...] = pr[tt - 1:tt]
    m = pr + mu_ref[...] * (prev - pr)
    r = m[:, 0:RWKV_W]
    k = m[:, RWKV_W:2 * RWKV_W]
    v = m[:, 2 * RWKV_W:3 * RWKV_W]
    o = 3 * RWKV_W
    wd = m[:, o:o + W_LORA]
    ad = m[:, o + W_LORA:o + W_LORA + A_LORA]
    gd = m[:, o + W_LORA + A_LORA:]

    lora = lambda z, w_ref: _mm(_split(z), _split(w_ref[...]))
    u = -(w0_ref[...] + lora(jnp.tanh(wd), wup_ref))
    softplus = jnp.maximum(u, 0.0) + jnp.log(1.0 + jnp.exp(-jnp.abs(u)))
    lw = -jnp.exp(-softplus - 0.5)
    a = jax.nn.sigmoid(a0_ref[...] + lora(ad, aup_ref))
    gate = lora(jax.nn.sigmoid(gd), gup_ref)
    kk = k * kk_ref[...]
    kk = kk / jnp.maximum(jnp.sqrt(gsum(kk * kk)), 1e-12)
    k2 = k * (1.0 + (a - 1.0) * ka_ref[...])
    bonus = gsum(r * k2 * rk_ref[...])
    av = -kk
    bv = kk * a
    if t_valid < tt:
        ok = (row < t_valid).astype(F32)
        k2, v, av, bv, lw = k2 * ok, v * ok, av * ok, bv * ok, lw * ok
    r_s[...] = r
    k_s[...] = k2
    v_s[...] = v
    a_s[...] = av
    b_s[...] = bv
    lw_s[...] = lw

    c = chunk
    ri = lax.broadcasted_iota(jnp.int32, (c, c), 0)
    ci = lax.broadcasted_iota(jnp.int32, (c, c), 1)
    tri_incl = (ci <= ri)
    tri_strict = (ci < ri)
    ltri = tri_incl.astype(BF16)
    eye_h = (lax.broadcasted_iota(jnp.int32, (hd, hd), 0) == lax.broadcasted_iota(jnp.int32, (hd, hd), 1))
    n_double = max(int(np.ceil(np.log2(c))), 1)
    heads = range(nh)
    hsl = [slice(h * hd, (h + 1) * hd) for h in heads]

    def chunk_body(ci_, carry):
        s0 = pl.multiple_of(ci_ * c, c)
        sl = pl.ds(s0, c)
        lwc = lw_s[sl, :]
        cum = sum(_dot(ltri, part) for part in _split(lwc, 3))
        cum_end = cum[c - 1:c, :]
        g_end = jnp.exp(cum_end)
        at = a_s[sl, :] * jnp.exp(cum - lwc)
        rt = r_s[sl, :] * jnp.exp(cum)
        g_inv = jnp.exp(-cum)
        g_tail = jnp.exp(cum_end - cum)
        bt = b_s[sl, :] * g_inv
        kt = k_s[sl, :] * g_inv
        bc = b_s[sl, :] * g_tail
        kc = k_s[sl, :] * g_tail
        vc = v_s[sl, :]

        left = [_split(jnp.concatenate([at[:, s], rt[:, s]], axis=0)) for s in hsl]
        right = [_split(jnp.concatenate([bt[:, s], kt[:, s]], axis=0)) for s in hsl]
        amat = [_mm(left[h], right[h], _dot_nt) for h in heads]
        a_ab = [jnp.where(tri_strict, amat[h][:c, :c], 0.0) for h in heads]
        a_ak = [_split(jnp.where(tri_strict, amat[h][:c, c:], 0.0)) for h in heads]
        a_rb = [_split(jnp.where(tri_incl, amat[h][c:, :c], 0.0)) for h in heads]
        a_rk = [_split(jnp.where(tri_incl, amat[h][c:, c:], 0.0)) for h in heads]
        vh = [_split(vc[:, s]) for s in hsl]
        akv = [_mm(a_ak[h], vh[h]) for h in heads]
        uu = [jnp.concatenate([at[:, hsl[h]], akv[h]], axis=1) for h in heads]
        pw = a_ab
        for step in range(n_double):
            pws = [_split(z) for z in pw]
            uus = [_split(z) for z in uu]
            uu = [uu[h] + _mm(pws[h], uus[h]) for h in heads]
            if step + 1 < n_double:
                pw = [_mm(pws[h], pws[h]) for h in heads]
        uus = [_split(z) for z in uu]
        x1 = [_mm(a_rb[h], uus[h]) for h in heads]
        x2 = [_mm(a_rk[h], vh[h]) for h in heads]
        mn = [_mm(_split(bc[:, hsl[h]]), uus[h], _dot_tn) for h in heads]
        nk = [_mm(_split(kc[:, hsl[h]]), vh[h], _dot_tn) for h in heads]
        pm = []
        for h in heads:
            p2 = rt[:, hsl[h]] + x1[h][:, :hd]
            mh = jnp.where(eye_h, g_end[:, hsl[h]], 0.0) + mn[h][:, :hd]
            pm.append(_split(jnp.concatenate([p2, mh], axis=0)))
        res = [_mm(pm[h], _split(z_scr[h])) for h in heads]
        for h in heads:
            y_s[sl, hsl[h]] = res[h][:c] + x1[h][:, hd:] + x2[h]
            z_scr[h] = res[h][c:] + mn[h][:, hd:] + nk[h]
        return carry

    lax.fori_loop(0, tt // c, chunk_body, 0)

    y = y_s[...]
    mean = gsum(y) * (1.0 / hd)
    dlt = y - mean
    var = gsum(dlt * dlt) * (1.0 / hd)
    yn = dlt * lax.rsqrt(var + LNX_EPS) * lnw_ref[...] + lnb_ref[...]
    ob_ref[0] = ((yn + bonus * v_s[...]) * gate).astype(ob_ref.dtype)

    @pl.when(ti == n_t - 1)
    def _():
        zout_ref[0] = z_scr[...]


def _rwkv(pr, shift0, z0, p, *, tt, chunk, t_valid):
    b, t, _ = pr.shape
    assert t % tt == 0 and tt % chunk == 0
    consts = (p["shift_mu"], p["w0"], p["w_lora_up"], p["a0"], p["a_lora_up"], p["g_lora_up"], p["k_k"], p["k_a"],
              p["r_k"], p["ln_x_w"], p["ln_x_b"], p["ge"], p["get"])
    kern = functools.partial(_rwkv_kernel, tt=tt, chunk=chunk, t_valid=t_valid)
    wide = lambda: pltpu.VMEM((tt, RWKV_W), F32)
    return pl.pallas_call(
        kern,
        grid=(b, t // tt),
        in_specs=[
            pl.BlockSpec((1, tt, RWKV_PROJ_W), lambda bi, ti: (bi, ti, 0)),
            pl.BlockSpec((1, 1, RWKV_PROJ_W), lambda bi, ti: (bi, 0, 0)),
            pl.BlockSpec((1, N_RWKV_HEADS, RWKV_HEAD, RWKV_HEAD), lambda bi, ti: (bi, 0, 0, 0)),
        ] + [_const_spec(c.shape) for c in consts],
        out_specs=[
            pl.BlockSpec((1, tt, RWKV_W), lambda bi, ti: (bi, ti, 0)),
            pl.BlockSpec((1, N_RWKV_HEADS, RWKV_HEAD, RWKV_HEAD), lambda bi, ti: (bi, 0, 0, 0)),
        ],
        out_shape=[jax.ShapeDtypeStruct((b, t, RWKV_W), BF16),
                   jax.ShapeDtypeStruct((b, N_RWKV_HEADS, RWKV_HEAD, RWKV_HEAD), F32)],
        scratch_shapes=[pltpu.VMEM((N_RWKV_HEADS, RWKV_HEAD, RWKV_HEAD), F32),
                        pltpu.VMEM((1, RWKV_PROJ_W), F32),
                        wide(), wide(), wide(), wide(), wide(), wide(), wide()],
        compiler_params=pltpu.CompilerParams(dimension_semantics=("parallel", "arbitrary"),
                                             vmem_limit_bytes=VMEM_LIMIT),
        name="rwkv",
    )(pr, shift0, z0, *consts)


def _merge_kernel(x_ref, oa_ref, ob_ref, ga_ref, gb_ref, wa_ref, wb_ref, wo_ref, nf_ref, rwt_ref, rb_ref,
                  h_out, hn_out, idx_out, gate_out):
    ma = _dot(oa_ref[...], wa_ref[...])
    mb = _dot(ob_ref[...], wb_ref[...])
    mm = ga_ref[...].astype(F32) * ma + gb_ref[...].astype(F32) * mb
    h = x_ref[...] + _dot(mm.astype(BF16), wo_ref[...])
    h_out[...] = h
    ms = jnp.mean(h * h, axis=-1, keepdims=True)
    hn = h * lax.rsqrt(ms + NORM_EPS) * nf_ref[...]
    hn_out[...] = hn.astype(hn_out.dtype)
    logits = _mm(_split(rwt_ref[...]), _split(hn), _dot_nt) + rb_ref[...]
    tm = logits.shape[1]
    expert = lax.broadcasted_iota(jnp.int32, logits.shape, 0)
    vals, idxs = [], []
    for _ in range(TOP_K):
        mx = jnp.max(logits, axis=0, keepdims=True)
        ix = jnp.min(jnp.where(logits == mx, expert, N_EXPERTS), axis=0, keepdims=True)
        vals.append(mx)
        idxs.append(ix)
        logits = jnp.where(expert == ix, -jnp.inf, logits)
    es = [jnp.exp(v - vals[0]) for v in vals]
    den = es[0] + es[1] + es[2] + es[3]
    pad = 8 - TOP_K
    idx_out[...] = jnp.concatenate(idxs + [jnp.zeros((pad, tm), jnp.int32)], axis=0)
    gate_out[...] = jnp.concatenate([e / den for e in es] + [jnp.zeros((pad, tm), F32)], axis=0)


def _merge(x2d, oa, ob, ga, gb, p, tm):
    n = x2d.shape[0]
    row = lambda w: pl.BlockSpec((tm, w), lambda i: (i, 0))
    consts = (p["w_proj_a"], p["w_proj_b"], p["w_out"], p["norm_ffn"], p["router_w"], p["router_b"])
    return pl.pallas_call(
        _merge_kernel,
        grid=(n // tm,),
        in_specs=[row(D_MODEL), row(ATTN_W), row(RWKV_W), row(D_MODEL), row(D_MODEL)]
        + [_const_spec(c.shape) for c in consts],
        out_specs=[row(D_MODEL), row(D_MODEL)] + [pl.BlockSpec((8, tm), lambda i: (0, i))] * 2,
        out_shape=[jax.ShapeDtypeStruct((n, D_MODEL), F32), jax.ShapeDtypeStruct((n, D_MODEL), BF16),
                   jax.ShapeDtypeStruct((8, n), jnp.int32), jax.ShapeDtypeStruct((8, n), F32)],
        compiler_params=pltpu.CompilerParams(dimension_semantics=("parallel",), vmem_limit_bytes=VMEM_LIMIT),
        name="merge",
    )(x2d, oa, ob, ga, gb, *consts)


def _moe_kernel(be_ref, nb_ref, x_ref, wgu_ref, bgu_ref, wd_ref, bd_ref, o_ref, wgu_s, wd_s):
    i = pl.program_id(0)
    used = i < nb_ref[0]

    @pl.when(used & ((i == 0) | (be_ref[i] != be_ref[jnp.maximum(i - 1, 0)])))
    def _():
        wgu_s[...] = wgu_ref[0].astype(wgu_s.dtype)
        wd_s[...] = wd_ref[0].astype(wd_s.dtype)

    @pl.when(used)
    def _():
        hcat = _dot(x_ref[...], wgu_s[...]) + bgu_ref[0]
        glu = jnp.minimum(hcat[:, :D_FF], SWIGLU_LIMIT)
        lin = jnp.clip(hcat[:, D_FF:], -SWIGLU_LIMIT, SWIGLU_LIMIT)
        act = glu * jax.nn.sigmoid(SWIGLU_ALPHA * glu) * (lin + 1.0)
        o_ref[...] = _dot(act.astype(BF16), wd_s[...]) + bd_ref[0]

    @pl.when(jnp.logical_not(used))
    def _():
        o_ref[...] = jnp.zeros(o_ref.shape, o_ref.dtype)


def _moe_rows(xg, block_exp, n_used, p, bm):
    rows = xg.shape[0]
    grid_spec = pltpu.PrefetchScalarGridSpec(
        num_scalar_prefetch=2,
        grid=(rows // bm,),
        in_specs=[
            pl.BlockSpec((bm, D_MODEL), lambda i, be, nb: (i, 0)),
            pl.BlockSpec((1, D_MODEL, 2 * D_FF), lambda i, be, nb: (be[i], 0, 0)),
            pl.BlockSpec((1, 1, 2 * D_FF), lambda i, be, nb: (be[i], 0, 0)),
            pl.BlockSpec((1, D_FF, D_MODEL), lambda i, be, nb: (be[i], 0, 0)),
            pl.BlockSpec((1, 1, D_MODEL), lambda i, be, nb: (be[i], 0, 0)),
        ],
        out_specs=pl.BlockSpec((bm, D_MODEL), lambda i, be, nb: (i, 0)),
        scratch_shapes=[pltpu.VMEM((D_MODEL, 2 * D_FF), BF16), pltpu.VMEM((D_FF, D_MODEL), BF16)],
    )
    return pl.pallas_call(
        _moe_kernel,
        grid_spec=grid_spec,
        out_shape=jax.ShapeDtypeStruct((rows, D_MODEL), F32),
        compiler_params=pltpu.CompilerParams(dimension_semantics=("arbitrary",), vmem_limit_bytes=VMEM_LIMIT),
        name="moe",
    )(block_exp, n_used, xg, p["w_gate_up"], p["b_gate_up"], p["w_down"], p["b_down"])


def _moe(hn, top_idx, gate, p, bm):
    n = hn.shape[0]
    nk = n * TOP_K
    e_flat = top_idx.reshape(-1)
    onehot = (e_flat[:, None] == jnp.arange(N_EXPERTS, dtype=jnp.int32)[None, :]).astype(jnp.int32)
    csum = jnp.cumsum(onehot, axis=0)
    counts = csum[-1]
    rank = jnp.take_along_axis(csum, e_flat[:, None], axis=1)[:, 0] - 1
    padded = (counts + bm - 1) // bm * bm
    pends = jnp.cumsum(padded)
    pstarts = pends - padded
    dest = pstarts[e_flat] + rank
    n_blocks = -(-nk // bm) + N_EXPERTS
    rows = n_blocks * bm
    block_start = jnp.arange(n_blocks, dtype=jnp.int32) * bm
    block_exp = jnp.minimum(jnp.sum((pends[None, :] <= block_start[:, None]).astype(jnp.int32), axis=1),
                            N_EXPERTS - 1)
    order = jnp.argsort(e_flat, stable=True).astype(jnp.int32)
    e_row = jnp.repeat(block_exp, bm)
    j_row = jnp.arange(rows, dtype=jnp.int32) - pstarts[e_row]
    src = jnp.clip((jnp.cumsum(counts) - counts)[e_row] + j_row, 0, nk - 1)
    row_tok = jnp.where(j_row < counts[e_row], order[src] % n, 0)
    n_used = (pends[-1:] // bm).astype(jnp.int32)
    out = _moe_rows(hn[row_tok], block_exp, n_used, p, bm)
    return (out[dest.reshape(TOP_K, n)] * gate[:, :, None]).sum(axis=0)


def _prep_params(norm_mix, w_in, q_norm, k_norm, idx_k_norm, shift_mu, w0, w_lora_up, a0, a_lora_up, g_lora_up, k_k,
                 k_a, r_k, ln_x_w, ln_x_b, w_proj_a, w_proj_b, w_out, norm_ffn, router_w, router_b, w_gate_up,
                 b_gate_up, w_down, b_down):
    splits = (ATTN_W, KV_W, KV_W, IDX_W, D_IDX, N_IDX_HEADS, RWKV_PROJ_W, D_MODEL, D_MODEL)
    cuts = np.cumsum(splits)[:-1].tolist()
    wq, wk, wv, wiq, wik, wiw, wpr, wga, wgb = jnp.split(w_in.astype(BF16), cuts, axis=-1)
    wikw = jnp.concatenate([wik, wiw, jnp.zeros((D_MODEL, LANES - D_IDX - N_IDX_HEADS), BF16)], axis=-1)
    row = lambda z: z.reshape(1, -1).astype(F32)
    ge = _group_indicator(RWKV_W, RWKV_HEAD).astype(BF16)
    ge2 = _group_indicator(KV_W, HEAD_DIM).astype(BF16)
    return dict(
        norm_mix=row(norm_mix), wq=wq, wk=wk, wv=wv, wiq=wiq, wikw=wikw, wpr=wpr, wga=wga, wgb=wgb,
        q_norm_t=row(jnp.tile(q_norm, N_Q_HEADS)), k_norm_t=row(jnp.tile(k_norm, N_KV_HEADS)),
        ik_norm_t=row(idx_k_norm),
        ge=ge, get=ge.T, ge2=ge2, ge2t=ge2.T,
        shift_mu=row(shift_mu), w0=row(w0), w_lora_up=w_lora_up, a0=row(a0), a_lora_up=a_lora_up,
        g_lora_up=g_lora_up, k_k=row(k_k), k_a=row(k_a), r_k=row(r_k), ln_x_w=row(ln_x_w), ln_x_b=row(ln_x_b),
        w_proj_a=w_proj_a.astype(BF16), w_proj_b=w_proj_b.astype(BF16), w_out=w_out.astype(BF16),
        norm_ffn=row(norm_ffn), router_w=router_w.T, router_b=router_b.reshape(-1, 1),
        w_gate_up=w_gate_up, b_gate_up=b_gate_up[:, None, :], w_down=w_down, b_down=b_down[:, None, :],
    )


def _pad_axis(z, axis, size):
    if z.shape[axis] == size:
        return z
    pad = [(0, 0)] * z.ndim
    pad[axis] = (0, size - z.shape[axis])
    return jnp.pad(z, pad)


def _group(x, p, *, tm):
    b, t, _ = x.shape
    names = ("q", "k", "kb", "v", "vb", "iq", "ik", "ikb", "iw", "pr", "ga", "gb")
    g = dict(zip(names, _in_proj(x.reshape(b * t, D_MODEL), p, tm)), b=b, t=t)
    g["pr"] = g["pr"].reshape(b, t, RWKV_PROJ_W)
    return g


def _attend(g, ik_all, k_all, v_all, *, tq, tk, q_offset, top):
    b, t = g["b"], g["t"]
    tp = -(-t // tq) * tq
    n_keys = -(-ik_all.shape[1] // tk) * tk
    seq = lambda z: _pad_axis(z.reshape(b, t, -1), 1, tp)
    keys = lambda z: _pad_axis(z, 1, n_keys)
    oa = _dsa(seq(g["iq"]), seq(g["iw"]), seq(g["q"]), keys(ik_all), keys(k_all), keys(v_all),
              tq=tq, tk=tk, q_offset=q_offset, top=top)
    return oa[:, :t].reshape(b * t, ATTN_W)


def _mix(g, shift0, wkv0, p, *, tt, chunk):
    b, t = g["b"], g["t"]
    tp = -(-t // tt) * tt
    ob, z = _rwkv(_pad_axis(g["pr"], 1, tp), shift0[:, None, :], jnp.swapaxes(wkv0, -1, -2), p,
                  tt=tt, chunk=chunk, t_valid=min(t, tt) if tp != t else tt)
    return ob[:, :t].reshape(b * t, RWKV_W), jnp.swapaxes(z, -1, -2)


def kernel(x_prompt, x_sample, cache_k, cache_v, cache_idx_k, page_table, state_wkv, state_shift, norm_mix, w_in, q_norm, k_norm, idx_k_norm, shift_mu, w0, w_lora_up, a0, a_lora_up, g_lora_up, k_k, k_a, r_k, ln_x_w, ln_x_b, w_proj_a, w_proj_b, w_out, norm_ffn, router_w, router_b, w_gate_up, b_gate_up, w_down, b_down):
    depth = norm_mix.shape[0]
    assert depth == 1
    params = (norm_mix, w_in, q_norm, k_norm, idx_k_norm, shift_mu, w0, w_lora_up, a0, a_lora_up, g_lora_up, k_k, k_a,
              r_k, ln_x_w, ln_x_b, w_proj_a, w_proj_b, w_out, norm_ffn, router_w, router_b, w_gate_up, b_gate_up,
              w_down, b_down)
    p = _prep_params(*[z[0] for z in params])
    bp, sp, _ = x_prompt.shape
    bs, ts, _ = x_sample.shape
    n_p, n_s = bp * sp, bs * ts
    past = page_table.shape[1] * PAGE_SIZE

    gp = _group(x_prompt, p, tm=min(256, n_p))
    k_p = gp["k"].reshape(bp, sp, N_KV_HEADS, HEAD_DIM)
    v_p = gp["v"].reshape(bp, sp, N_KV_HEADS, HEAD_DIM)
    ik_p = gp["ik"].reshape(bp, sp, D_IDX)
    oa_p = _attend(gp, gp["ikb"].reshape(bp, sp, D_IDX), gp["kb"].reshape(bp, sp, KV_W),
                   gp["vb"].reshape(bp, sp, KV_W), tq=min(128, sp), tk=min(512, sp), q_offset=0,
                   top=min(TOPK_MAX, sp // 4))
    ob_p, wkv_p = _mix(gp, jnp.zeros((bp, RWKV_PROJ_W), F32),
                       jnp.zeros((bp, N_RWKV_HEADS, RWKV_HEAD, RWKV_HEAD), F32), p,
                       tt=min(256, sp), chunk=min(64, sp))

    gs = _group(x_sample, p, tm=min(256, n_s))
    k_s = gs["k"].reshape(bs, ts, N_KV_HEADS, HEAD_DIM)
    v_s = gs["v"].reshape(bs, ts, N_KV_HEADS, HEAD_DIM)
    ik_s = gs["ik"].reshape(bs, ts, D_IDX)
    tq_s = 16
    seq = lambda z: _pad_axis(z.reshape(bs, ts, -1), 1, tq_s)
    new_t = lambda z, *hd: _pad_axis(jnp.moveaxis(z.reshape((bs, ts) + hd), 1, -1), len(hd) + 1, PAGE_SIZE)
    oa_s = _dsa_paged(seq(gs["iq"]), seq(gs["iw"]), seq(gs["q"]),
                      new_t(gs["ikb"], D_IDX), new_t(gs["kb"], N_KV_HEADS, HEAD_DIM),
                      new_t(gs["vb"], N_KV_HEADS, HEAD_DIM),
                      jnp.transpose(cache_idx_k[0], (0, 2, 1)), jnp.transpose(cache_k[0], (0, 2, 3, 1)),
                      jnp.transpose(cache_v[0], (0, 2, 3, 1)), page_table,
                      tq=tq_s, top=min(TOPK_MAX, (past + ts) // 4))[:, :ts].reshape(n_s, ATTN_W)
    ob_s, wkv_s = _mix(gs, state_shift[0], state_wkv[0], p, tt=8, chunk=8)

    h_p, hn_p, idx_p, gate_p = _merge(x_prompt.reshape(n_p, D_MODEL), oa_p, ob_p, gp["ga"], gp["gb"], p,
                                      tm=min(256, n_p))
    h_s, hn_s, idx_s, gate_s = _merge(x_sample.reshape(n_s, D_MODEL), oa_s, ob_s, gs["ga"], gs["gb"], p,
                                      tm=min(256, n_s))
    hn = jnp.concatenate([hn_p, hn_s], axis=0)
    top_idx = jnp.concatenate([idx_p[:TOP_K], idx_s[:TOP_K]], axis=1)
    gate = jnp.concatenate([gate_p[:TOP_K], gate_s[:TOP_K]], axis=1)
    f = _moe(hn, top_idx, gate, p, bm=512)
    y_p = (h_p + f[:n_p]).reshape(bp, sp, D_MODEL)
    y_s = (h_s + f[n_p:]).reshape(bs, ts, D_MODEL)

    st = lambda z: z[None]
    return (y_p, y_s, st(k_p), st(v_p), st(ik_p), st(wkv_p), st(gp["pr"][:, -1]),
            st(k_s), st(v_s), st(ik_s), st(wkv_s), st(gs["pr"][:, -1]))
```
